```python
import math
import jax, jax.numpy as jnp
from jax import lax
import numpy as np

D_MODEL = 2048
BATCH = 2
SEQ = 8192
DEPTH = 4
DEC_BATCH = 8
DEC_SEQ = 2048
PAST_LEN = 128

D_MIX = D_MODEL
M_HEADS = 4
M_WIDTH = D_MIX // 2
M_HEAD_DIM = M_WIDTH // M_HEADS
CHUNK = 128
CONV_K = 5
POOL_WINDOWS = (2, 4, 8, 16)
POOL_WIDTH = D_MIX // 4
POOL_GROUP = POOL_WIDTH // len(POOL_WINDOWS)
X_HEADS = 4
X_WIDTH = D_MIX // 4
X_HEAD_DIM = X_WIDTH // X_HEADS
N_MEM = 256
OFF_Q = 0
OFF_K = OFF_Q + M_WIDTH
OFF_V = OFF_K + M_WIDTH
OFF_O = OFF_V + M_WIDTH
OFF_P = OFF_O + M_WIDTH
OFF_XQ = OFF_P + POOL_WIDTH
OFF_G = OFF_XQ + X_WIDTH
N_GATES = 4 * M_HEADS
IN_WIDTH = OFF_G + N_GATES
N_EXPERTS = 64
TOP_K = 8
N_GROUPS = 8
TOPK_GROUPS = 4
D_EXPERT = 512
D_SHARED = 512
ROUTED_SCALE = 2.5
ROW_BLOCK = 128
ALPHA = (2 * DEPTH) ** 0.25
BETA = (8 * DEPTH) ** -0.25
LN_EPS = 1e-5
HN_EPS = 1e-6

kernel_name = 'hybrid_mlstm_pool_memattn_moe_encoder'

F32 = jnp.float32


def layer_norm(x, g, b):
    xf = x.astype(F32)
    mu = xf.mean(-1, keepdims=True)
    var = jnp.square(xf - mu).mean(-1, keepdims=True)
    return ((xf - mu) * lax.rsqrt(var + LN_EPS) * g + b).astype(x.dtype)


def centred_dwconv(u, w, b):
    S = u.shape[1]
    pad = CONV_K // 2
    up = jnp.pad(u, ((0, 0), (pad, pad), (0, 0)))
    out = b
    for j in range(CONV_K):
        out = out + up[:, j:j + S] * w[j]
    return out


def mlstm_scan(q, k, v, li, lf):
    B, H, S, DK = q.shape
    DV = v.shape[-1]
    NC = S // CHUNK

    def to_chunks(a):
        return jnp.moveaxis(a.reshape((B, H, NC, CHUNK) + a.shape[3:]), 2, 0)

    causal = jnp.tril(jnp.ones((CHUNK, CHUNK), bool))

    def step(carry, inp):
        C, n, m = carry
        qc, kc, vc, ic, fc = inp
        b = jnp.cumsum(fc, axis=-1)
        dmat = jnp.where(causal, b[..., :, None] - b[..., None, :] + ic[..., None, :], -jnp.inf)
        m_t = jnp.maximum(b + m[..., None], dmat.max(-1))
        a = jnp.einsum('bhtk,bhsk->bhts', qc, kc) * jnp.exp(dmat - m_t[..., None])
        inter = jnp.exp(b + m[..., None] - m_t)
        num = jnp.einsum('bhts,bhsv->bhtv', a, vc) + inter[..., None] * jnp.einsum('bhtk,bhkv->bhtv', qc, C)
        den = a.sum(-1) + inter * jnp.einsum('bhtk,bhk->bht', qc, n)
        h = num / jnp.maximum(jnp.abs(den), jnp.exp(-m_t))[..., None]
        g = b[..., -1]
        lw = g[..., None] - b + ic
        m_new = jnp.maximum(g + m, lw.max(-1))
        wk = jnp.exp(lw - m_new[..., None])
        decay = jnp.exp(g + m - m_new)
        C = decay[..., None, None] * C + jnp.einsum('bhsk,bhsv->bhkv', kc * wk[..., None], vc)
        n = decay[..., None] * n + jnp.einsum('bhsk,bhs->bhk', kc, wk)
        return (C, n, m_new), h

    init = (jnp.zeros((B, H, DK, DV), F32), jnp.zeros((B, H, DK), F32), jnp.zeros((B, H), F32))
    _, hs = lax.scan(step, init, (to_chunks(q), to_chunks(k), to_chunks(v), to_chunks(li), to_chunks(lf)))
    return jnp.moveaxis(hs, 0, 2).reshape(B, H, S, DV)


def mlstm_mixer(u_qk, u_v, u_o, gates, conv_w, conv_b, norm_g):
    B, S, _ = u_v.shape
    qk = jax.nn.silu(centred_dwconv(u_qk, conv_w, conv_b))

    def heads(a):
        return a.reshape(B, S, M_HEADS, M_HEAD_DIM).transpose(0, 2, 1, 3).astype(F32)

    q = heads(qk[..., :M_WIDTH])
    k = heads(qk[..., M_WIDTH:]) * (M_HEAD_DIM ** -0.5)
    v = heads(u_v)
    gt = gates.astype(F32).transpose(0, 2, 1)
    i_f, f_f, i_b, f_b = jnp.split(gt, 4, axis=1)
    h_f = mlstm_scan(q, k, v, i_f, jax.nn.log_sigmoid(f_f))
    fl = lambda a: jnp.flip(a, axis=2)
    h_b = fl(mlstm_scan(fl(q), fl(k), fl(v), fl(i_b), jax.nn.log_sigmoid(fl(f_b))))
    h = h_f + h_b
    mu = h.mean(-1, keepdims=True)
    var = jnp.square(h - mu).mean(-1, keepdims=True)
    h = (h - mu) * lax.rsqrt(var + HN_EPS)
    h = h.transpose(0, 2, 1, 3).reshape(B, S, M_WIDTH) * norm_g
    return (h * jax.nn.sigmoid(u_o.astype(F32))).astype(u_v.dtype)


def pool_mixer(u, pool_w, pool_scale):
    B, S, _ = u.shape
    uf = u.astype(F32)
    cs = jnp.pad(jnp.cumsum(uf, axis=1), ((0, 0), (1, 0), (0, 0)))
    t = jnp.arange(S)
    outs = []
    for gi, w in enumerate(POOL_WINDOWS):
        lo = jnp.maximum(t - w // 2, 0)
        hi1 = jnp.minimum(t + w // 2, S)
        sl = slice(gi * POOL_GROUP, (gi + 1) * POOL_GROUP)
        csg = cs[..., sl]
        mean = (csg[:, hi1] - csg[:, lo]) / (hi1 - lo).astype(F32)[:, None]
        outs.append(jnp.einsum('bsc,cd->bsd', mean - uf[..., sl], pool_w[gi].astype(F32)))
    return (jnp.concatenate(outs, -1) * pool_scale).astype(u.dtype)


def memory_attention(u_q, mem, w_mem_kv):
    B, S, _ = u_q.shape
    kv = mem @ w_mem_kv
    k = kv[..., :X_WIDTH].reshape(B, -1, X_HEADS, X_HEAD_DIM)
    v = kv[..., X_WIDTH:].reshape(B, -1, X_HEADS, X_HEAD_DIM)
    q = u_q.reshape(B, S, X_HEADS, X_HEAD_DIM)
    s = jnp.einsum('bshd,bmhd->bhsm', q, k).astype(F32) * (X_HEAD_DIM ** -0.5)
    p = jax.nn.softmax(s, axis=-1).astype(v.dtype)
    return jnp.einsum('bhsm,bmhd->bshd', p, v).reshape(B, S, X_WIDTH)


def moe_ffn(x, w_router, b_router, w_gu, w_dn, w_sh_gu, w_sh_dn):
    T, D = x.shape
    s = jax.nn.sigmoid((x @ w_router).astype(F32))
    sel = s + b_router.astype(F32)
    grp_score = lax.top_k(sel.reshape(T, N_GROUPS, N_EXPERTS // N_GROUPS), 2)[0].sum(-1)
    _, top_g = lax.top_k(grp_score, TOPK_GROUPS)
    g_mask = (top_g[:, :, None] == jnp.arange(N_GROUPS)).any(1)
    e_mask = jnp.repeat(g_mask, N_EXPERTS // N_GROUPS, axis=1)
    _, top_e = lax.top_k(jnp.where(e_mask, sel, -jnp.inf), TOP_K)
    gate = jnp.take_along_axis(s, top_e, axis=1)
    gate = gate / gate.sum(-1, keepdims=True) * ROUTED_SCALE
    TK = T * TOP_K
    flat_e = top_e.reshape(TK)
    order = jnp.argsort(flat_e)
    se = flat_e[order]
    counts = jnp.bincount(flat_e, length=N_EXPERTS)
    padded = (counts + ROW_BLOCK - 1) // ROW_BLOCK * ROW_BLOCK
    starts = jnp.cumsum(counts) - counts
    pends = jnp.cumsum(padded)
    dest = (pends - padded)[se] + jnp.arange(TK) - starts[se]
    n_blocks = -(-TK // ROW_BLOCK) + N_EXPERTS
    P = n_blocks * ROW_BLOCK
    row_tok = jnp.full((P,), T, jnp.int32).at[dest].set((order // TOP_K).astype(jnp.int32))
    row_gate = jnp.zeros((P,), F32).at[dest].set(gate.reshape(TK)[order])
    blk_e = jnp.minimum(jnp.searchsorted(pends, jnp.arange(n_blocks) * ROW_BLOCK, side='right'), N_EXPERTS - 1)
    x_ext = jnp.concatenate([x, jnp.zeros((1, D), x.dtype)], axis=0)

    def expert_block(acc, blk):
        e, toks, gw = blk
        gu = x_ext[toks] @ w_gu[e]
        g_, u_ = jnp.split(gu, 2, axis=-1)
        y = (jax.nn.silu(g_) * u_) @ w_dn[e]
        return acc.at[toks].add(y.astype(F32) * gw[:, None]), None

    acc, _ = lax.scan(expert_block, jnp.zeros((T + 1, D), F32),
                      (blk_e, row_tok.reshape(n_blocks, ROW_BLOCK), row_gate.reshape(n_blocks, ROW_BLOCK)))
    sg, su = jnp.split(x @ w_sh_gu, 2, axis=-1)
    shared = (jax.nn.silu(sg) * su) @ w_sh_dn
    return (acc[:T] + shared.astype(F32)).astype(x.dtype)


def encoder_layer(x, mem, w_in, b_gate, conv_w, conv_b, mh_norm_g, pool_w, pool_scale, w_mem_kv, w_out,
                  ln1_g, ln1_b, w_router, b_router, w_gu, w_dn, w_sh_gu, w_sh_dn, ln2_g, ln2_b):
    B, S, D = x.shape
    hin = x @ w_in
    m_out = mlstm_mixer(hin[..., OFF_Q:OFF_V], hin[..., OFF_V:OFF_O], hin[..., OFF_O:OFF_P],
                        hin[..., OFF_G:] + b_gate, conv_w, conv_b, mh_norm_g)
    p_out = pool_mixer(hin[..., OFF_P:OFF_XQ], pool_w, pool_scale)
    a_out = memory_attention(hin[..., OFF_XQ:OFF_G], mem, w_mem_kv)
    mix = jnp.concatenate([m_out, p_out, a_out], axis=-1) @ w_out
    x = layer_norm(ALPHA * x + mix, ln1_g, ln1_b)
    ffn = moe_ffn(x.reshape(B * S, D), w_router, b_router, w_gu, w_dn, w_sh_gu, w_sh_dn).reshape(B, S, D)
    return layer_norm(ALPHA * x + ffn, ln2_g, ln2_b)


def encoder_trunk(x, mem, ln_in_g, ln_in_b, w_in, b_gate, conv_w, conv_b, mh_norm_g, pool_w, pool_scale,
                  w_mem_kv, w_out, ln1_g, ln1_b, w_router, b_router, w_gu, w_dn, w_sh_gu, w_sh_dn, ln2_g, ln2_b):
    x = layer_norm(x, ln_in_g, ln_in_b)
    for l in range(DEPTH):
        x = encoder_layer(x, mem, w_in[l], b_gate[l], conv_w[l], conv_b[l], mh_norm_g[l], pool_w[l],
                          pool_scale[l], w_mem_kv[l], w_out[l], ln1_g[l], ln1_b[l], w_router[l], b_router[l],
                          w_gu[l], w_dn[l], w_sh_gu[l], w_sh_dn[l], ln2_g[l], ln2_b[l])
    return x


def setup_inputs(seed: int = 0) -> dict:
    key = jax.random.key(seed)
    ks = list(jax.random.split(key, 40))

    def nrm(i, shape, scale):
        return jax.random.normal(ks[i], shape, F32) * scale

    D = D_MODEL
    w_in = nrm(6, (DEPTH, D, IN_WIDTH), D ** -0.5)
    w_in = w_in.at[:, :, OFF_V:OFF_O].multiply(BETA)
    f_lin = jnp.linspace(3.0, 6.0, M_HEADS, dtype=F32)[None]
    b_gate = jnp.concatenate([nrm(7, (DEPTH, M_HEADS), 0.1), f_lin + nrm(8, (DEPTH, M_HEADS), 0.1),
                              nrm(9, (DEPTH, M_HEADS), 0.1), f_lin + nrm(10, (DEPTH, M_HEADS), 0.1)], axis=-1)
    w_mem_kv = nrm(16, (DEPTH, D, 2 * X_WIDTH), D ** -0.5)
    w_mem_kv = w_mem_kv.at[:, :, X_WIDTH:].multiply(BETA)
    return {
        'x_prompt': nrm(0, (BATCH, SEQ, D), 1.0),
        'x_sample': nrm(1, (DEC_BATCH, DEC_SEQ, D), 1.0),
        'mem_prompt': nrm(2, (BATCH, N_MEM, D), 1.0),
        'mem_sample': nrm(3, (DEC_BATCH, N_MEM, D), 1.0),
        'ln_in_g': 1.0 + nrm(4, (D,), 0.02),
        'ln_in_b': nrm(5, (D,), 0.02),
        'w_in': w_in,
        'b_gate': b_gate,
        'conv_w': nrm(11, (DEPTH, CONV_K, 2 * M_WIDTH), CONV_K ** -0.5),
        'conv_b': nrm(12, (DEPTH, 2 * M_WIDTH), 0.02),
        'mh_norm_g': 1.0 + nrm(13, (DEPTH, M_WIDTH), 0.02),
        'pool_w': nrm(14, (DEPTH, len(POOL_WINDOWS), POOL_GROUP, POOL_GROUP), POOL_GROUP ** -0.5),
        'pool_scale': 1.0 + nrm(15, (DEPTH, POOL_WIDTH), 0.02),
        'w_mem_kv': w_mem_kv,
        'w_out': nrm(17, (DEPTH, D_MIX, D), BETA * D_MIX ** -0.5),
        'ln1_g': 1.0 + nrm(18, (DEPTH, D), 0.02),
        'ln1_b': nrm(19, (DEPTH, D), 0.02),
        'w_router': nrm(20, (DEPTH, D, N_EXPERTS), D ** -0.5),
        'b_router': nrm(21, (DEPTH, N_EXPERTS), 0.01),
        'w_gu': nrm(22, (DEPTH, N_EXPERTS, D, 2 * D_EXPERT), D ** -0.5),
        'w_dn': nrm(23, (DEPTH, N_EXPERTS, D_EXPERT, D), BETA * D_EXPERT ** -0.5),
        'w_sh_gu': nrm(24, (DEPTH, D, 2 * D_SHARED), D ** -0.5),
        'w_sh_dn': nrm(25, (DEPTH, D_SHARED, D), BETA * D_SHARED ** -0.5),
        'ln2_g': 1.0 + nrm(26, (DEPTH, D), 0.02),
        'ln2_b': nrm(27, (DEPTH, D), 0.02),
    }


def reference(x_prompt, x_sample, mem_prompt, mem_sample, ln_in_g, ln_in_b, w_in, b_gate, conv_w, conv_b,
              mh_norm_g, pool_w, pool_scale, w_mem_kv, w_out, ln1_g, ln1_b, w_router, b_router, w_gu, w_dn,
              w_sh_gu, w_sh_dn, ln2_g, ln2_b):
    y_prompt = encoder_trunk(x_prompt, mem_prompt, ln_in_g, ln_in_b, w_in, b_gate, conv_w, conv_b, mh_norm_g,
                             pool_w, pool_scale, w_mem_kv, w_out, ln1_g, ln1_b, w_router, b_router, w_gu, w_dn,
                             w_sh_gu, w_sh_dn, ln2_g, ln2_b)
    y_sample = encoder_trunk(x_sample, mem_sample, ln_in_g, ln_in_b, w_in, b_gate, conv_w, conv_b, mh_norm_g,
                             pool_w, pool_scale, w_mem_kv, w_out, ln1_g, ln1_b, w_router, b_router, w_gu, w_dn,
                             w_sh_gu, w_sh_dn, ln2_g, ln2_b)
    return (y_prompt, y_sample)
```

```python
import functools

import jax
import jax.numpy as jnp
from jax import lax
from jax.experimental import pallas as pl
from jax.experimental.pallas import tpu as pltpu

F32 = jnp.float32
BF16 = jnp.bfloat16
I32 = jnp.int32

DEPTH = 4
D_MODEL = 2048
M_HEADS = 4
M_WIDTH = 1024
M_HEAD_DIM = 256
CHUNK = 128
CONV_K = 5
POOL_WINDOWS = (2, 4, 8, 16)
POOL_WIDTH = 512
POOL_GROUP = 128
X_HEADS = 4
X_WIDTH = 512
X_HEAD_DIM = 128
OFF_Q = 0
OFF_K = OFF_Q + M_WIDTH
OFF_V = OFF_K + M_WIDTH
OFF_O = OFF_V + M_WIDTH
OFF_P = OFF_O + M_WIDTH
OFF_XQ = OFF_P + POOL_WIDTH
OFF_G = OFF_XQ + X_WIDTH
N_GATES = 4 * M_HEADS
N_EXPERTS = 64
TOP_K = 8
N_GROUPS = 8
GROUP_SIZE = N_EXPERTS // N_GROUPS
TOPK_GROUPS = 4
D_EXPERT = 512
D_SHARED = 512
ROUTED_SCALE = 2.5
ALPHA = (2 * DEPTH) ** 0.25
LN_EPS = 1e-5
HN_EPS = 1e-6

LANES = 128
SUBLANES = 8
HALO = SUBLANES
VMEM_LIMIT = 56 * 1024 * 1024

LN_TM = 512
MM_TM = 512
MM_TN = 1024
SEQ_TM = 512
MLSTM_CB = 4
OUT_TM = 256
ROUTE_TM = 512
MOE_BM = 256
COMB_TM = 128


def _cparams(*sem):
    return pltpu.CompilerParams(dimension_semantics=sem, vmem_limit_bytes=VMEM_LIMIT)


class SeqInfo:
    def __init__(self, b1, s1, b2, s2):
        self.b1, self.s1, self.b2, self.s2 = b1, s1, b2, s2
        self.t1 = b1 * s1
        self.t = b1 * s1 + b2 * s2
        self.nseq = b1 + b2

    def pos_len(self, row0):
        in1 = row0 < self.t1
        pos = jnp.where(in1, lax.rem(row0, self.s1), lax.rem(row0 - self.t1, self.s2))
        slen = jnp.where(in1, self.s1, self.s2)
        return pos, slen

    def seq_id(self, row0):
        in1 = row0 < self.t1
        return jnp.where(in1, row0 // self.s1, self.b1 + (row0 - self.t1) // self.s2)


def _layer_norm(y, g, b, eps):
    mu = jnp.mean(y, axis=-1, keepdims=True)
    d = y - mu
    var = jnp.mean(d * d, axis=-1, keepdims=True)
    return d * lax.rsqrt(var + eps) * g + b


def _ln_kernel(x_ref, g_ref, b_ref, o_ref, ob_ref):
    y = _layer_norm(x_ref[...], g_ref[...], b_ref[...], LN_EPS)
    o_ref[...] = y
    ob_ref[...] = y.astype(BF16)


def _ln_call(x, g, b):
    t, d = x.shape
    tm = min(LN_TM, t)
    return pl.pallas_call(
        _ln_kernel,
        grid=(t // tm,),
        in_specs=[pl.BlockSpec((tm, d), lambda i: (i, 0)),
                  pl.BlockSpec((1, d), lambda i: (0, 0)),
                  pl.BlockSpec((1, d), lambda i: (0, 0))],
        out_specs=[pl.BlockSpec((tm, d), lambda i: (i, 0)),
                   pl.BlockSpec((tm, d), lambda i: (i, 0))],
        out_shape=[jax.ShapeDtypeStruct((t, d), F32), jax.ShapeDtypeStruct((t, d), BF16)],
        compiler_params=_cparams("parallel"),
        name="ln_in",
    )(x, g.reshape(1, d), b.reshape(1, d))


def _mm_kernel(x_ref, w_ref, o_ref):
    o_ref[...] = jnp.dot(x_ref[...], w_ref[...], preferred_element_type=F32).astype(o_ref.dtype)


def _mm_call(xb, wb, out_dtype, name):
    m, k = xb.shape
    n = wb.shape[1]
    tm = min(MM_TM, m)
    while m % tm:
        tm //= 2
    tn = min(MM_TN, n)
    return pl.pallas_call(
        _mm_kernel,
        grid=(n // tn, m // tm),
        in_specs=[pl.BlockSpec((tm, k), lambda j, i: (i, 0)),
                  pl.BlockSpec((k, tn), lambda j, i: (0, j))],
        out_specs=pl.BlockSpec((tm, tn), lambda j, i: (i, j)),
        out_shape=jax.ShapeDtypeStruct((m, n), out_dtype),
        compiler_params=_cparams("parallel", "parallel"),
        name=name,
    )(xb, wb)


def _gates_kernel(x_ref, w_ref, b_ref, o_ref):
    g = lax.dot_general(w_ref[...], x_ref[...], (((1,), (1,)), ((), ())),
                        preferred_element_type=F32) + b_ref[...]
    row = lax.broadcasted_iota(I32, g.shape, 0)
    is_forget = lax.rem(row // M_HEADS, 2) == 1
    o_ref[...] = jnp.where(is_forget, jax.nn.log_sigmoid(g), g)


def _gates_call(xb, wgt, bg):
    t, d = xb.shape
    tm = min(MM_TM, t)
    return pl.pallas_call(
        _gates_kernel,
        grid=(t // tm,),
        in_specs=[pl.BlockSpec((tm, d), lambda i: (i, 0)),
                  pl.BlockSpec((N_GATES, d), lambda i: (0, 0)),
                  pl.BlockSpec((N_GATES, 1), lambda i: (0, 0))],
        out_specs=pl.BlockSpec((N_GATES, tm), lambda i: (0, i)),
        out_shape=jax.ShapeDtypeStruct((N_GATES, t), F32),
        compiler_params=_cparams("parallel"),
        name="gates",
    )(xb, wgt, bg.reshape(N_GATES, 1))


def _halo_specs(tm, width, col_block, n_rows):
    per = tm // HALO
    last = n_rows // HALO - 1
    prev = pl.BlockSpec((HALO, width), lambda i, *_: (jnp.maximum(i * per - 1, 0), col_block))
    cur = pl.BlockSpec((tm, width), lambda i, *_: (i, col_block))
    nxt = pl.BlockSpec((HALO, width), lambda i, *_: (jnp.minimum((i + 1) * per, last), col_block))
    return prev, cur, nxt


def _extended_tile(prev_ref, cur_ref, next_ref, tm, seq):
    row0 = pl.program_id(0) * tm
    pos, slen = seq.pos_len(row0)
    prev = jnp.where(pos == 0, 0.0, prev_ref[...])
    nxt = jnp.where(pos + tm == slen, 0.0, next_ref[...])
    return jnp.concatenate([prev, cur_ref[...], nxt], axis=0), pos, slen


def _shift_rows(xe, d):
    n = xe.shape[0]
    if d == 0:
        return xe
    return pltpu.roll(xe, (n - d) % n, axis=0)


def _conv_kernel(prev_ref, cur_ref, next_ref, w_ref, b_ref, o_ref, *, tm, tc, seq):
    xe, _, _ = _extended_tile(prev_ref, cur_ref, next_ref, tm, seq)
    acc = jnp.broadcast_to(b_ref[...], (tm, tc))
    for j in range(CONV_K):
        sh = _shift_rows(xe, j - CONV_K // 2)
        acc = acc + sh[HALO:HALO + tm] * w_ref[j:j + 1, :]
    y = acc * jax.nn.sigmoid(acc)
    is_k = pl.program_id(1) * tc >= M_WIDTH
    scale = jnp.where(is_k, M_HEAD_DIM ** -0.5, 1.0).astype(F32)
    o_ref[...] = (y * scale).astype(BF16)


def _conv_call(hin, conv_w, conv_b, seq):
    t = hin.shape[0]
    tm = min(SEQ_TM, seq.s1, seq.s2)
    tc = 512
    width = 2 * M_WIDTH
    prev = pl.BlockSpec((HALO, tc), lambda i, j: (jnp.maximum(i * (tm // HALO) - 1, 0), j))
    cur = pl.BlockSpec((tm, tc), lambda i, j: (i, j))
    nxt = pl.BlockSpec((HALO, tc), lambda i, j: (jnp.minimum((i + 1) * (tm // HALO), t // HALO - 1), j))
    return pl.pallas_call(
        functools.partial(_conv_kernel, tm=tm, tc=tc, seq=seq),
        grid=(t // tm, width // tc),
        in_specs=[prev, cur, nxt,
                  pl.BlockSpec((CONV_K, tc), lambda i, j: (0, j)),
                  pl.BlockSpec((1, tc), lambda i, j: (0, j))],
        out_specs=pl.BlockSpec((tm, tc), lambda i, j: (i, j)),
        out_shape=jax.ShapeDtypeStruct((t, width), BF16),
        compiler_params=_cparams("parallel", "parallel"),
        name="conv_silu",
    )(hin, hin, hin, conv_w, conv_b.reshape(1, width))


def _lane_cumsum(x, reverse):
    lane = lax.broadcasted_iota(I32, x.shape, 1)
    s = 1
    while s < LANES:
        if reverse:
            x = x + jnp.where(lane < LANES - s, pltpu.roll(x, LANES - s, axis=1), 0.0)
        else:
            x = x + jnp.where(lane >= s, pltpu.roll(x, s, axis=1), 0.0)
        s *= 2
    return x


def _row_to_col(row, eye):
    return jnp.sum(jnp.where(eye, row, 0.0), axis=1, keepdims=True)


def _mlstm_kernel(*refs, reverse, finalize, cb, seq):
    if finalize:
        q_ref, k_ref, v_ref, g_ref, hf_ref, uo_ref, ng_ref, o_ref, c_ref, m_ref = refs
    else:
        q_ref, k_ref, v_ref, g_ref, o_ref, c_ref, m_ref = refs
    L, DK, DV = CHUNK, M_HEAD_DIM, M_HEAD_DIM
    step = pl.program_id(0)
    nsteps = pl.num_programs(0)
    blk = (nsteps - 1 - step) if reverse else step
    ti = lax.broadcasted_iota(I32, (L, L), 0)
    si = lax.broadcasted_iota(I32, (L, L), 1)
    causal = (si >= ti) if reverse else (si <= ti)
    eye = si == ti
    i_row0 = 2 * M_HEADS if reverse else 0
    f_row0 = i_row0 + M_HEADS

    def chunk_body(jj, carry):
        j = (cb - 1 - jj) if reverse else jj
        r0 = pl.multiple_of(j * L, L)
        pos, slen = seq.pos_len((blk * cb + j) * L)
        at_start = (pos + L == slen) if reverse else (pos == 0)

        @pl.when(at_start)
        def _():
            c_ref[...] = jnp.zeros_like(c_ref)
            m_ref[...] = jnp.zeros_like(m_ref)

        gates = g_ref[j]
        bsum = _lane_cumsum(gates, reverse)
        for h in range(M_HEADS):
            cols = slice(h * DK, (h + 1) * DK)
            q = q_ref[pl.ds(r0, L), cols]
            k = k_ref[pl.ds(r0, L), cols]
            v32 = v_ref[pl.ds(r0, L), cols]
            i_row = gates[i_row0 + h:i_row0 + h + 1, :]
            b_row = bsum[f_row0 + h:f_row0 + h + 1, :]
            b_col = _row_to_col(b_row, eye)
            m_prev = m_ref[h, 0:1, 0:1]
            dmat = jnp.where(causal, b_col - b_row + i_row, -jnp.inf)
            m_t = jnp.maximum(b_col + m_prev, jnp.max(dmat, axis=1, keepdims=True))
            s = lax.dot_general(q, k, (((1,), (1,)), ((), ())), preferred_element_type=F32)
            a = s * jnp.exp(dmat - m_t)
            inter = jnp.exp(b_col + m_prev - m_t)
            c_aug = c_ref[h]
            qc = jnp.dot(q, c_aug.astype(BF16), preferred_element_type=F32)
            num = jnp.dot(a.astype(BF16), v32.astype(BF16), preferred_element_type=F32) + inter * qc[:, :DV]
            den = jnp.sum(a, axis=1, keepdims=True) + inter * qc[:, DV:DV + 1]
            hh = num / jnp.maximum(jnp.abs(den), jnp.exp(-m_t))
            g_tot = b_row[:, 0:1] if reverse else b_row[:, L - 1:L]
            lw = g_tot - b_row + i_row
            m_new = jnp.maximum(g_tot + m_prev, jnp.max(lw, axis=1, keepdims=True))
            wk_col = _row_to_col(jnp.exp(lw - m_new), eye)
            decay = jnp.exp(g_tot + m_prev - m_new)
            v_aug = jnp.concatenate([v32 * wk_col, jnp.broadcast_to(wk_col, (L, LANES))], axis=1)
            upd = lax.dot_general(k, v_aug.astype(BF16), (((0,), (0,)), ((), ())),
                                  preferred_element_type=F32)
            c_ref[h] = decay * c_aug + upd
            m_ref[h] = jnp.broadcast_to(m_new, (SUBLANES, LANES))
            if finalize:
                hs = hf_ref[pl.ds(r0, L), cols] + hh
                mu = jnp.mean(hs, axis=1, keepdims=True)
                d = hs - mu
                var = jnp.mean(d * d, axis=1, keepdims=True)
                hn = d * lax.rsqrt(var + HN_EPS) * ng_ref[:, cols]
                o_ref[pl.ds(r0, L), cols] = (hn * jax.nn.sigmoid(uo_ref[pl.ds(r0, L), cols])).astype(o_ref.dtype)
            else:
                o_ref[pl.ds(r0, L), cols] = hh
        return carry

    lax.fori_loop(0, cb, chunk_body, 0)


def _mlstm_call(qk, hin, g3, hf, norm_g, seq, *, reverse):
    t = qk.shape[0]
    cb = MLSTM_CB
    while (seq.s1 // CHUNK) % cb or (seq.s2 // CHUNK) % cb:
        cb //= 2
    rows = cb * CHUNK
    nsteps = t // rows
    finalize = reverse
    blk = (lambda s: nsteps - 1 - s) if reverse else (lambda s: s)
    vblk = OFF_V // M_WIDTH
    oblk = OFF_O // M_WIDTH
    in_specs = [pl.BlockSpec((rows, M_WIDTH), lambda s: (blk(s), 0)),
                pl.BlockSpec((rows, M_WIDTH), lambda s: (blk(s), 1)),
                pl.BlockSpec((rows, M_WIDTH), lambda s: (blk(s), vblk)),
                pl.BlockSpec((cb, N_GATES, CHUNK), lambda s: (blk(s), 0, 0))]
    args = [qk, qk, hin, g3]
    if finalize:
        in_specs += [pl.BlockSpec((rows, M_WIDTH), lambda s: (blk(s), 0)),
                     pl.BlockSpec((rows, M_WIDTH), lambda s: (blk(s), oblk)),
                     pl.BlockSpec((1, M_WIDTH), lambda s: (0, 0))]
        args += [hf, hin, norm_g.reshape(1, M_WIDTH)]
    return pl.pallas_call(
        functools.partial(_mlstm_kernel, reverse=reverse, finalize=finalize, cb=cb, seq=seq),
        grid=(nsteps,),
        in_specs=in_specs,
        out_specs=pl.BlockSpec((rows, M_WIDTH), lambda s: (blk(s), 0)),
        out_shape=jax.ShapeDtypeStruct((t, M_WIDTH), BF16 if finalize else F32),
        scratch_shapes=[pltpu.VMEM((M_HEADS, M_HEAD_DIM, M_HEAD_DIM + LANES), F32),
                        pltpu.VMEM((M_HEADS, SUBLANES, LANES), F32)],
        compiler_params=_cparams("arbitrary"),
        name="mlstm_bwd" if reverse else "mlstm_fwd",
    )(*args)


def _pool_attn_kernel(prev_ref, cur_ref, next_ref, xq_ref, kv_ref, pw_ref, ps_ref, o_ref, *, tm, seq):
    xe, pos0, slen = _extended_tile(prev_ref, cur_ref, next_ref, tm, seq)
    pos = pos0 + lax.broadcasted_iota(I32, (tm, 1), 0)
    for gi, w in enumerate(POOL_WINDOWS):
        cols = slice(gi * POOL_GROUP, (gi + 1) * POOL_GROUP)
        u = xe[:, cols]
        win = _shift_rows(u, -1) + u
        half = 1
        while 2 * half < w:
            win = _shift_rows(win, -half) + _shift_rows(win, half)
            half *= 2
        cnt = (jnp.minimum(pos + w // 2, slen) - jnp.maximum(pos - w // 2, 0)).astype(F32)
        mean = win[HALO:HALO + tm] / cnt
        diff = (mean - u[HALO:HALO + tm]).astype(BF16)
        y = jnp.dot(diff, pw_ref[gi], preferred_element_type=F32)
        o_ref[:, cols] = (y * ps_ref[:, cols]).astype(o_ref.dtype)
    xq = xq_ref[...].astype(BF16)
    for h in range(X_HEADS):
        cols = slice(h * X_HEAD_DIM, (h + 1) * X_HEAD_DIM)
        kh = kv_ref[0, :, h * X_HEAD_DIM:(h + 1) * X_HEAD_DIM]
        vh = kv_ref[0, :, X_WIDTH + h * X_HEAD_DIM:X_WIDTH + (h + 1) * X_HEAD_DIM]
        s = lax.dot_general(xq[:, cols], kh, (((1,), (1,)), ((), ())),
                            preferred_element_type=F32) * (X_HEAD_DIM ** -0.5)
        e = jnp.exp(s - jnp.max(s, axis=1, keepdims=True))
        p = e / jnp.sum(e, axis=1, keepdims=True)
        y = jnp.dot(p.astype(BF16), vh, preferred_element_type=F32)
        o_ref[:, POOL_WIDTH + h * X_HEAD_DIM:POOL_WIDTH + (h + 1) * X_HEAD_DIM] = y.astype(o_ref.dtype)


def _pool_attn_call(hin, kv, pool_w, pool_scale, seq):
    t = hin.shape[0]
    tm = min(SEQ_TM, seq.s1, seq.s2)
    n_mem = kv.shape[1]
    prev, cur, nxt = _halo_specs(tm, POOL_WIDTH, OFF_P // POOL_WIDTH, t)
    return pl.pallas_call(
        functools.partial(_pool_attn_kernel, tm=tm, seq=seq),
        grid=(t // tm,),
        in_specs=[prev, cur, nxt,
                  pl.BlockSpec((tm, X_WIDTH), lambda i: (i, OFF_XQ // X_WIDTH)),
                  pl.BlockSpec((1, n_mem, 2 * X_WIDTH), lambda i: (seq.seq_id(i * tm), 0, 0)),
                  pl.BlockSpec((len(POOL_WINDOWS), POOL_GROUP, POOL_GROUP), lambda i: (0, 0, 0)),
                  pl.BlockSpec((1, POOL_WIDTH), lambda i: (0, 0))],
        out_specs=pl.BlockSpec((tm, POOL_WIDTH + X_WIDTH), lambda i: (i, 0)),
        out_shape=jax.ShapeDtypeStruct((t, POOL_WIDTH + X_WIDTH), BF16),
        compiler_params=_cparams("parallel"),
        name="pool_attn",
    )(hin, hin, hin, hin, kv, pool_w, pool_scale.reshape(1, POOL_WIDTH))


def _outproj_kernel(m_ref, pa_ref, x_ref, w_ref, g_ref, b_ref, o_ref, ob_ref):
    acc = jnp.dot(m_ref[...], w_ref[0:M_WIDTH, :], preferred_element_type=F32)
    acc = acc + jnp.dot(pa_ref[...], w_ref[M_WIDTH:, :], preferred_element_type=F32)
    y = _layer_norm(ALPHA * x_ref[...] + acc, g_ref[...], b_ref[...], LN_EPS)
    o_ref[...] = y
    ob_ref[...] = y.astype(BF16)


def _outproj_call(m_out, pa_out, x, w_out, g, b):
    t, d = x.shape
    tm = min(OUT_TM, t)
    return pl.pallas_call(
        _outproj_kernel,
        grid=(t // tm,),
        in_specs=[pl.BlockSpec((tm, M_WIDTH), lambda i: (i, 0)),
                  pl.BlockSpec((tm, POOL_WIDTH + X_WIDTH), lambda i: (i, 0)),
                  pl.BlockSpec((tm, d), lambda i: (i, 0)),
                  pl.BlockSpec((d, d), lambda i: (0, 0)),
                  pl.BlockSpec((1, d), lambda i: (0, 0)),
                  pl.BlockSpec((1, d), lambda i: (0, 0))],
        out_specs=[pl.BlockSpec((tm, d), lambda i: (i, 0)),
                   pl.BlockSpec((tm, d), lambda i: (i, 0))],
        out_shape=[jax.ShapeDtypeStruct((t, d), F32), jax.ShapeDtypeStruct((t, d), BF16)],
        compiler_params=_cparams("parallel"),
        name="outproj_ln",
    )(m_out, pa_out, x, w_out, g.reshape(1, d), b.reshape(1, d))


def _first_index_of_max(vals, idx, axis_size):
    mx = jnp.max(vals, axis=0, keepdims=True)
    first = jnp.min(jnp.where(vals == mx, idx, axis_size), axis=0, keepdims=True)
    return mx, first


def _router_kernel(x_ref, w_ref, b_ref, ids_ref, gate_ref, rank_ref, cnt_ref, carry_ref, *, tm):
    @pl.when(pl.program_id(0) == 0)
    def _():
        carry_ref[...] = jnp.zeros_like(carry_ref)

    logits = lax.dot_general(w_ref[...], x_ref[...], (((1,), (1,)), ((), ())),
                             preferred_element_type=F32)
    s = jax.nn.sigmoid(logits)
    sel = s + b_ref[...]
    neg = -jnp.inf
    sub = lax.broadcasted_iota(I32, (GROUP_SIZE, tm), 0)
    gscores = []
    for g in range(N_GROUPS):
        blk = sel[g * GROUP_SIZE:(g + 1) * GROUP_SIZE, :]
        m1, i1 = _first_index_of_max(blk, sub, GROUP_SIZE)
        m2 = jnp.max(jnp.where(sub == i1, neg, blk), axis=0, keepdims=True)
        gscores.append(m1 + m2)
    gs = jnp.concatenate(gscores, axis=0)
    gidx = lax.broadcasted_iota(I32, (N_GROUPS, tm), 0)
    gmask = jnp.zeros((N_GROUPS, tm), jnp.bool_)
    for _ in range(TOPK_GROUPS):
        _, gi = _first_index_of_max(gs, gidx, N_GROUPS)
        hit = gidx == gi
        gmask = jnp.logical_or(gmask, hit)
        gs = jnp.where(hit, neg, gs)
    gmask_f = gmask.astype(F32)
    masked = jnp.concatenate(
        [jnp.where(gmask_f[g:g + 1, :] > 0.5, sel[g * GROUP_SIZE:(g + 1) * GROUP_SIZE, :], neg)
         for g in range(N_GROUPS)], axis=0)
    eidx = lax.broadcasted_iota(I32, (N_EXPERTS, tm), 0)
    chosen = jnp.zeros((N_EXPERTS, tm), F32)
    ids, gates, hits = [], [], []
    for _ in range(TOP_K):
        _, ei = _first_index_of_max(masked, eidx, N_EXPERTS)
        hit = eidx == ei
        ids.append(ei)
        gates.append(jnp.sum(jnp.where(hit, s, 0.0), axis=0, keepdims=True))
        hits.append(hit)
        chosen = jnp.where(hit, 1.0, chosen)
        masked = jnp.where(hit, neg, masked)
    gate = jnp.concatenate(gates, axis=0)
    gate = gate / jnp.sum(gate, axis=0, keepdims=True) * ROUTED_SCALE
    ids_ref[...] = jnp.concatenate(ids, axis=0)
    gate_ref[...] = gate
    r = lax.broadcasted_iota(I32, (tm, tm), 0)
    c = lax.broadcasted_iota(I32, (tm, tm), 1)
    before = (r < c).astype(BF16)
    cnt = jnp.dot(chosen.astype(BF16), before, preferred_element_type=F32) + carry_ref[:, 0:1]
    ranks = [jnp.sum(jnp.where(hit, cnt, 0.0), axis=0, keepdims=True) for hit in hits]
    rank_ref[...] = jnp.concatenate(ranks, axis=0).astype(I32)
    total = carry_ref[:, 0:1] + jnp.sum(chosen, axis=1, keepdims=True)
    carry_ref[...] = jnp.broadcast_to(total, carry_ref.shape)
    cnt_ref[...] = jnp.broadcast_to(total, cnt_ref.shape)


def _router_call(xb, wrt, br):
    t, d = xb.shape
    tm = min(ROUTE_TM, t)
    return pl.pallas_call(
        functools.partial(_router_kernel, tm=tm),
        grid=(t // tm,),
        in_specs=[pl.BlockSpec((tm, d), lambda i: (i, 0)),
                  pl.BlockSpec((N_EXPERTS, d), lambda i: (0, 0)),
                  pl.BlockSpec((N_EXPERTS, 1), lambda i: (0, 0))],
        out_specs=[pl.BlockSpec((TOP_K, tm), lambda i: (0, i)),
                   pl.BlockSpec((TOP_K, tm), lambda i: (0, i)),
                   pl.BlockSpec((TOP_K, tm), lambda i: (0, i)),
                   pl.BlockSpec((N_EXPERTS, LANES), lambda i: (0, 0))],
        out_shape=[jax.ShapeDtypeStruct((TOP_K, t), I32),
                   jax.ShapeDtypeStruct((TOP_K, t), F32),
                   jax.ShapeDtypeStruct((TOP_K, t), I32),
                   jax.ShapeDtypeStruct((N_EXPERTS, LANES), F32)],
        scratch_shapes=[pltpu.VMEM((N_EXPERTS, LANES), F32)],
        compiler_params=_cparams("arbitrary"),
        name="router",
    )(xb, wrt, br.reshape(N_EXPERTS, 1))


def _row_copy(x_hbm, xbuf, sem, slot, tok, r):
    return pltpu.make_async_copy(x_hbm.at[pl.ds(tok, 1)], xbuf.at[slot, pl.ds(r, 1)], sem.at[slot])


def _expert_kernel(blk_e_ref, nused_ref, tok_ref, tok_next_ref, x_hbm, wgu_ref, wdn_ref, o_ref,
                   xbuf, sem, *, bm):
    b = pl.program_id(0)
    nb = pl.num_programs(0)
    n_used = nused_ref[0]
    slot = lax.rem(b, 2)

    def issue(ref, slot_):
        def body(r, carry):
            _row_copy(x_hbm, xbuf, sem, slot_, ref[0, 0, r], r).start()
            return carry
        lax.fori_loop(0, bm, body, 0)

    @pl.when(jnp.logical_and(b == 0, n_used > 0))
    def _():
        issue(tok_ref, 0)

    @pl.when(b + 1 < jnp.minimum(n_used, nb))
    def _():
        issue(tok_next_ref, 1 - slot)

    @pl.when(b < n_used)
    def _():
        def wait_body(r, carry):
            _row_copy(x_hbm, xbuf, sem, slot, 0, r).wait()
            return carry
        lax.fori_loop(0, bm, wait_body, 0)
        x = xbuf[slot].astype(BF16)
        gu = jnp.dot(x, wgu_ref[0], preferred_element_type=F32)
        g = gu[:, :D_EXPERT]
        u = gu[:, D_EXPERT:]
        hmid = (g * jax.nn.sigmoid(g) * u).astype(BF16)
        o_ref[...] = jnp.dot(hmid, wdn_ref[0], preferred_element_type=F32)

    @pl.when(b >= n_used)
    def _():
        o_ref[...] = jnp.zeros_like(o_ref)


def _expert_call(blk_e, n_used, row_tok, x, wgu, wdn, bm):
    n_blocks = row_tok.shape[0]
    d = x.shape[1]
    grid_spec = pltpu.PrefetchScalarGridSpec(
        num_scalar_prefetch=2,
        grid=(n_blocks,),
        in_specs=[pl.BlockSpec((1, 1, bm), lambda b, be, nu: (b, 0, 0), memory_space=pltpu.SMEM),
                  pl.BlockSpec((1, 1, bm), lambda b, be, nu: (jnp.minimum(b + 1, n_blocks - 1), 0, 0),
                               memory_space=pltpu.SMEM),
                  pl.BlockSpec(memory_space=pl.ANY),
                  pl.BlockSpec((1, d, 2 * D_EXPERT), lambda b, be, nu: (be[b], 0, 0)),
                  pl.BlockSpec((1, D_EXPERT, d), lambda b, be, nu: (be[b], 0, 0))],
        out_specs=pl.BlockSpec((bm, d), lambda b, be, nu: (b, 0)),
        scratch_shapes=[pltpu.VMEM((2, bm, d), F32), pltpu.SemaphoreType.DMA((2,))],
    )
    return pl.pallas_call(
        functools.partial(_expert_kernel, bm=bm),
        grid_spec=grid_spec,
        out_shape=jax.ShapeDtypeStruct((n_blocks * bm, d), F32),
        compiler_params=_cparams("arbitrary"),
        name="experts",
    )(blk_e, n_used, row_tok, row_tok, x, wgu, wdn)


def _ys_copy(ys_hbm, buf, sem, k, row, r):
    return pltpu.make_async_copy(ys_hbm.at[pl.ds(row, 1)], buf.at[k, pl.ds(r, 1)], sem.at[0])


def _combine_kernel(dest_ref, gate_ref, x_ref, xb_ref, ys_hbm, wsg_ref, wsd_ref, g_ref, b_ref,
                    o_ref, ob_ref, buf, sem, *, tm):
    def issue(r, carry):
        for k in range(TOP_K):
            _ys_copy(ys_hbm, buf, sem, k, dest_ref[0, k, r], r).start()
        return carry
    lax.fori_loop(0, tm, issue, 0)
    gu = jnp.dot(xb_ref[...], wsg_ref[...], preferred_element_type=F32)
    g = gu[:, :D_SHARED]
    u = gu[:, D_SHARED:]
    hmid = (g * jax.nn.sigmoid(g) * u).astype(BF16)
    shared = jnp.dot(hmid, wsd_ref[...], preferred_element_type=F32)

    def wait(r, carry):
        for k in range(TOP_K):
            _ys_copy(ys_hbm, buf, sem, k, 0, r).wait()
        return carry
    lax.fori_loop(0, tm, wait, 0)
    acc = gate_ref[:, 0:1] * buf[0]
    for k in range(1, TOP_K):
        acc = acc + gate_ref[:, k:k + 1] * buf[k]
    y = _layer_norm(ALPHA * x_ref[...] + (acc + shared), g_ref[...], b_ref[...], LN_EPS)
    o_ref[...] = y
    ob_ref[...] = y.astype(BF16)


def _combine_call(dest3, gate_tk, x, xb, ys, wsg, wsd, g, b):
    t, d = x.shape
    tm = min(COMB_TM, t)
    return pl.pallas_call(
        functools.partial(_combine_kernel, tm=tm),
        grid=(t // tm,),
        in_specs=[pl.BlockSpec((1, TOP_K, tm), lambda i: (i, 0, 0), memory_space=pltpu.SMEM),
                  pl.BlockSpec((tm, TOP_K), lambda i: (i, 0)),
                  pl.BlockSpec((tm, d), lambda i: (i, 0)),
                  pl.BlockSpec((tm, d), lambda i: (i, 0)),
                  pl.BlockSpec(memory_space=pl.ANY),
                  pl.BlockSpec((d, 2 * D_SHARED), lambda i: (0, 0)),
                  pl.BlockSpec((D_SHARED, d), lambda i: (0, 0)),
                  pl.BlockSpec((1, d), lambda i: (0, 0)),
                  pl.BlockSpec((1, d), lambda i: (0, 0))],
        out_specs=[pl.BlockSpec((tm, d), lambda i: (i, 0)),
                   pl.BlockSpec((tm, d), lambda i: (i, 0))],
        out_shape=[jax.ShapeDtypeStruct((t, d), F32), jax.ShapeDtypeStruct((t, d), BF16)],
        scratch_shapes=[pltpu.VMEM((TOP_K, tm, d), F32), pltpu.SemaphoreType.DMA((1,))],
        compiler_params=_cparams("arbitrary"),
        name="combine_ln",
    )(dest3, gate_tk, x, xb, ys, wsg, wsd, g.reshape(1, d), b.reshape(1, d))


def _dispatch_tables(ids, rank, counts_f, bm):
    k, t = ids.shape
    tk = k * t
    n_blocks = -(-tk // bm) + N_EXPERTS
    counts = counts_f[:, 0].astype(I32)
    padded = (counts + bm - 1) // bm * bm
    pends = jnp.cumsum(padded)
    pstart = pends - padded
    starts = jnp.cumsum(counts) - counts
    dest = pstart[ids] + rank
    tok = jnp.broadcast_to(jnp.arange(t, dtype=I32)[None, :], (k, t))
    sorted_tok = lax.rem(jnp.sort((ids * t + tok).reshape(tk)), t)
    blk_e = jnp.minimum(jnp.searchsorted(pends, jnp.arange(n_blocks, dtype=I32) * bm, side='right'),
                        N_EXPERTS - 1).astype(I32)
    p = jnp.arange(n_blocks * bm, dtype=I32)
    e = blk_e[p // bm]
    off = p - pstart[e]
    valid = off < counts[e]
    row_tok = jnp.where(valid, sorted_tok[jnp.clip(starts[e] + off, 0, tk - 1)], 0)
    n_used = (pends[-1] // bm).astype(I32).reshape(1)
    return dest, row_tok.reshape(n_blocks, 1, bm), blk_e, n_used


def _layer(x, xb, memb, seq, w_in, b_gate, conv_w, conv_b, mh_norm_g, pool_w, pool_scale, w_mem_kv, w_out,
           ln1_g, ln1_b, w_router, b_router, w_gu, w_dn, w_sh_gu, w_sh_dn, ln2_g, ln2_b):
    t = x.shape[0]
    hin = _mm_call(xb, w_in[:, :OFF_G].astype(BF16), F32, "in_proj")
    gates = _gates_call(xb, w_in[:, OFF_G:].T.astype(BF16), b_gate)
    g3 = gates.reshape(N_GATES, t // CHUNK, CHUNK).transpose(1, 0, 2)
    qk = _conv_call(hin, conv_w, conv_b, seq)
    h_f = _mlstm_call(qk, hin, g3, None, None, seq, reverse=False)
    m_out = _mlstm_call(qk, hin, g3, h_f, mh_norm_g, seq, reverse=True)
    kv = _mm_call(memb, w_mem_kv.astype(BF16), BF16, "mem_kv").reshape(seq.nseq, -1, 2 * X_WIDTH)
    pa_out = _pool_attn_call(hin, kv, pool_w.astype(BF16), pool_scale, seq)
    x1, x1b = _outproj_call(m_out, pa_out, x, w_out.astype(BF16), ln1_g, ln1_b)
    ids, gate, rank, counts = _router_call(x1b, w_router.T.astype(BF16), b_router)
    bm = MOE_BM
    dest, row_tok, blk_e, n_used = _dispatch_tables(ids, rank, counts, bm)
    ys = _expert_call(blk_e, n_used, row_tok, x1, w_gu.astype(BF16), w_dn.astype(BF16), bm)
    ctm = min(COMB_TM, t)
    dest3 = dest.reshape(TOP_K, t // ctm, ctm).transpose(1, 0, 2)
    return _combine_call(dest3, gate.T, x1, x1b, ys, w_sh_gu.astype(BF16), w_sh_dn.astype(BF16), ln2_g, ln2_b)


def kernel(x_prompt, x_sample, mem_prompt, mem_sample, ln_in_g, ln_in_b, w_in, b_gate, conv_w, conv_b, mh_norm_g, pool_w, pool_scale, w_mem_kv, w_out, ln1_g, ln1_b, w_router, b_router, w_gu, w_dn, w_sh_gu, w_sh_dn, ln2_g, ln2_b):
    b1, s1, d = x_prompt.shape
    b2, s2, _ = x_sample.shape
    seq = SeqInfo(b1, s1, b2, s2)
    x = jnp.concatenate([x_prompt.reshape(b1 * s1, d), x_sample.reshape(b2 * s2, d)], axis=0)
    memb = jnp.concatenate([mem_prompt.reshape(-1, d), mem_sample.reshape(-1, d)], axis=0).astype(BF16)
    x, xb = _ln_call(x, ln_in_g, ln_in_b)
    for l in range(w_in.shape[0]):
        x, xb = _layer(x, xb, memb, seq, w_in[l], b_gate[l], conv_w[l], conv_b[l], mh_norm_g[l], pool_w[l],
                       pool_scale[l], w_mem_kv[l], w_out[l], ln1_g[l], ln1_b[l], w_router[l], b_router[l],
                       w_gu[l], w_dn[l], w_sh_gu[l], w_sh_dn[l], ln2_g[l], ln2_b[l])
    return (x[:seq.t1].reshape(b1, s1, d), x[seq.t1:].reshape(b2, s2, d))
```

```python
import functools

import jax
import jax.numpy as jnp
from jax import lax
from jax.experimental import pallas as pl
from jax.experimental.pallas import tpu as pltpu

F32 = jnp.float32
BF16 = jnp.bfloat16
I32 = jnp.int32

DEPTH = 4
D_MODEL = 2048
M_HEADS = 4
M_WIDTH = 1024
M_HEAD_DIM = 256
CHUNK = 128
CONV_K = 5
POOL_WINDOWS = (2, 4, 8, 16)
POOL_WIDTH = 512
POOL_GROUP = 128
X_HEADS = 4
X_WIDTH = 512
X_HEAD_DIM = 128
OFF_Q = 0
OFF_K = OFF_Q + M_WIDTH
OFF_V = OFF_K + M_WIDTH
OFF_O = OFF_V + M_WIDTH
OFF_P = OFF_O + M_WIDTH
OFF_XQ = OFF_P + POOL_WIDTH
OFF_G = OFF_XQ + X_WIDTH
N_GATES = 4 * M_HEADS
N_EXPERTS = 64
TOP_K = 8
N_GROUPS = 8
GROUP_SIZE = N_EXPERTS // N_GROUPS
TOPK_GROUPS = 4
D_EXPERT = 512
D_SHARED = 512
ROUTED_SCALE = 2.5
ALPHA = (2 * DEPTH) ** 0.25
LN_EPS = 1e-5
HN_EPS = 1e-6

LANES = 128
SUBLANES = 8
HALO = SUBLANES
VMEM_LIMIT = 56 * 1024 * 1024

LN_TM = 512
MM_TM = 512
MM_TN = 1024
SEQ_TM = 512
MLSTM_CB = 4
OUT_TM = 256
ROUTE_TM = 512
MOE_BM = 256
COMB_TM = 128


def _cparams(*sem):
    return pltpu.CompilerParams(dimension_semantics=sem, vmem_limit_bytes=VMEM_LIMIT)


class SeqInfo:
    def __init__(self, b1, s1, b2, s2):
        self.b1, self.s1, self.b2, self.s2 = b1, s1, b2, s2
        self.t1 = b1 * s1
        self.t = b1 * s1 + b2 * s2
        self.nseq = b1 + b2

    def pos_len(self, row0):
        in1 = row0 < self.t1
        pos = jnp.where(in1, lax.rem(row0, self.s1), lax.rem(row0 - self.t1, self.s2))
        slen = jnp.where(in1, self.s1, self.s2)
        return pos, slen

    def seq_id(self, row0):
        in1 = row0 < self.t1
        return jnp.where(in1, row0 // self.s1, self.b1 + (row0 - self.t1) // self.s2)


def _layer_norm(y, g, b, eps):
    mu = jnp.mean(y, axis=-1, keepdims=True)
    d = y - mu
    var = jnp.mean(d * d, axis=-1, keepdims=True)
    return d * lax.rsqrt(var + eps) * g + b


def _ln_kernel(x1_ref, x2_ref, g_ref, b_ref, o_ref, ob_ref, *, n1):
    x = jnp.where(pl.program_id(0) < n1, x1_ref[...], x2_ref[...])
    y = _layer_norm(x, g_ref[...], b_ref[...], LN_EPS)
    o_ref[...] = y
    ob_ref[...] = y.astype(BF16)


def _ln_call(x1, x2, g, b):
    t1, d = x1.shape
    t2 = x2.shape[0]
    tm = min(LN_TM, t1, t2)
    n1, n2 = t1 // tm, t2 // tm
    t = t1 + t2
    return pl.pallas_call(
        functools.partial(_ln_kernel, n1=n1),
        grid=(n1 + n2,),
        in_specs=[pl.BlockSpec((tm, d), lambda i: (jnp.minimum(i, n1 - 1), 0)),
                  pl.BlockSpec((tm, d), lambda i: (jnp.maximum(i - n1, 0), 0)),
                  pl.BlockSpec((1, d), lambda i: (0, 0)),
                  pl.BlockSpec((1, d), lambda i: (0, 0))],
        out_specs=[pl.BlockSpec((tm, d), lambda i: (i, 0)),
                   pl.BlockSpec((tm, d), lambda i: (i, 0))],
        out_shape=[jax.ShapeDtypeStruct((t, d), F32), jax.ShapeDtypeStruct((t, d), BF16)],
        compiler_params=_cparams("arbitrary"),
        name="ln_in",
    )(x1, x2, g.reshape(1, d), b.reshape(1, d))


def _mm_kernel(x_ref, w_ref, o_ref):
    o_ref[...] = jnp.dot(x_ref[...], w_ref[...], preferred_element_type=F32).astype(o_ref.dtype)


def _mm_call(xb, wb, out_dtype, name):
    m, k = xb.shape
    n = wb.shape[1]
    tm = min(MM_TM, m)
    while m % tm:
        tm //= 2
    tn = min(MM_TN, n)
    return pl.pallas_call(
        _mm_kernel,
        grid=(n // tn, m // tm),
        in_specs=[pl.BlockSpec((tm, k), lambda j, i: (i, 0)),
                  pl.BlockSpec((k, tn), lambda j, i: (0, j))],
        out_specs=pl.BlockSpec((tm, tn), lambda j, i: (i, j)),
        out_shape=jax.ShapeDtypeStruct((m, n), out_dtype),
        compiler_params=_cparams("parallel", "parallel"),
        name=name,
    )(xb, wb)


def _chunk_cumsum(x, reverse):
    n = x.shape[1]
    pos = lax.rem(lax.broadcasted_iota(I32, x.shape, 1), CHUNK)
    s = 1
    while s < CHUNK:
        if reverse:
            x = x + jnp.where(pos < CHUNK - s, pltpu.roll(x, n - s, axis=1), 0.0)
        else:
            x = x + jnp.where(pos >= s, pltpu.roll(x, s, axis=1), 0.0)
        s *= 2
    return x


def _gates_kernel(x_ref, w_ref, b_ref, o_ref):
    g = lax.dot_general(w_ref[...], x_ref[...], (((1,), (1,)), ((), ())),
                        preferred_element_type=F32) + b_ref[...]
    kind = lax.broadcasted_iota(I32, g.shape, 0) // M_HEADS
    logf = jax.nn.log_sigmoid(g)
    o_ref[...] = jnp.where(kind == 1, _chunk_cumsum(logf, False),
                           jnp.where(kind == 3, _chunk_cumsum(logf, True), g))


def _gates_call(xb, wgt, bg):
    t, d = xb.shape
    tm = min(MM_TM, t)
    return pl.pallas_call(
        _gates_kernel,
        grid=(t // tm,),
        in_specs=[pl.BlockSpec((tm, d), lambda i: (i, 0)),
                  pl.BlockSpec((N_GATES, d), lambda i: (0, 0)),
                  pl.BlockSpec((N_GATES, 1), lambda i: (0, 0))],
        out_specs=pl.BlockSpec((N_GATES, tm), lambda i: (0, i)),
        out_shape=jax.ShapeDtypeStruct((N_GATES, t), F32),
        compiler_params=_cparams("parallel"),
        name="gates",
    )(xb, wgt, bg.reshape(N_GATES, 1))


def _halo_specs(tm, width, col_block, n_rows):
    per = tm // HALO
    last = n_rows // HALO - 1
    prev = pl.BlockSpec((HALO, width), lambda i, *_: (jnp.maximum(i * per - 1, 0), col_block))
    cur = pl.BlockSpec((tm, width), lambda i, *_: (i, col_block))
    nxt = pl.BlockSpec((HALO, width), lambda i, *_: (jnp.minimum((i + 1) * per, last), col_block))
    return prev, cur, nxt


def _extended_tile(prev_ref, cur_ref, next_ref, tm, seq):
    row0 = pl.program_id(0) * tm
    pos, slen = seq.pos_len(row0)
    prev = jnp.where(pos == 0, 0.0, prev_ref[...])
    nxt = jnp.where(pos + tm == slen, 0.0, next_ref[...])
    return jnp.concatenate([prev, cur_ref[...], nxt], axis=0), pos, slen


def _shift_rows(xe, d):
    n = xe.shape[0]
    if d == 0:
        return xe
    return pltpu.roll(xe, (n - d) % n, axis=0)


def _conv_kernel(prev_ref, cur_ref, next_ref, w_ref, b_ref, o_ref, *, tm, tc, seq):
    xe, _, _ = _extended_tile(prev_ref, cur_ref, next_ref, tm, seq)
    acc = jnp.broadcast_to(b_ref[...], (tm, tc))
    for j in range(CONV_K):
        sh = _shift_rows(xe, j - CONV_K // 2)
        acc = acc + sh[HALO:HALO + tm] * w_ref[j:j + 1, :]
    y = acc * jax.nn.sigmoid(acc)
    is_k = pl.program_id(1) * tc >= M_WIDTH
    scale = jnp.where(is_k, M_HEAD_DIM ** -0.5, 1.0).astype(F32)
    o_ref[...] = (y * scale).astype(BF16)


def _conv_call(hin, conv_w, conv_b, seq):
    t = hin.shape[0]
    tm = min(SEQ_TM, seq.s1, seq.s2)
    tc = 512
    width = 2 * M_WIDTH
    prev = pl.BlockSpec((HALO, tc), lambda i, j: (jnp.maximum(i * (tm // HALO) - 1, 0), j))
    cur = pl.BlockSpec((tm, tc), lambda i, j: (i, j))
    nxt = pl.BlockSpec((HALO, tc), lambda i, j: (jnp.minimum((i + 1) * (tm // HALO), t // HALO - 1), j))
    return pl.pallas_call(
        functools.partial(_conv_kernel, tm=tm, tc=tc, seq=seq),
        grid=(t // tm, width // tc),
        in_specs=[prev, cur, nxt,
                  pl.BlockSpec((CONV_K, tc), lambda i, j: (0, j)),
                  pl.BlockSpec((1, tc), lambda i, j: (0, j))],
        out_specs=pl.BlockSpec((tm, tc), lambda i, j: (i, j)),
        out_shape=jax.ShapeDtypeStruct((t, width), BF16),
        compiler_params=_cparams("parallel", "parallel"),
        name="conv_silu",
    )(hin, hin, hin, conv_w, conv_b.reshape(1, width))


def _row_to_col(row, eye):
    return jnp.sum(jnp.where(eye, row, 0.0), axis=1, keepdims=True)


def _mlstm_kernel(*refs, reverse, finalize, cb, seq):
    if finalize:
        q_ref, k_ref, v_ref, g_ref, hf_ref, uo_ref, ng_ref, o_ref, c_ref, m_ref = refs
    else:
        q_ref, k_ref, v_ref, g_ref, o_ref, c_ref, m_ref = refs
    L, DK, DV = CHUNK, M_HEAD_DIM, M_HEAD_DIM
    step = pl.program_id(0)
    nsteps = pl.num_programs(0)
    blk = (nsteps - 1 - step) if reverse else step
    ti = lax.broadcasted_iota(I32, (L, L), 0)
    si = lax.broadcasted_iota(I32, (L, L), 1)
    causal = (si >= ti) if reverse else (si <= ti)
    eye = si == ti
    i_row0 = 2 * M_HEADS if reverse else 0
    f_row0 = i_row0 + M_HEADS

    @pl.when(step == 0)
    def _():
        c_ref[...] = jnp.zeros_like(c_ref)
        m_ref[...] = jnp.zeros_like(m_ref)

    m_vals = [m_ref[h, 0:1, 0:1] for h in range(M_HEADS)]
    for jj in range(cb):
        j = (cb - 1 - jj) if reverse else jj
        rows = slice(j * L, (j + 1) * L)
        pos, slen = seq.pos_len((blk * cb + j) * L)
        at_start = (pos + L == slen) if reverse else (pos == 0)
        gates = g_ref[j]
        for h in range(M_HEADS):
            cols = slice(h * DK, (h + 1) * DK)
            q = q_ref[rows, cols]
            k = k_ref[rows, cols]
            v32 = v_ref[rows, cols]
            i_row = gates[i_row0 + h:i_row0 + h + 1, :]
            b_row = gates[f_row0 + h:f_row0 + h + 1, :]
            b_col = _row_to_col(b_row, eye)
            m_prev = jnp.where(at_start, 0.0, m_vals[h])
            dmat = jnp.where(causal, b_col - b_row + i_row, -jnp.inf)
            m_t = jnp.maximum(b_col + m_prev, jnp.max(dmat, axis=1, keepdims=True))
            s = lax.dot_general(q, k, (((1,), (1,)), ((), ())), preferred_element_type=F32)
            a = s * jnp.exp(dmat - m_t)
            inter = jnp.exp(b_col + m_prev - m_t)
            c_aug = jnp.where(at_start, 0.0, c_ref[h])
            qc = jnp.dot(q, c_aug.astype(BF16), preferred_element_type=F32)
            num = jnp.dot(a.astype(BF16), v32.astype(BF16), preferred_element_type=F32) + inter * qc[:, :DV]
            den = jnp.sum(a, axis=1, keepdims=True) + inter * qc[:, DV:DV + 1]
            hh = num / jnp.maximum(jnp.abs(den), jnp.exp(-m_t))
            g_tot = b_row[:, 0:1] if reverse else b_row[:, L - 1:L]
            lw = g_tot - b_row + i_row
            m_new = jnp.maximum(g_tot + m_prev, jnp.max(lw, axis=1, keepdims=True))
            wk_col = _row_to_col(jnp.exp(lw - m_new), eye)
            decay = jnp.exp(g_tot + m_prev - m_new)
            v_aug = jnp.concatenate([v32 * wk_col, jnp.broadcast_to(wk_col, (L, LANES))], axis=1)
            upd = lax.dot_general(k, v_aug.astype(BF16), (((0,), (0,)), ((), ())),
                                  preferred_element_type=F32)
            c_ref[h] = decay * c_aug + upd
            m_vals[h] = m_new
            if finalize:
                hs = hf_ref[rows, cols] + hh
                mu = jnp.mean(hs, axis=1, keepdims=True)
                d = hs - mu
                var = jnp.mean(d * d, axis=1, keepdims=True)
                hn = d * lax.rsqrt(var + HN_EPS) * ng_ref[:, cols]
                o_ref[rows, cols] = (hn * jax.nn.sigmoid(uo_ref[rows, cols])).astype(o_ref.dtype)
            else:
                o_ref[rows, cols] = hh
    for h in range(M_HEADS):
        m_ref[h] = jnp.broadcast_to(m_vals[h], (SUBLANES, LANES))


def _mlstm_call(qk, hin, g3, hf, norm_g, seq, *, reverse):
    t = qk.shape[0]
    cb = MLSTM_CB
    while (seq.s1 // CHUNK) % cb or (seq.s2 // CHUNK) % cb:
        cb //= 2
    rows = cb * CHUNK
    nsteps = t // rows
    finalize = reverse
    blk = (lambda s: nsteps - 1 - s) if reverse else (lambda s: s)
    vblk = OFF_V // M_WIDTH
    oblk = OFF_O // M_WIDTH
    in_specs = [pl.BlockSpec((rows, M_WIDTH), lambda s: (blk(s), 0)),
                pl.BlockSpec((rows, M_WIDTH), lambda s: (blk(s), 1)),
                pl.BlockSpec((rows, M_WIDTH), lambda s: (blk(s), vblk)),
                pl.BlockSpec((cb, N_GATES, CHUNK), lambda s: (blk(s), 0, 0))]
    args = [qk, qk, hin, g3]
    if finalize:
        in_specs += [pl.BlockSpec((rows, M_WIDTH), lambda s: (blk(s), 0)),
                     pl.BlockSpec((rows, M_WIDTH), lambda s: (blk(s), oblk)),
                     pl.BlockSpec((1, M_WIDTH), lambda s: (0, 0))]
        args += [hf, hin, norm_g.reshape(1, M_WIDTH)]
    return pl.pallas_call(
        functools.partial(_mlstm_kernel, reverse=reverse, finalize=finalize, cb=cb, seq=seq),
        grid=(nsteps,),
        in_specs=in_specs,
        out_specs=pl.BlockSpec((rows, M_WIDTH), lambda s: (blk(s), 0)),
        out_shape=jax.ShapeDtypeStruct((t, M_WIDTH), BF16 if finalize else F32),
        scratch_shapes=[pltpu.VMEM((M_HEADS, M_HEAD_DIM, M_HEAD_DIM + LANES), F32),
                        pltpu.VMEM((M_HEADS, SUBLANES, LANES), F32)],
        compiler_params=_cparams("arbitrary"),
        name="mlstm_bwd" if reverse else "mlstm_fwd",
    )(*args)


def _pool_attn_kernel(prev_ref, cur_ref, next_ref, xq_ref, kv_ref, pw_ref, ps_ref, o_ref, *, tm, seq):
    xe, pos0, slen = _extended_tile(prev_ref, cur_ref, next_ref, tm, seq)
    pos = pos0 + lax.broadcasted_iota(I32, (tm, 1), 0)
    for gi, w in enumerate(POOL_WINDOWS):
        cols = slice(gi * POOL_GROUP, (gi + 1) * POOL_GROUP)
        u = xe[:, cols]
        win = _shift_rows(u, -1) + u
        half = 1
        while 2 * half < w:
            win = _shift_rows(win, -half) + _shift_rows(win, half)
            half *= 2
        cnt = (jnp.minimum(pos + w // 2, slen) - jnp.maximum(pos - w // 2, 0)).astype(F32)
        mean = win[HALO:HALO + tm] / cnt
        diff = (mean - u[HALO:HALO + tm]).astype(BF16)
        y = jnp.dot(diff, pw_ref[gi], preferred_element_type=F32)
        o_ref[:, cols] = (y * ps_ref[:, cols]).astype(o_ref.dtype)
    xq = xq_ref[...].astype(BF16)
    for h in range(X_HEADS):
        cols = slice(h * X_HEAD_DIM, (h + 1) * X_HEAD_DIM)
        kh = kv_ref[0, :, h * X_HEAD_DIM:(h + 1) * X_HEAD_DIM]
        vh = kv_ref[0, :, X_WIDTH + h * X_HEAD_DIM:X_WIDTH + (h + 1) * X_HEAD_DIM]
        s = lax.dot_general(xq[:, cols], kh, (((1,), (1,)), ((), ())),
                            preferred_element_type=F32) * (X_HEAD_DIM ** -0.5)
        e = jnp.exp(s - jnp.max(s, axis=1, keepdims=True))
        p = e / jnp.sum(e, axis=1, keepdims=True)
        y = jnp.dot(p.astype(BF16), vh, preferred_element_type=F32)
        o_ref[:, POOL_WIDTH + h * X_HEAD_DIM:POOL_WIDTH + (h + 1) * X_HEAD_DIM] = y.astype(o_ref.dtype)


def _pool_attn_call(hin, kv, pool_w, pool_scale, seq):
    t = hin.shape[0]
    tm = min(SEQ_TM, seq.s1, seq.s2)
    n_mem = kv.shape[1]
    prev, cur, nxt = _halo_specs(tm, POOL_WIDTH, OFF_P // POOL_WIDTH, t)
    return pl.pallas_call(
        functools.partial(_pool_attn_kernel, tm=tm, seq=seq),
        grid=(t // tm,),
        in_specs=[prev, cur, nxt,
                  pl.BlockSpec((tm, X_WIDTH), lambda i: (i, OFF_XQ // X_WIDTH)),
                  pl.BlockSpec((1, n_mem, 2 * X_WIDTH), lambda i: (seq.seq_id(i * tm), 0, 0)),
                  pl.BlockSpec((len(POOL_WINDOWS), POOL_GROUP, POOL_GROUP), lambda i: (0, 0, 0)),
                  pl.BlockSpec((1, POOL_WIDTH), lambda i: (0, 0))],
        out_specs=pl.BlockSpec((tm, POOL_WIDTH + X_WIDTH), lambda i: (i, 0)),
        out_shape=jax.ShapeDtypeStruct((t, POOL_WIDTH + X_WIDTH), BF16),
        compiler_params=_cparams("parallel"),
        name="pool_attn",
    )(hin, hin, hin, hin, kv, pool_w, pool_scale.reshape(1, POOL_WIDTH))


def _outproj_kernel(m_ref, pa_ref, x_ref, w_ref, g_ref, b_ref, o_ref, ob_ref):
    acc = jnp.dot(m_ref[...], w_ref[0:M_WIDTH, :], preferred_element_type=F32)
    acc = acc + jnp.dot(pa_ref[...], w_ref[M_WIDTH:, :], preferred_element_type=F32)
    y = _layer_norm(ALPHA * x_ref[...] + acc, g_ref[...], b_ref[...], LN_EPS)
    o_ref[...] = y
    ob_ref[...] = y.astype(BF16)


def _outproj_call(m_out, pa_out, x, w_out, g, b):
    t, d = x.shape
    tm = min(OUT_TM, t)
    return pl.pallas_call(
        _outproj_kernel,
        grid=(t // tm,),
        in_specs=[pl.BlockSpec((tm, M_WIDTH), lambda i: (i, 0)),
                  pl.BlockSpec((tm, POOL_WIDTH + X_WIDTH), lambda i: (i, 0)),
                  pl.BlockSpec((tm, d), lambda i: (i, 0)),
                  pl.BlockSpec((d, d), lambda i: (0, 0)),
                  pl.BlockSpec((1, d), lambda i: (0, 0)),
                  pl.BlockSpec((1, d), lambda i: (0, 0))],
        out_specs=[pl.BlockSpec((tm, d), lambda i: (i, 0)),
                   pl.BlockSpec((tm, d), lambda i: (i, 0))],
        out_shape=[jax.ShapeDtypeStruct((t, d), F32), jax.ShapeDtypeStruct((t, d), BF16)],
        compiler_params=_cparams("parallel"),
        name="outproj_ln",
    )(m_out, pa_out, x, w_out, g.reshape(1, d), b.reshape(1, d))


def _first_index_of_max(vals, idx, axis_size):
    mx = jnp.max(vals, axis=0, keepdims=True)
    first = jnp.min(jnp.where(vals == mx, idx, axis_size), axis=0, keepdims=True)
    return mx, first


def _router_kernel(x_ref, w_ref, b_ref, ids_ref, gate_ref, rank_ref, cnt_ref, carry_ref, *, tm):
    @pl.when(pl.program_id(0) == 0)
    def _():
        carry_ref[...] = jnp.zeros_like(carry_ref)

    logits = lax.dot_general(w_ref[...], x_ref[...], (((1,), (1,)), ((), ())),
                             preferred_element_type=F32)
    s = jax.nn.sigmoid(logits)
    sel = s + b_ref[...]
    neg = -jnp.inf
    sub = lax.broadcasted_iota(I32, (GROUP_SIZE, tm), 0)
    gscores = []
    for g in range(N_GROUPS):
        blk = sel[g * GROUP_SIZE:(g + 1) * GROUP_SIZE, :]
        m1, i1 = _first_index_of_max(blk, sub, GROUP_SIZE)
        m2 = jnp.max(jnp.where(sub == i1, neg, blk), axis=0, keepdims=True)
        gscores.append(m1 + m2)
    gs = jnp.concatenate(gscores, axis=0)
    gidx = lax.broadcasted_iota(I32, (N_GROUPS, tm), 0)
    gmask = jnp.zeros((N_GROUPS, tm), jnp.bool_)
    for _ in range(TOPK_GROUPS):
        _, gi = _first_index_of_max(gs, gidx, N_GROUPS)
        hit = gidx == gi
        gmask = jnp.logical_or(gmask, hit)
        gs = jnp.where(hit, neg, gs)
    gmask_f = gmask.astype(F32)
    masked = jnp.concatenate(
        [jnp.where(gmask_f[g:g + 1, :] > 0.5, sel[g * GROUP_SIZE:(g + 1) * GROUP_SIZE, :], neg)
         for g in range(N_GROUPS)], axis=0)
    eidx = lax.broadcasted_iota(I32, (N_EXPERTS, tm), 0)
    chosen = jnp.zeros((N_EXPERTS, tm), F32)
    ids, gates, hits = [], [], []
    for _ in range(TOP_K):
        _, ei = _first_index_of_max(masked, eidx, N_EXPERTS)
        hit = eidx == ei
        ids.append(ei)
        gates.append(jnp.sum(jnp.where(hit, s, 0.0), axis=0, keepdims=True))
        hits.append(hit)
        chosen = jnp.where(hit, 1.0, chosen)
        masked = jnp.where(hit, neg, masked)
    gate = jnp.concatenate(gates, axis=0)
    gate = gate / jnp.sum(gate, axis=0, keepdims=True) * ROUTED_SCALE
    ids_ref[...] = jnp.concatenate(ids, axis=0)
    gate_ref[...] = gate
    r = lax.broadcasted_iota(I32, (tm, tm), 0)
    c = lax.broadcasted_iota(I32, (tm, tm), 1)
    before = (r < c).astype(BF16)
    cnt = jnp.dot(chosen.astype(BF16), before, preferred_element_type=F32) + carry_ref[:, 0:1]
    ranks = [jnp.sum(jnp.where(hit, cnt, 0.0), axis=0, keepdims=True) for hit in hits]
    rank_ref[...] = jnp.concatenate(ranks, axis=0).astype(I32)
    total = carry_ref[:, 0:1] + jnp.sum(chosen, axis=1, keepdims=True)
    carry_ref[...] = jnp.broadcast_to(total, carry_ref.shape)
    cnt_ref[...] = jnp.broadcast_to(total, cnt_ref.shape)


def _router_call(xb, wrt, br):
    t, d = xb.shape
    tm = min(ROUTE_TM, t)
    return pl.pallas_call(
        functools.partial(_router_kernel, tm=tm),
        grid=(t // tm,),
        in_specs=[pl.BlockSpec((tm, d), lambda i: (i, 0)),
                  pl.BlockSpec((N_EXPERTS, d), lambda i: (0, 0)),
                  pl.BlockSpec((N_EXPERTS, 1), lambda i: (0, 0))],
        out_specs=[pl.BlockSpec((TOP_K, tm), lambda i: (0, i)),
                   pl.BlockSpec((TOP_K, tm), lambda i: (0, i)),
                   pl.BlockSpec((TOP_K, tm), lambda i: (0, i)),
                   pl.BlockSpec((N_EXPERTS, LANES), lambda i: (0, 0))],
        out_shape=[jax.ShapeDtypeStruct((TOP_K, t), I32),
                   jax.ShapeDtypeStruct((TOP_K, t), F32),
                   jax.ShapeDtypeStruct((TOP_K, t), I32),
                   jax.ShapeDtypeStruct((N_EXPERTS, LANES), F32)],
        scratch_shapes=[pltpu.VMEM((N_EXPERTS, LANES), F32)],
        compiler_params=_cparams("arbitrary"),
        name="router",
    )(xb, wrt, br.reshape(N_EXPERTS, 1))


def _slab_wait(src_hbm, buf, sem, slot, rows):
    pltpu.make_async_copy(src_hbm.at[pl.ds(0, rows)], buf.at[slot], sem.at[slot]).wait()


def _expert_kernel(blk_e_ref, tok_ref, tok_next_ref, x_hbm, wgu_ref, wdn_ref, o_ref,
                   xbuf0, xbuf1, wgu_b, wdn_b, sem, *, bm, n_blocks):
    b = pl.program_id(0)
    bufs = (xbuf0, xbuf1)

    def issue(ref, s):
        for r in range(bm):
            pltpu.make_async_copy(x_hbm.at[pl.ds(ref[0, 0, r], 1)], bufs[s].at[pl.ds(r, 1)], sem.at[s]).start()

    def wait(s):
        pltpu.make_async_copy(x_hbm.at[pl.ds(0, bm)], bufs[s], sem.at[s]).wait()

    @pl.when(b == 0)
    def _():
        issue(tok_ref, 0)

    @pl.when(jnp.logical_or(b == 0, blk_e_ref[b] != blk_e_ref[jnp.maximum(b - 1, 0)]))
    def _():
        wgu_b[...] = wgu_ref[0].astype(BF16)
        wdn_b[...] = wdn_ref[0].astype(BF16)

    def step(cur):
        wait(cur)
        issue(tok_next_ref, 1 - cur)
        x = bufs[cur][...].astype(BF16)
        gu = jnp.dot(x, wgu_b[...], preferred_element_type=F32)
        g = gu[:, :D_EXPERT]
        u = gu[:, D_EXPERT:]
        hmid = (g * jax.nn.sigmoid(g) * u).astype(BF16)
        o_ref[...] = jnp.dot(hmid, wdn_b[...], preferred_element_type=F32)

    for cur in range(2):
        pl.when(lax.rem(b, 2) == cur)(functools.partial(step, cur))

    @pl.when(b == n_blocks - 1)
    def _():
        wait(n_blocks % 2)


def _expert_call(blk_e, row_tok, x, wgu, wdn, bm):
    n_blocks = row_tok.shape[0]
    d = x.shape[1]
    grid_spec = pltpu.PrefetchScalarGridSpec(
        num_scalar_prefetch=1,
        grid=(n_blocks,),
        in_specs=[pl.BlockSpec((1, 1, bm), lambda b, be: (b, 0, 0), memory_space=pltpu.SMEM),
                  pl.BlockSpec((1, 1, bm), lambda b, be: (jnp.minimum(b + 1, n_blocks - 1), 0, 0),
                               memory_space=pltpu.SMEM),
                  pl.BlockSpec(memory_space=pl.ANY),
                  pl.BlockSpec((1, d, 2 * D_EXPERT), lambda b, be: (be[b], 0, 0)),
                  pl.BlockSpec((1, D_EXPERT, d), lambda b, be: (be[b], 0, 0))],
        out_specs=pl.BlockSpec((bm, d), lambda b, be: (b, 0)),
        scratch_shapes=[pltpu.VMEM((bm, d), F32), pltpu.VMEM((bm, d), F32),
                        pltpu.VMEM((d, 2 * D_EXPERT), BF16), pltpu.VMEM((D_EXPERT, d), BF16),
                        pltpu.SemaphoreType.DMA((2,))],
    )
    return pl.pallas_call(
        functools.partial(_expert_kernel, bm=bm, n_blocks=n_blocks),
        grid_spec=grid_spec,
        out_shape=jax.ShapeDtypeStruct((n_blocks * bm, d), F32),
        compiler_params=_cparams("arbitrary"),
        name="experts",
    )(blk_e, row_tok, row_tok, x, wgu, wdn)


COMB_UNROLL = 8


def _combine_kernel(dest_ref, dest_next_ref, gate_ref, x_ref, xb_ref, ys_hbm, wsg_ref, wsd_ref, g_ref, b_ref,
                    *rest, tm, n1, nsteps):
    if n1 is None:
        o_ref, ob_ref, buf0, buf1, base_ref, sem = rest
    else:
        o1_ref, o2_ref, buf0, buf1, base_ref, sem = rest
    i = pl.program_id(0)
    bufs = (buf0, buf1)

    def issue(ref, s):
        for r in range(tm):
            for k in range(TOP_K):
                pltpu.make_async_copy(ys_hbm.at[pl.ds(ref[0, k, r], 1)],
                                      bufs[s].at[pl.ds(k * tm + r, 1)], sem.at[s]).start()

    def wait(s):
        pltpu.make_async_copy(ys_hbm.at[pl.ds(0, TOP_K * tm)], bufs[s], sem.at[s]).wait()

    @pl.when(i == 0)
    def _():
        issue(dest_ref, 0)

    gu = jnp.dot(xb_ref[...], wsg_ref[...], preferred_element_type=F32)
    g = gu[:, :D_SHARED]
    u = gu[:, D_SHARED:]
    hmid = (g * jax.nn.sigmoid(g) * u).astype(BF16)
    base_ref[...] = ALPHA * x_ref[...] + jnp.dot(hmid, wsd_ref[...], preferred_element_type=F32)

    def step(cur):
        wait(cur)
        issue(dest_next_ref, 1 - cur)
        acc = gate_ref[:, 0:1] * bufs[cur][0:tm, :]
        for k in range(1, TOP_K):
            acc = acc + gate_ref[:, k:k + 1] * bufs[cur][k * tm:(k + 1) * tm, :]
        y = _layer_norm(base_ref[...] + acc, g_ref[...], b_ref[...], LN_EPS)
        if n1 is None:
            o_ref[...] = y
            ob_ref[...] = y.astype(BF16)
        else:
            @pl.when(i < n1)
            def _():
                o1_ref[...] = y

            @pl.when(i >= n1)
            def _():
                o2_ref[...] = y

    for cur in range(2):
        pl.when(lax.rem(i, 2) == cur)(functools.partial(step, cur))

    @pl.when(i == nsteps - 1)
    def _():
        wait(nsteps % 2)


def _combine_call(dest3, gate_tk, x, xb, ys, wsg, wsd, g, b, split_rows=None):
    t, d = x.shape
    tm = dest3.shape[2]
    nsteps = t // tm
    if split_rows is None:
        n1 = None
        out_specs = [pl.BlockSpec((tm, d), lambda i: (i, 0)), pl.BlockSpec((tm, d), lambda i: (i, 0))]
        out_shape = [jax.ShapeDtypeStruct((t, d), F32), jax.ShapeDtypeStruct((t, d), BF16)]
    else:
        n1 = split_rows // tm
        out_specs = [pl.BlockSpec((tm, d), lambda i: (jnp.minimum(i, n1 - 1), 0)),
                     pl.BlockSpec((tm, d), lambda i: (jnp.maximum(i - n1, 0), 0))]
        out_shape = [jax.ShapeDtypeStruct((split_rows, d), F32), jax.ShapeDtypeStruct((t - split_rows, d), F32)]
    return pl.pallas_call(
        functools.partial(_combine_kernel, tm=tm, n1=n1, nsteps=nsteps),
        grid=(nsteps,),
        in_specs=[pl.BlockSpec((1, TOP_K, tm), lambda i: (i, 0, 0), memory_space=pltpu.SMEM),
                  pl.BlockSpec((1, TOP_K, tm), lambda i: (jnp.minimum(i + 1, nsteps - 1), 0, 0),
                               memory_space=pltpu.SMEM),
                  pl.BlockSpec((tm, TOP_K), lambda i: (i, 0)),
                  pl.BlockSpec((tm, d), lambda i: (i, 0)),
                  pl.BlockSpec((tm, d), lambda i: (i, 0)),
                  pl.BlockSpec(memory_space=pl.ANY),
                  pl.BlockSpec((d, 2 * D_SHARED), lambda i: (0, 0)),
                  pl.BlockSpec((D_SHARED, d), lambda i: (0, 0)),
                  pl.BlockSpec((1, d), lambda i: (0, 0)),
                  pl.BlockSpec((1, d), lambda i: (0, 0))],
        out_specs=out_specs,
        out_shape=out_shape,
        scratch_shapes=[pltpu.VMEM((TOP_K * tm, d), F32), pltpu.VMEM((TOP_K * tm, d), F32),
                        pltpu.VMEM((tm, d), F32), pltpu.SemaphoreType.DMA((2,))],
        compiler_params=_cparams("arbitrary"),
        name="combine_ln",
    )(dest3, dest3, gate_tk, x, xb, ys, wsg, wsd, g.reshape(1, d), b.reshape(1, d))


def _dispatch_tables(ids, rank, counts_f, bm):
    k, t = ids.shape
    tk = k * t
    n_blocks = -(-tk // bm) + N_EXPERTS
    n_fill = n_blocks * bm - tk
    counts = counts_f[:, 0].astype(I32)
    padded = (counts + bm - 1) // bm * bm
    pends = jnp.cumsum(padded)
    pstart = pends - padded
    experts = jnp.arange(N_EXPERTS, dtype=I32)
    dest = jnp.sum(jnp.where(ids[:, :, None] == experts, pstart, 0), axis=-1) + rank
    stride = t + 1
    tok = jnp.broadcast_to(jnp.arange(t, dtype=I32)[None, :], (k, t))
    real_keys = (ids * stride + tok).reshape(tk)
    cum_need = jnp.cumsum(padded - counts)
    fill_e = jnp.sum(cum_need[None, :] <= jnp.arange(n_fill, dtype=I32)[:, None], axis=1)
    keys = jnp.sort(jnp.concatenate([real_keys, fill_e.astype(I32) * stride + t]))
    row_tok = lax.rem(keys, stride)
    row_tok = jnp.where(row_tok == t, 0, row_tok)
    blk_first = jnp.arange(n_blocks, dtype=I32) * bm
    blk_e = jnp.minimum(jnp.sum(pends[None, :] <= blk_first[:, None], axis=1), N_EXPERTS - 1).astype(I32)
    return dest, row_tok.reshape(n_blocks, 1, bm), blk_e


def _layer(x, xb, memb, seq, w_in, b_gate, conv_w, conv_b, mh_norm_g, pool_w, pool_scale, w_mem_kv, w_out,
           ln1_g, ln1_b, w_router, b_router, w_gu, w_dn, w_sh_gu, w_sh_dn, ln2_g, ln2_b, last):
    t = x.shape[0]
    hin = _mm_call(xb, w_in[:, :OFF_G].astype(BF16), F32, "in_proj")
    gates = _gates_call(xb, w_in[:, OFF_G:].T.astype(BF16), b_gate)
    g3 = gates.reshape(N_GATES, t // CHUNK, CHUNK).transpose(1, 0, 2)
    qk = _conv_call(hin, conv_w, conv_b, seq)
    h_f = _mlstm_call(qk, hin, g3, None, None, seq, reverse=False)
    m_out = _mlstm_call(qk, hin, g3, h_f, mh_norm_g, seq, reverse=True)
    kv = _mm_call(memb, w_mem_kv.astype(BF16), BF16, "mem_kv").reshape(seq.nseq, -1, 2 * X_WIDTH)
    pa_out = _pool_attn_call(hin, kv, pool_w.astype(BF16), pool_scale, seq)
    x1, x1b = _outproj_call(m_out, pa_out, x, w_out.astype(BF16), ln1_g, ln1_b)
    ids, gate, rank, counts = _router_call(x1b, w_router.T.astype(BF16), b_router)
    bm = MOE_BM
    dest, row_tok, blk_e = _dispatch_tables(ids, rank, counts, bm)
    ys = _expert_call(blk_e, row_tok, x1, w_gu, w_dn, bm)
    ctm = min(COMB_TM, t)
    dest3 = dest.reshape(TOP_K, t // ctm, ctm).transpose(1, 0, 2)
    return _combine_call(dest3, gate.T, x1, x1b, ys, w_sh_gu.astype(BF16), w_sh_dn.astype(BF16), ln2_g, ln2_b,
                         split_rows=seq.t1 if last else None)


def kernel(x_prompt, x_sample, mem_prompt, mem_sample, ln_in_g, ln_in_b, w_in, b_gate, conv_w, conv_b, mh_norm_g, pool_w, pool_scale, w_mem_kv, w_out, ln1_g, ln1_b, w_router, b_router, w_gu, w_dn, w_sh_gu, w_sh_dn, ln2_g, ln2_b):
    b1, s1, d = x_prompt.shape
    b2, s2, _ = x_sample.shape
    seq = SeqInfo(b1, s1, b2, s2)
    memb = jnp.concatenate([mem_prompt.reshape(-1, d), mem_sample.reshape(-1, d)], axis=0).astype(BF16)
    x, xb = _ln_call(x_prompt.reshape(b1 * s1, d), x_sample.reshape(b2 * s2, d), ln_in_g, ln_in_b)
    depth = w_in.shape[0]
    for l in range(depth):
        x, xb = _layer(x, xb, memb, seq, w_in[l], b_gate[l], conv_w[l], conv_b[l], mh_norm_g[l], pool_w[l],
                       pool_scale[l], w_mem_kv[l], w_out[l], ln1_g[l], ln1_b[l], w_router[l], b_router[l],
                       w_gu[l], w_dn[l], w_sh_gu[l], w_sh_dn[l], ln2_g[l], ln2_b[l], last=l == depth - 1)
    return (x.reshape(b1, s1, d), xb.reshape(b2, s2, d))
```

```python
import functools

import jax
import jax.numpy as jnp
from jax import lax
from jax.experimental import pallas as pl
from jax.experimental.pallas import tpu as pltpu

F32 = jnp.float32
BF16 = jnp.bfloat16
I32 = jnp.int32

DEPTH = 4
D_MODEL = 2048
M_HEADS = 4
M_WIDTH = 1024
M_HEAD_DIM = 256
CHUNK = 128
CONV_K = 5
POOL_WINDOWS = (2, 4, 8, 16)
POOL_WIDTH = 512
POOL_GROUP = 128
X_HEADS = 4
X_WIDTH = 512
X_HEAD_DIM = 128
OFF_Q = 0
OFF_K = OFF_Q + M_WIDTH
OFF_V = OFF_K + M_WIDTH
OFF_O = OFF_V + M_WIDTH
OFF_P = OFF_O + M_WIDTH
OFF_XQ = OFF_P + POOL_WIDTH
OFF_G = OFF_XQ + X_WIDTH
N_GATES = 4 * M_HEADS
N_EXPERTS = 64
TOP_K = 8
N_GROUPS = 8
GROUP_SIZE = N_EXPERTS // N_GROUPS
TOPK_GROUPS = 4
D_EXPERT = 512
D_SHARED = 512
ROUTED_SCALE = 2.5
ALPHA = (2 * DEPTH) ** 0.25
LN_EPS = 1e-5
HN_EPS = 1e-6

LANES = 128
SUBLANES = 8
HALO = SUBLANES
VMEM_LIMIT = 56 * 1024 * 1024

LN_TM = 512
MM_TM = 512
MM_TN = 1024
SEQ_TM = 512
MLSTM_CB = 4
OUT_TM = 256
ROUTE_TM = 512
MOE_BM = 256
COMB_TM = 128


def _cparams(*sem):
    return pltpu.CompilerParams(dimension_semantics=sem, vmem_limit_bytes=VMEM_LIMIT)


class SeqInfo:
    def __init__(self, b1, s1, b2, s2):
        self.b1, self.s1, self.b2, self.s2 = b1, s1, b2, s2
        self.t1 = b1 * s1
        self.t = b1 * s1 + b2 * s2
        self.nseq = b1 + b2

    def pos_len(self, row0):
        in1 = row0 < self.t1
        pos = jnp.where(in1, lax.rem(row0, self.s1), lax.rem(row0 - self.t1, self.s2))
        slen = jnp.where(in1, self.s1, self.s2)
        return pos, slen

    def seq_id(self, row0):
        in1 = row0 < self.t1
        return jnp.where(in1, row0 // self.s1, self.b1 + (row0 - self.t1) // self.s2)


def _layer_norm(y, g, b, eps):
    mu = jnp.mean(y, axis=-1, keepdims=True)
    d = y - mu
    var = jnp.mean(d * d, axis=-1, keepdims=True)
    return d * lax.rsqrt(var + eps) * g + b


def _ln_kernel(x1_ref, x2_ref, g_ref, b_ref, o_ref, ob_ref, *, n1):
    x = jnp.where(pl.program_id(0) < n1, x1_ref[...], x2_ref[...])
    y = _layer_norm(x, g_ref[...], b_ref[...], LN_EPS)
    o_ref[...] = y
    ob_ref[...] = y.astype(BF16)


def _ln_call(x1, x2, g, b):
    t1, d = x1.shape
    t2 = x2.shape[0]
    tm = min(LN_TM, t1, t2)
    n1, n2 = t1 // tm, t2 // tm
    t = t1 + t2
    return pl.pallas_call(
        functools.partial(_ln_kernel, n1=n1),
        grid=(n1 + n2,),
        in_specs=[pl.BlockSpec((tm, d), lambda i: (jnp.minimum(i, n1 - 1), 0)),
                  pl.BlockSpec((tm, d), lambda i: (jnp.maximum(i - n1, 0), 0)),
                  pl.BlockSpec((1, d), lambda i: (0, 0)),
                  pl.BlockSpec((1, d), lambda i: (0, 0))],
        out_specs=[pl.BlockSpec((tm, d), lambda i: (i, 0)),
                   pl.BlockSpec((tm, d), lambda i: (i, 0))],
        out_shape=[jax.ShapeDtypeStruct((t, d), F32), jax.ShapeDtypeStruct((t, d), BF16)],
        compiler_params=_cparams("arbitrary"),
        name="ln_in",
    )(x1, x2, g.reshape(1, d), b.reshape(1, d))


def _mm_kernel(x_ref, w_ref, o_ref):
    o_ref[...] = jnp.dot(x_ref[...], w_ref[...], preferred_element_type=F32).astype(o_ref.dtype)


def _mm_call(xb, wb, out_dtype, name):
    m, k = xb.shape
    n = wb.shape[1]
    tm = min(MM_TM, m)
    while m % tm:
        tm //= 2
    tn = min(MM_TN, n)
    return pl.pallas_call(
        _mm_kernel,
        grid=(n // tn, m // tm),
        in_specs=[pl.BlockSpec((tm, k), lambda j, i: (i, 0)),
                  pl.BlockSpec((k, tn), lambda j, i: (0, j))],
        out_specs=pl.BlockSpec((tm, tn), lambda j, i: (i, j)),
        out_shape=jax.ShapeDtypeStruct((m, n), out_dtype),
        compiler_params=_cparams("parallel", "parallel"),
        name=name,
    )(xb, wb)


GDIR = 2 * M_HEADS


def _chunk_scan(x, op, reverse):
    n = x.shape[1]
    pos = lax.rem(lax.broadcasted_iota(I32, x.shape, 1), CHUNK)
    s = 1
    while s < CHUNK:
        if reverse:
            x = jnp.where(pos < CHUNK - s, op(x, pltpu.roll(x, n - s, axis=1)), x)
        else:
            x = jnp.where(pos >= s, op(x, pltpu.roll(x, s, axis=1)), x)
        s *= 2
    return x


def _gates_kernel(x_ref, w_ref, b_ref, row_ref, col_ref):
    g = lax.dot_general(w_ref[...], x_ref[...], (((1,), (1,)), ((), ())),
                        preferred_element_type=F32) + b_ref[...]
    gi = g[0:GDIR]
    logf = jax.nn.log_sigmoid(g[GDIR:2 * GDIR])
    is_fwd = lax.broadcasted_iota(I32, gi.shape, 0) < M_HEADS
    pre = _chunk_scan(logf, jnp.add, False)
    suf = _chunk_scan(logf, jnp.add, True)
    b = jnp.where(is_fwd, pre, suf)
    total = pre + suf - logf
    ib = gi - b
    cmax = jnp.where(is_fwd, _chunk_scan(ib, jnp.maximum, False), _chunk_scan(ib, jnp.maximum, True))
    lw = total + ib
    lw_max = jnp.maximum(_chunk_scan(lw, jnp.maximum, False), _chunk_scan(lw, jnp.maximum, True))
    row_ref[...] = jnp.concatenate([ib, total, lw_max], axis=0)
    pad = jnp.zeros((LANES - 3 * GDIR, gi.shape[1]), F32)
    col_ref[...] = jnp.concatenate([b, cmax, lw, pad], axis=0).T


def _gates_call(xb, wgt, bg):
    t, d = xb.shape
    tm = min(MM_TM, t)
    return pl.pallas_call(
        _gates_kernel,
        grid=(t // tm,),
        in_specs=[pl.BlockSpec((tm, d), lambda i: (i, 0)),
                  pl.BlockSpec((N_GATES, d), lambda i: (0, 0)),
                  pl.BlockSpec((N_GATES, 1), lambda i: (0, 0))],
        out_specs=[pl.BlockSpec((3 * GDIR, tm), lambda i: (0, i)),
                   pl.BlockSpec((tm, LANES), lambda i: (i, 0))],
        out_shape=[jax.ShapeDtypeStruct((3 * GDIR, t), F32), jax.ShapeDtypeStruct((t, LANES), F32)],
        compiler_params=_cparams("parallel"),
        name="gates",
    )(xb, wgt, bg.reshape(N_GATES, 1))


def _gate_rows(a):
    h = M_HEADS
    return jnp.concatenate([a[0:h], a[2 * h:3 * h], a[h:2 * h], a[3 * h:4 * h]], axis=0)


def _halo_specs(tm, width, col_block, n_rows):
    per = tm // HALO
    last = n_rows // HALO - 1
    prev = pl.BlockSpec((HALO, width), lambda i, *_: (jnp.maximum(i * per - 1, 0), col_block))
    cur = pl.BlockSpec((tm, width), lambda i, *_: (i, col_block))
    nxt = pl.BlockSpec((HALO, width), lambda i, *_: (jnp.minimum((i + 1) * per, last), col_block))
    return prev, cur, nxt


def _extended_tile(prev_ref, cur_ref, next_ref, tm, seq):
    row0 = pl.program_id(0) * tm
    pos, slen = seq.pos_len(row0)
    prev = jnp.where(pos == 0, 0.0, prev_ref[...])
    nxt = jnp.where(pos + tm == slen, 0.0, next_ref[...])
    return jnp.concatenate([prev, cur_ref[...], nxt], axis=0), pos, slen


def _shift_rows(xe, d):
    n = xe.shape[0]
    if d == 0:
        return xe
    return pltpu.roll(xe, (n - d) % n, axis=0)


def _conv_kernel(prev_ref, cur_ref, next_ref, w_ref, b_ref, o_ref, *, tm, tc, seq, scale, transpose_out):
    xe, _, _ = _extended_tile(prev_ref, cur_ref, next_ref, tm, seq)
    acc = jnp.broadcast_to(b_ref[...], (tm, tc))
    for j in range(CONV_K):
        sh = _shift_rows(xe, j - CONV_K // 2)
        acc = acc + sh[HALO:HALO + tm] * w_ref[j:j + 1, :]
    y = acc * jax.nn.sigmoid(acc) * scale
    o_ref[...] = (y.T if transpose_out else y).astype(BF16)


def _conv_call(hin, conv_w, conv_b, seq, *, col0, scale, transpose_out):
    t = hin.shape[0]
    tm = min(SEQ_TM, seq.s1, seq.s2)
    tc = 512
    j0 = col0 // tc
    prev = pl.BlockSpec((HALO, tc), lambda i, j: (jnp.maximum(i * (tm // HALO) - 1, 0), j + j0))
    cur = pl.BlockSpec((tm, tc), lambda i, j: (i, j + j0))
    nxt = pl.BlockSpec((HALO, tc), lambda i, j: (jnp.minimum((i + 1) * (tm // HALO), t // HALO - 1), j + j0))
    if transpose_out:
        out_spec = pl.BlockSpec((tc, tm), lambda i, j: (j, i))
        out_shape = jax.ShapeDtypeStruct((M_WIDTH, t), BF16)
    else:
        out_spec = pl.BlockSpec((tm, tc), lambda i, j: (i, j))
        out_shape = jax.ShapeDtypeStruct((t, M_WIDTH), BF16)
    return pl.pallas_call(
        functools.partial(_conv_kernel, tm=tm, tc=tc, seq=seq, scale=scale, transpose_out=transpose_out),
        grid=(t // tm, M_WIDTH // tc),
        in_specs=[prev, cur, nxt,
                  pl.BlockSpec((CONV_K, tc), lambda i, j: (0, j + j0)),
                  pl.BlockSpec((1, tc), lambda i, j: (0, j + j0))],
        out_specs=out_spec,
        out_shape=out_shape,
        compiler_params=_cparams("parallel", "parallel"),
        name="conv_silu_t" if transpose_out else "conv_silu",
    )(hin, hin, hin, conv_w, conv_b.reshape(1, 2 * M_WIDTH))


def _mlstm_kernel(*refs, reverse, finalize, cb, seq):
    if finalize:
        q_ref, kt_ref, v_ref, gr_ref, gc_ref, hf_ref, uo_ref, ng_ref, o_ref, c_ref, m_ref = refs
    else:
        q_ref, kt_ref, v_ref, gr_ref, gc_ref, o_ref, c_ref, m_ref = refs
    L, DK, DV = CHUNK, M_HEAD_DIM, M_HEAD_DIM
    step = pl.program_id(0)
    nsteps = pl.num_programs(0)
    blk = (nsteps - 1 - step) if reverse else step
    ti = lax.broadcasted_iota(I32, (L, L), 0)
    si = lax.broadcasted_iota(I32, (L, L), 1)
    causal = (si >= ti) if reverse else (si <= ti)
    slot0 = M_HEADS if reverse else 0

    @pl.when(step == 0)
    def _():
        c_ref[...] = jnp.zeros_like(c_ref)
        m_ref[...] = jnp.zeros_like(m_ref)

    m_vals = [m_ref[h, 0:1, 0:1] for h in range(M_HEADS)]
    for jj in range(cb):
        j = (cb - 1 - jj) if reverse else jj
        rows = slice(j * L, (j + 1) * L)
        pos, slen = seq.pos_len((blk * cb + j) * L)
        at_start = (pos + L == slen) if reverse else (pos == 0)
        grow = gr_ref[j]
        ones = jnp.ones((L, LANES), F32)
        for h in range(M_HEADS):
            cols = slice(h * DK, (h + 1) * DK)
            sl = slot0 + h
            q = q_ref[rows, cols]
            kt = kt_ref[cols, rows]
            v32 = v_ref[rows, cols]
            ib_row = grow[sl:sl + 1, :]
            g_tot = grow[GDIR + sl:GDIR + sl + 1, 0:1]
            lw_max = grow[2 * GDIR + sl:2 * GDIR + sl + 1, 0:1]
            b_col = gc_ref[rows, sl:sl + 1]
            cmax_col = gc_ref[rows, GDIR + sl:GDIR + sl + 1]
            lw_col = gc_ref[rows, 2 * GDIR + sl:2 * GDIR + sl + 1]
            m_prev = jnp.where(at_start, 0.0, m_vals[h])
            m_rel = jnp.maximum(m_prev, cmax_col)
            m_t = b_col + m_rel
            dexp = jnp.exp(jnp.where(causal, ib_row - m_rel, -jnp.inf))
            s = jnp.dot(q, kt, preferred_element_type=F32)
            a = (s * dexp).astype(BF16)
            inter = jnp.exp(m_prev - m_rel)
            c_aug = jnp.where(at_start, 0.0, c_ref[h])
            qc = jnp.dot(q, c_aug.astype(BF16), preferred_element_type=F32)
            v_ext = jnp.concatenate([v32, ones], axis=1).astype(BF16)
            nd = jnp.dot(a, v_ext, preferred_element_type=F32) + inter * qc
            den = nd[:, DV:DV + 1]
            hh = nd[:, :DV] / jnp.maximum(jnp.abs(den), jnp.exp(-m_t))
            m_new = jnp.maximum(g_tot + m_prev, lw_max)
            wk_col = jnp.exp(lw_col - m_new)
            decay = jnp.exp(g_tot + m_prev - m_new)
            v_aug = (jnp.concatenate([v32, ones], axis=1) * wk_col).astype(BF16)
            upd = jnp.dot(kt, v_aug, preferred_element_type=F32)
            c_ref[h] = decay * c_aug + upd
            m_vals[h] = m_new
            if finalize:
                hs = hf_ref[rows, cols] + hh
                mu = jnp.mean(hs, axis=1, keepdims=True)
                d = hs - mu
                var = jnp.mean(d * d, axis=1, keepdims=True)
                hn = d * lax.rsqrt(var + HN_EPS) * ng_ref[:, cols]
                o_ref[rows, cols] = (hn * jax.nn.sigmoid(uo_ref[rows, cols])).astype(o_ref.dtype)
            else:
                o_ref[rows, cols] = hh
    for h in range(M_HEADS):
        m_ref[h] = jnp.broadcast_to(m_vals[h], (SUBLANES, LANES))


def _mlstm_call(q, kt, hin, gr3, gc, hf, norm_g, seq, *, reverse):
    t = q.shape[0]
    cb = MLSTM_CB
    while (seq.s1 // CHUNK) % cb or (seq.s2 // CHUNK) % cb:
        cb //= 2
    rows = cb * CHUNK
    nsteps = t // rows
    finalize = reverse
    blk = (lambda s: nsteps - 1 - s) if reverse else (lambda s: s)
    vblk = OFF_V // M_WIDTH
    oblk = OFF_O // M_WIDTH
    in_specs = [pl.BlockSpec((rows, M_WIDTH), lambda s: (blk(s), 0)),
                pl.BlockSpec((M_WIDTH, rows), lambda s: (0, blk(s))),
                pl.BlockSpec((rows, M_WIDTH), lambda s: (blk(s), vblk)),
                pl.BlockSpec((cb, 3 * GDIR, CHUNK), lambda s: (blk(s), 0, 0)),
                pl.BlockSpec((rows, LANES), lambda s: (blk(s), 0))]
    args = [q, kt, hin, gr3, gc]
    if finalize:
        in_specs += [pl.BlockSpec((rows, M_WIDTH), lambda s: (blk(s), 0)),
                     pl.BlockSpec((rows, M_WIDTH), lambda s: (blk(s), oblk)),
                     pl.BlockSpec((1, M_WIDTH), lambda s: (0, 0))]
        args += [hf, hin, norm_g.reshape(1, M_WIDTH)]
    return pl.pallas_call(
        functools.partial(_mlstm_kernel, reverse=reverse, finalize=finalize, cb=cb, seq=seq),
        grid=(nsteps,),
        in_specs=in_specs,
        out_specs=pl.BlockSpec((rows, M_WIDTH), lambda s: (blk(s), 0)),
        out_shape=jax.ShapeDtypeStruct((t, M_WIDTH), BF16 if finalize else F32),
        scratch_shapes=[pltpu.VMEM((M_HEADS, M_HEAD_DIM, M_HEAD_DIM + LANES), F32),
                        pltpu.VMEM((M_HEADS, SUBLANES, LANES), F32)],
        compiler_params=_cparams("arbitrary"),
        name="mlstm_bwd" if reverse else "mlstm_fwd",
    )(*args)


def _pool_attn_kernel(prev_ref, cur_ref, next_ref, xq_ref, kv_ref, pw_ref, ps_ref, o_ref, *, tm, seq):
    xe, pos0, slen = _extended_tile(prev_ref, cur_ref, next_ref, tm, seq)
    pos = pos0 + lax.broadcasted_iota(I32, (tm, 1), 0)
    for gi, w in enumerate(POOL_WINDOWS):
        cols = slice(gi * POOL_GROUP, (gi + 1) * POOL_GROUP)
        u = xe[:, cols]
        win = _shift_rows(u, -1) + u
        half = 1
        while 2 * half < w:
            win = _shift_rows(win, -half) + _shift_rows(win, half)
            half *= 2
        cnt = (jnp.minimum(pos + w // 2, slen) - jnp.maximum(pos - w // 2, 0)).astype(F32)
        mean = win[HALO:HALO + tm] / cnt
        diff = (mean - u[HALO:HALO + tm]).astype(BF16)
        y = jnp.dot(diff, pw_ref[gi], preferred_element_type=F32)
        o_ref[:, cols] = (y * ps_ref[:, cols]).astype(o_ref.dtype)
    xq = xq_ref[...].astype(BF16)
    for h in range(X_HEADS):
        cols = slice(h * X_HEAD_DIM, (h + 1) * X_HEAD_DIM)
        kh = kv_ref[0, :, h * X_HEAD_DIM:(h + 1) * X_HEAD_DIM]
        vh = kv_ref[0, :, X_WIDTH + h * X_HEAD_DIM:X_WIDTH + (h + 1) * X_HEAD_DIM]
        s = lax.dot_general(xq[:, cols], kh, (((1,), (1,)), ((), ())),
                            preferred_element_type=F32) * (X_HEAD_DIM ** -0.5)
        e = jnp.exp(s - jnp.max(s, axis=1, keepdims=True))
        p = e / jnp.sum(e, axis=1, keepdims=True)
        y = jnp.dot(p.astype(BF16), vh, preferred_element_type=F32)
        o_ref[:, POOL_WIDTH + h * X_HEAD_DIM:POOL_WIDTH + (h + 1) * X_HEAD_DIM] = y.astype(o_ref.dtype)


def _pool_attn_call(hin, kv, pool_w, pool_scale, seq):
    t = hin.shape[0]
    tm = min(SEQ_TM, seq.s1, seq.s2)
    n_mem = kv.shape[1]
    prev, cur, nxt = _halo_specs(tm, POOL_WIDTH, OFF_P // POOL_WIDTH, t)
    return pl.pallas_call(
        functools.partial(_pool_attn_kernel, tm=tm, seq=seq),
        grid=(t // tm,),
        in_specs=[prev, cur, nxt,
                  pl.BlockSpec((tm, X_WIDTH), lambda i: (i, OFF_XQ // X_WIDTH)),
                  pl.BlockSpec((1, n_mem, 2 * X_WIDTH), lambda i: (seq.seq_id(i * tm), 0, 0)),
                  pl.BlockSpec((len(POOL_WINDOWS), POOL_GROUP, POOL_GROUP), lambda i: (0, 0, 0)),
                  pl.BlockSpec((1, POOL_WIDTH), lambda i: (0, 0))],
        out_specs=pl.BlockSpec((tm, POOL_WIDTH + X_WIDTH), lambda i: (i, 0)),
        out_shape=jax.ShapeDtypeStruct((t, POOL_WIDTH + X_WIDTH), BF16),
        compiler_params=_cparams("parallel"),
        name="pool_attn",
    )(hin, hin, hin, hin, kv, pool_w, pool_scale.reshape(1, POOL_WIDTH))


def _outproj_kernel(m_ref, pa_ref, x_ref, w_ref, g_ref, b_ref, o_ref, ob_ref):
    acc = jnp.dot(m_ref[...], w_ref[0:M_WIDTH, :], preferred_element_type=F32)
    acc = acc + jnp.dot(pa_ref[...], w_ref[M_WIDTH:, :], preferred_element_type=F32)
    y = _layer_norm(ALPHA * x_ref[...] + acc, g_ref[...], b_ref[...], LN_EPS)
    o_ref[...] = y
    ob_ref[...] = y.astype(BF16)


def _outproj_call(m_out, pa_out, x, w_out, g, b):
    t, d = x.shape
    tm = min(OUT_TM, t)
    return pl.pallas_call(
        _outproj_kernel,
        grid=(t // tm,),
        in_specs=[pl.BlockSpec((tm, M_WIDTH), lambda i: (i, 0)),
                  pl.BlockSpec((tm, POOL_WIDTH + X_WIDTH), lambda i: (i, 0)),
                  pl.BlockSpec((tm, d), lambda i: (i, 0)),
                  pl.BlockSpec((d, d), lambda i: (0, 0)),
                  pl.BlockSpec((1, d), lambda i: (0, 0)),
                  pl.BlockSpec((1, d), lambda i: (0, 0))],
        out_specs=[pl.BlockSpec((tm, d), lambda i: (i, 0)),
                   pl.BlockSpec((tm, d), lambda i: (i, 0))],
        out_shape=[jax.ShapeDtypeStruct((t, d), F32), jax.ShapeDtypeStruct((t, d), BF16)],
        compiler_params=_cparams("parallel"),
        name="outproj_ln",
    )(m_out, pa_out, x, w_out, g.reshape(1, d), b.reshape(1, d))


def _first_index_of_max(vals, idx, axis_size):
    mx = jnp.max(vals, axis=0, keepdims=True)
    first = jnp.min(jnp.where(vals == mx, idx, axis_size), axis=0, keepdims=True)
    return mx, first


def _router_kernel(x_ref, w_ref, b_ref, ids_ref, gate_ref, rank_ref, cnt_ref, carry_ref, *, tm):
    @pl.when(pl.program_id(0) == 0)
    def _():
        carry_ref[...] = jnp.zeros_like(carry_ref)

    logits = lax.dot_general(w_ref[...], x_ref[...], (((1,), (1,)), ((), ())),
                             preferred_element_type=F32)
    s = jax.nn.sigmoid(logits)
    sel = s + b_ref[...]
    neg = -jnp.inf
    sub = lax.broadcasted_iota(I32, (GROUP_SIZE, tm), 0)
    gscores = []
    for g in range(N_GROUPS):
        blk = sel[g * GROUP_SIZE:(g + 1) * GROUP_SIZE, :]
        m1, i1 = _first_index_of_max(blk, sub, GROUP_SIZE)
        m2 = jnp.max(jnp.where(sub == i1, neg, blk), axis=0, keepdims=True)
        gscores.append(m1 + m2)
    gs = jnp.concatenate(gscores, axis=0)
    gidx = lax.broadcasted_iota(I32, (N_GROUPS, tm), 0)
    gmask = jnp.zeros((N_GROUPS, tm), jnp.bool_)
    for _ in range(TOPK_GROUPS):
        _, gi = _first_index_of_max(gs, gidx, N_GROUPS)
        hit = gidx == gi
        gmask = jnp.logical_or(gmask, hit)
        gs = jnp.where(hit, neg, gs)
    gmask_f = gmask.astype(F32)
    masked = jnp.concatenate(
        [jnp.where(gmask_f[g:g + 1, :] > 0.5, sel[g * GROUP_SIZE:(g + 1) * GROUP_SIZE, :], neg)
         for g in range(N_GROUPS)], axis=0)
    eidx = lax.broadcasted_iota(I32, (N_EXPERTS, tm), 0)
    chosen = jnp.zeros((N_EXPERTS, tm), F32)
    ids, gates, hits = [], [], []
    for _ in range(TOP_K):
        _, ei = _first_index_of_max(masked, eidx, N_EXPERTS)
        hit = eidx == ei
        ids.append(ei)
        gates.append(jnp.sum(jnp.where(hit, s, 0.0), axis=0, keepdims=True))
        hits.append(hit)
        chosen = jnp.where(hit, 1.0, chosen)
        masked = jnp.where(hit, neg, masked)
    gate = jnp.concatenate(gates, axis=0)
    gate = gate / jnp.sum(gate, axis=0, keepdims=True) * ROUTED_SCALE
    ids_ref[...] = jnp.concatenate(ids, axis=0)
    gate_ref[...] = gate
    r = lax.broadcasted_iota(I32, (tm, tm), 0)
    c = lax.broadcasted_iota(I32, (tm, tm), 1)
    before = (r < c).astype(BF16)
    cnt = jnp.dot(chosen.astype(BF16), before, preferred_element_type=F32) + carry_ref[:, 0:1]
    ranks = [jnp.sum(jnp.where(hit, cnt, 0.0), axis=0, keepdims=True) for hit in hits]
    rank_ref[...] = jnp.concatenate(ranks, axis=0).astype(I32)
    total = carry_ref[:, 0:1] + jnp.sum(chosen, axis=1, keepdims=True)
    carry_ref[...] = jnp.broadcast_to(total, carry_ref.shape)
    cnt_ref[...] = jnp.broadcast_to(total, cnt_ref.shape)


def _router_call(xb, wrt, br):
    t, d = xb.shape
    tm = min(ROUTE_TM, t)
    return pl.pallas_call(
        functools.partial(_router_kernel, tm=tm),
        grid=(t // tm,),
        in_specs=[pl.BlockSpec((tm, d), lambda i: (i, 0)),
                  pl.BlockSpec((N_EXPERTS, d), lambda i: (0, 0)),
                  pl.BlockSpec((N_EXPERTS, 1), lambda i: (0, 0))],
        out_specs=[pl.BlockSpec((TOP_K, tm), lambda i: (0, i)),
                   pl.BlockSpec((TOP_K, tm), lambda i: (0, i)),
                   pl.BlockSpec((TOP_K, tm), lambda i: (0, i)),
                   pl.BlockSpec((N_EXPERTS, LANES), lambda i: (0, 0))],
        out_shape=[jax.ShapeDtypeStruct((TOP_K, t), I32),
                   jax.ShapeDtypeStruct((TOP_K, t), F32),
                   jax.ShapeDtypeStruct((TOP_K, t), I32),
                   jax.ShapeDtypeStruct((N_EXPERTS, LANES), F32)],
        scratch_shapes=[pltpu.VMEM((N_EXPERTS, LANES), F32)],
        compiler_params=_cparams("arbitrary"),
        name="router",
    )(xb, wrt, br.reshape(N_EXPERTS, 1))


def _expert_kernel(blk_e_ref, tok_ref, tok_next_ref, x_hbm, wgu_ref, wdn_ref, o_ref,
                   xbuf0, xbuf1, wgu_b, wdn_b, sem, *, bm, n_blocks):
    b = pl.program_id(0)
    bufs = (xbuf0, xbuf1)

    def issue(ref, s):
        for r in range(bm):
            pltpu.make_async_copy(x_hbm.at[pl.ds(ref[0, 0, r], 1)], bufs[s].at[pl.ds(r, 1)], sem.at[s]).start()

    def wait(s):
        pltpu.make_async_copy(x_hbm.at[pl.ds(0, bm)], bufs[s], sem.at[s]).wait()

    @pl.when(b == 0)
    def _():
        issue(tok_ref, 0)

    @pl.when(jnp.logical_or(b == 0, blk_e_ref[b] != blk_e_ref[jnp.maximum(b - 1, 0)]))
    def _():
        wgu_b[...] = wgu_ref[0, 0].astype(BF16)
        wdn_b[...] = wdn_ref[0, 0].astype(BF16)

    def step(cur):
        wait(cur)
        issue(tok_next_ref, 1 - cur)
        x = bufs[cur][...].astype(BF16)
        gu = jnp.dot(x, wgu_b[...], preferred_element_type=F32)
        g = gu[:, :D_EXPERT]
        u = gu[:, D_EXPERT:]
        hmid = (g * jax.nn.sigmoid(g) * u).astype(BF16)
        o_ref[...] = jnp.dot(hmid, wdn_b[...], preferred_element_type=F32)

    for cur in range(2):
        pl.when(lax.rem(b, 2) == cur)(functools.partial(step, cur))

    @pl.when(b == n_blocks - 1)
    def _():
        wait(n_blocks % 2)


def _expert_call(blk_e, row_tok, x, wgu, wdn, bm, layer):
    n_blocks = row_tok.shape[0]
    d = x.shape[1]
    grid_spec = pltpu.PrefetchScalarGridSpec(
        num_scalar_prefetch=1,
        grid=(n_blocks,),
        in_specs=[pl.BlockSpec((1, 1, bm), lambda b, be: (b, 0, 0), memory_space=pltpu.SMEM),
                  pl.BlockSpec((1, 1, bm), lambda b, be: (jnp.minimum(b + 1, n_blocks - 1), 0, 0),
                               memory_space=pltpu.SMEM),
                  pl.BlockSpec(memory_space=pl.ANY),
                  pl.BlockSpec((1, 1, d, 2 * D_EXPERT), lambda b, be: (layer, be[b], 0, 0)),
                  pl.BlockSpec((1, 1, D_EXPERT, d), lambda b, be: (layer, be[b], 0, 0))],
        out_specs=pl.BlockSpec((bm, d), lambda b, be: (b, 0)),
        scratch_shapes=[pltpu.VMEM((bm, d), F32), pltpu.VMEM((bm, d), F32),
                        pltpu.VMEM((d, 2 * D_EXPERT), BF16), pltpu.VMEM((D_EXPERT, d), BF16),
                        pltpu.SemaphoreType.DMA((2,))],
    )
    return pl.pallas_call(
        functools.partial(_expert_kernel, bm=bm, n_blocks=n_blocks),
        grid_spec=grid_spec,
        out_shape=jax.ShapeDtypeStruct((n_blocks * bm, d), F32),
        compiler_params=_cparams("arbitrary"),
        name="experts",
    )(blk_e, row_tok, row_tok, x, wgu, wdn)


def _combine_kernel(dest_ref, dest_next_ref, gate_ref, x_ref, xb_ref, ys_hbm, wsg_ref, wsd_ref, g_ref, b_ref,
                    *rest, tm, n1, nsteps):
    if n1 is None:
        o_ref, ob_ref, buf0, buf1, base_ref, sem = rest
    else:
        o1_ref, o2_ref, buf0, buf1, base_ref, sem = rest
    i = pl.program_id(0)
    bufs = (buf0, buf1)

    def issue(ref, s):
        for r in range(tm):
            for k in range(TOP_K):
                pltpu.make_async_copy(ys_hbm.at[pl.ds(ref[0, k, r], 1)],
                                      bufs[s].at[pl.ds(k * tm + r, 1)], sem.at[s]).start(priority=k % 2)

    def wait(s):
        pltpu.make_async_copy(ys_hbm.at[pl.ds(0, TOP_K * tm)], bufs[s], sem.at[s]).wait()

    @pl.when(i == 0)
    def _():
        issue(dest_ref, 0)

    gu = jnp.dot(xb_ref[...], wsg_ref[...], preferred_element_type=F32)
    g = gu[:, :D_SHARED]
    u = gu[:, D_SHARED:]
    hmid = (g * jax.nn.sigmoid(g) * u).astype(BF16)
    base_ref[...] = ALPHA * x_ref[...] + jnp.dot(hmid, wsd_ref[...], preferred_element_type=F32)

    def step(cur):
        wait(cur)
        issue(dest_next_ref, 1 - cur)
        acc = gate_ref[:, 0:1] * bufs[cur][0:tm, :]
        for k in range(1, TOP_K):
            acc = acc + gate_ref[:, k:k + 1] * bufs[cur][k * tm:(k + 1) * tm, :]
        y = _layer_norm(base_ref[...] + acc, g_ref[...], b_ref[...], LN_EPS)
        if n1 is None:
            o_ref[...] = y
            ob_ref[...] = y.astype(BF16)
        else:
            @pl.when(i < n1)
            def _():
                o1_ref[...] = y

            @pl.when(i >= n1)
            def _():
                o2_ref[...] = y

    for cur in range(2):
        pl.when(lax.rem(i, 2) == cur)(functools.partial(step, cur))

    @pl.when(i == nsteps - 1)
    def _():
        wait(nsteps % 2)


def _combine_call(dest3, gate_tk, x, xb, ys, wsg, wsd, g, b, split_rows=None):
    t, d = x.shape
    tm = dest3.shape[2]
    nsteps = t // tm
    if split_rows is None:
        n1 = None
        out_specs = [pl.BlockSpec((tm, d), lambda i: (i, 0)), pl.BlockSpec((tm, d), lambda i: (i, 0))]
        out_shape = [jax.ShapeDtypeStruct((t, d), F32), jax.ShapeDtypeStruct((t, d), BF16)]
    else:
        n1 = split_rows // tm
        out_specs = [pl.BlockSpec((tm, d), lambda i: (jnp.minimum(i, n1 - 1), 0)),
                     pl.BlockSpec((tm, d), lambda i: (jnp.maximum(i - n1, 0), 0))]
        out_shape = [jax.ShapeDtypeStruct((split_rows, d), F32), jax.ShapeDtypeStruct((t - split_rows, d), F32)]
    return pl.pallas_call(
        functools.partial(_combine_kernel, tm=tm, n1=n1, nsteps=nsteps),
        grid=(nsteps,),
        in_specs=[pl.BlockSpec((1, TOP_K, tm), lambda i: (i, 0, 0), memory_space=pltpu.SMEM),
                  pl.BlockSpec((1, TOP_K, tm), lambda i: (jnp.minimum(i + 1, nsteps - 1), 0, 0),
                               memory_space=pltpu.SMEM),
                  pl.BlockSpec((tm, TOP_K), lambda i: (i, 0)),
                  pl.BlockSpec((tm, d), lambda i: (i, 0)),
                  pl.BlockSpec((tm, d), lambda i: (i, 0)),
                  pl.BlockSpec(memory_space=pl.ANY),
                  pl.BlockSpec((d, 2 * D_SHARED), lambda i: (0, 0)),
                  pl.BlockSpec((D_SHARED, d), lambda i: (0, 0)),
                  pl.BlockSpec((1, d), lambda i: (0, 0)),
                  pl.BlockSpec((1, d), lambda i: (0, 0))],
        out_specs=out_specs,
        out_shape=out_shape,
        scratch_shapes=[pltpu.VMEM((TOP_K * tm, d), F32), pltpu.VMEM((TOP_K * tm, d), F32),
                        pltpu.VMEM((tm, d), F32), pltpu.SemaphoreType.DMA((2,))],
        compiler_params=_cparams("arbitrary"),
        name="combine_ln",
    )(dest3, dest3, gate_tk, x, xb, ys, wsg, wsd, g.reshape(1, d), b.reshape(1, d))


def _dispatch_tables(ids, rank, counts_f, bm):
    k, t = ids.shape
    tk = k * t
    n_blocks = -(-tk // bm) + N_EXPERTS
    n_fill = n_blocks * bm - tk
    counts = counts_f[:, 0].astype(I32)
    padded = (counts + bm - 1) // bm * bm
    pends = jnp.cumsum(padded)
    pstart = pends - padded
    experts = jnp.arange(N_EXPERTS, dtype=I32)
    dest = jnp.sum(jnp.where(ids[:, :, None] == experts, pstart, 0), axis=-1) + rank
    stride = t + 1
    tok = jnp.broadcast_to(jnp.arange(t, dtype=I32)[None, :], (k, t))
    real_keys = (ids * stride + tok).reshape(tk)
    cum_need = jnp.cumsum(padded - counts)
    fill_e = jnp.sum(cum_need[None, :] <= jnp.arange(n_fill, dtype=I32)[:, None], axis=1)
    keys = jnp.sort(jnp.concatenate([real_keys, fill_e.astype(I32) * stride + t]))
    row_tok = lax.rem(keys, stride)
    row_tok = jnp.where(row_tok == t, 0, row_tok)
    blk_first = jnp.arange(n_blocks, dtype=I32) * bm
    blk_e = jnp.minimum(jnp.sum(pends[None, :] <= blk_first[:, None], axis=1), N_EXPERTS - 1).astype(I32)
    return dest, row_tok.reshape(n_blocks, 1, bm), blk_e


def _layer(x, xb, memb, seq, layer, w_in, b_gate, conv_w, conv_b, mh_norm_g, pool_w, pool_scale, w_mem_kv, w_out,
           ln1_g, ln1_b, w_router, b_router, w_gu_all, w_dn_all, w_sh_gu, w_sh_dn, ln2_g, ln2_b, last):
    t = x.shape[0]
    hin = _mm_call(xb, w_in[:, :OFF_G].astype(BF16), F32, "in_proj")
    g_rows, g_cols = _gates_call(xb, _gate_rows(w_in[:, OFF_G:].T).astype(BF16), _gate_rows(b_gate))
    gr3 = g_rows.reshape(3 * GDIR, t // CHUNK, CHUNK).transpose(1, 0, 2)
    q = _conv_call(hin, conv_w, conv_b, seq, col0=OFF_Q, scale=1.0, transpose_out=False)
    kt = _conv_call(hin, conv_w, conv_b, seq, col0=OFF_K, scale=M_HEAD_DIM ** -0.5, transpose_out=True)
    h_f = _mlstm_call(q, kt, hin, gr3, g_cols, None, None, seq, reverse=False)
    m_out = _mlstm_call(q, kt, hin, gr3, g_cols, h_f, mh_norm_g, seq, reverse=True)
    kv = _mm_call(memb, w_mem_kv.astype(BF16), BF16, "mem_kv").reshape(seq.nseq, -1, 2 * X_WIDTH)
    pa_out = _pool_attn_call(hin, kv, pool_w.astype(BF16), pool_scale, seq)
    x1, x1b = _outproj_call(m_out, pa_out, x, w_out.astype(BF16), ln1_g, ln1_b)
    ids, gate, rank, counts = _router_call(x1b, w_router.T.astype(BF16), b_router)
    bm = MOE_BM
    dest, row_tok, blk_e = _dispatch_tables(ids, rank, counts, bm)
    ys = _expert_call(blk_e, row_tok, x1, w_gu_all, w_dn_all, bm, layer)
    ctm = min(COMB_TM, t)
    dest3 = dest.reshape(TOP_K, t // ctm, ctm).transpose(1, 0, 2)
    return _combine_call(dest3, gate.T, x1, x1b, ys, w_sh_gu.astype(BF16), w_sh_dn.astype(BF16), ln2_g, ln2_b,
                         split_rows=seq.t1 if last else None)


def kernel(x_prompt, x_sample, mem_prompt, mem_sample, ln_in_g, ln_in_b, w_in, b_gate, conv_w, conv_b, mh_norm_g, pool_w, pool_scale, w_mem_kv, w_out, ln1_g, ln1_b, w_router, b_router, w_gu, w_dn, w_sh_gu, w_sh_dn, ln2_g, ln2_b):
    b1, s1, d = x_prompt.shape
    b2, s2, _ = x_sample.shape
    seq = SeqInfo(b1, s1, b2, s2)
    memb = jnp.concatenate([mem_prompt.reshape(-1, d), mem_sample.reshape(-1, d)], axis=0).astype(BF16)
    x, xb = _ln_call(x_prompt.reshape(b1 * s1, d), x_sample.reshape(b2 * s2, d), ln_in_g, ln_in_b)
    depth = w_in.shape[0]
    for l in range(depth):
        x, xb = _layer(x, xb, memb, seq, l, w_in[l], b_gate[l], conv_w[l], conv_b[l], mh_norm_g[l], pool_w[l],
                       pool_scale[l], w_mem_kv[l], w_out[l], ln1_g[l], ln1_b[l], w_router[l], b_router[l],
                       w_gu, w_dn, w_sh_gu[l], w_sh_dn[l], ln2_g[l], ln2_b[l], last=l == depth - 1)
    return (x.reshape(b1, s1, d), xb.reshape(b2, s2, d))
```

```python
import functools

import jax
import jax.numpy as jnp
from jax import lax
from jax.experimental import pallas as pl
from jax.experimental.pallas import tpu as pltpu

F32 = jnp.float32
BF16 = jnp.bfloat16
I32 = jnp.int32

DEPTH = 4
D_MODEL = 2048
M_HEADS = 4
M_WIDTH = 1024
M_HEAD_DIM = 256
CHUNK = 128
CONV_K = 5
POOL_WINDOWS = (2, 4, 8, 16)
POOL_WIDTH = 512
POOL_GROUP = 128
X_HEADS = 4
X_WIDTH = 512
X_HEAD_DIM = 128
OFF_Q = 0
OFF_K = OFF_Q + M_WIDTH
OFF_V = OFF_K + M_WIDTH
OFF_O = OFF_V + M_WIDTH
OFF_P = OFF_O + M_WIDTH
OFF_XQ = OFF_P + POOL_WIDTH
OFF_G = OFF_XQ + X_WIDTH
N_GATES = 4 * M_HEADS
N_EXPERTS = 64
TOP_K = 8
N_GROUPS = 8
GROUP_SIZE = N_EXPERTS // N_GROUPS
TOPK_GROUPS = 4
D_EXPERT = 512
D_SHARED = 512
ROUTED_SCALE = 2.5
ALPHA = (2 * DEPTH) ** 0.25
LN_EPS = 1e-5
HN_EPS = 1e-6

LANES = 128
SUBLANES = 8
HALO = SUBLANES
VMEM_LIMIT = 56 * 1024 * 1024

LN_TM = 512
MM_TM = 512
MM_TN = 1024
SEQ_TM = 512
MLSTM_CB = 4
OUT_TM = 256
ROUTE_TM = 512
MOE_BM = 256
COMB_TM = 128


def _cparams(*sem):
    return pltpu.CompilerParams(dimension_semantics=sem, vmem_limit_bytes=VMEM_LIMIT)


class SeqInfo:
    def __init__(self, b1, s1, b2, s2):
        self.b1, self.s1, self.b2, self.s2 = b1, s1, b2, s2
        self.t1 = b1 * s1
        self.t = b1 * s1 + b2 * s2
        self.nseq = b1 + b2

    def pos_len(self, row0):
        in1 = row0 < self.t1
        pos = jnp.where(in1, lax.rem(row0, self.s1), lax.rem(row0 - self.t1, self.s2))
        slen = jnp.where(in1, self.s1, self.s2)
        return pos, slen

    def seq_id(self, row0):
        in1 = row0 < self.t1
        return jnp.where(in1, row0 // self.s1, self.b1 + (row0 - self.t1) // self.s2)


def _layer_norm(y, g, b, eps):
    mu = jnp.mean(y, axis=-1, keepdims=True)
    d = y - mu
    var = jnp.mean(d * d, axis=-1, keepdims=True)
    return d * lax.rsqrt(var + eps) * g + b


def _ln_kernel(x1_ref, x2_ref, g_ref, b_ref, o_ref, ob_ref, *, n1):
    x = jnp.where(pl.program_id(0) < n1, x1_ref[...], x2_ref[...])
    y = _layer_norm(x, g_ref[...], b_ref[...], LN_EPS)
    o_ref[...] = y
    ob_ref[...] = y.astype(BF16)


def _ln_call(x1, x2, g, b):
    t1, d = x1.shape
    t2 = x2.shape[0]
    tm = min(LN_TM, t1, t2)
    n1, n2 = t1 // tm, t2 // tm
    t = t1 + t2
    return pl.pallas_call(
        functools.partial(_ln_kernel, n1=n1),
        grid=(n1 + n2,),
        in_specs=[pl.BlockSpec((tm, d), lambda i: (jnp.minimum(i, n1 - 1), 0)),
                  pl.BlockSpec((tm, d), lambda i: (jnp.maximum(i - n1, 0), 0)),
                  pl.BlockSpec((1, d), lambda i: (0, 0)),
                  pl.BlockSpec((1, d), lambda i: (0, 0))],
        out_specs=[pl.BlockSpec((tm, d), lambda i: (i, 0)),
                   pl.BlockSpec((tm, d), lambda i: (i, 0))],
        out_shape=[jax.ShapeDtypeStruct((t, d), F32), jax.ShapeDtypeStruct((t, d), BF16)],
        compiler_params=_cparams("arbitrary"),
        name="ln_in",
    )(x1, x2, g.reshape(1, d), b.reshape(1, d))


def _mm_kernel(x_ref, w_ref, o_ref):
    o_ref[...] = jnp.dot(x_ref[...], w_ref[...], preferred_element_type=F32).astype(o_ref.dtype)


def _mm_call(xb, wb, out_dtype, name):
    m, k = xb.shape
    n = wb.shape[1]
    tm = min(MM_TM, m)
    while m % tm:
        tm //= 2
    tn = min(MM_TN, n)
    return pl.pallas_call(
        _mm_kernel,
        grid=(n // tn, m // tm),
        in_specs=[pl.BlockSpec((tm, k), lambda j, i: (i, 0)),
                  pl.BlockSpec((k, tn), lambda j, i: (0, j))],
        out_specs=pl.BlockSpec((tm, tn), lambda j, i: (i, j)),
        out_shape=jax.ShapeDtypeStruct((m, n), out_dtype),
        compiler_params=_cparams("parallel", "parallel"),
        name=name,
    )(xb, wb)


GDIR = 2 * M_HEADS


def _chunk_scan(x, op, reverse):
    n = x.shape[1]
    pos = lax.rem(lax.broadcasted_iota(I32, x.shape, 1), CHUNK)
    s = 1
    while s < CHUNK:
        if reverse:
            x = jnp.where(pos < CHUNK - s, op(x, pltpu.roll(x, n - s, axis=1)), x)
        else:
            x = jnp.where(pos >= s, op(x, pltpu.roll(x, s, axis=1)), x)
        s *= 2
    return x


def _gates_kernel(x_ref, w_ref, b_ref, row_ref, col_ref):
    g = lax.dot_general(w_ref[...], x_ref[...], (((1,), (1,)), ((), ())),
                        preferred_element_type=F32) + b_ref[...]
    gi = g[0:GDIR]
    logf = jax.nn.log_sigmoid(g[GDIR:2 * GDIR])
    is_fwd = lax.broadcasted_iota(I32, gi.shape, 0) < M_HEADS
    pre = _chunk_scan(logf, jnp.add, False)
    suf = _chunk_scan(logf, jnp.add, True)
    b = jnp.where(is_fwd, pre, suf)
    total = pre + suf - logf
    ib = gi - b
    cmax = jnp.where(is_fwd, _chunk_scan(ib, jnp.maximum, False), _chunk_scan(ib, jnp.maximum, True))
    lw = total + ib
    lw_max = jnp.maximum(_chunk_scan(lw, jnp.maximum, False), _chunk_scan(lw, jnp.maximum, True))
    row_ref[...] = jnp.concatenate([ib, total, lw_max], axis=0)
    pad = jnp.zeros((LANES - 3 * GDIR, gi.shape[1]), F32)
    col_ref[...] = jnp.concatenate([b, cmax, lw, pad], axis=0).T


def _gates_call(xb, wgt, bg):
    t, d = xb.shape
    tm = min(MM_TM, t)
    return pl.pallas_call(
        _gates_kernel,
        grid=(t // tm,),
        in_specs=[pl.BlockSpec((tm, d), lambda i: (i, 0)),
                  pl.BlockSpec((N_GATES, d), lambda i: (0, 0)),
                  pl.BlockSpec((N_GATES, 1), lambda i: (0, 0))],
        out_specs=[pl.BlockSpec((3 * GDIR, tm), lambda i: (0, i)),
                   pl.BlockSpec((tm, LANES), lambda i: (i, 0))],
        out_shape=[jax.ShapeDtypeStruct((3 * GDIR, t), F32), jax.ShapeDtypeStruct((t, LANES), F32)],
        compiler_params=_cparams("parallel"),
        name="gates",
    )(xb, wgt, bg.reshape(N_GATES, 1))


def _gate_rows(a):
    h = M_HEADS
    return jnp.concatenate([a[0:h], a[2 * h:3 * h], a[h:2 * h], a[3 * h:4 * h]], axis=0)


def _halo_specs(tm, width, col_block, n_rows):
    per = tm // HALO
    last = n_rows // HALO - 1
    prev = pl.BlockSpec((HALO, width), lambda i, *_: (jnp.maximum(i * per - 1, 0), col_block))
    cur = pl.BlockSpec((tm, width), lambda i, *_: (i, col_block))
    nxt = pl.BlockSpec((HALO, width), lambda i, *_: (jnp.minimum((i + 1) * per, last), col_block))
    return prev, cur, nxt


def _extended_tile(prev_ref, cur_ref, next_ref, tm, seq):
    row0 = pl.program_id(0) * tm
    pos, slen = seq.pos_len(row0)
    prev = jnp.where(pos == 0, 0.0, prev_ref[...])
    nxt = jnp.where(pos + tm == slen, 0.0, next_ref[...])
    return jnp.concatenate([prev, cur_ref[...], nxt], axis=0), pos, slen


def _shift_rows(xe, d):
    n = xe.shape[0]
    if d == 0:
        return xe
    return pltpu.roll(xe, (n - d) % n, axis=0)


def _conv_kernel(prev_ref, cur_ref, next_ref, w_ref, b_ref, o_ref, *, tm, tc, seq, scale, transpose_out):
    xe, _, _ = _extended_tile(prev_ref, cur_ref, next_ref, tm, seq)
    acc = jnp.broadcast_to(b_ref[...], (tm, tc))
    for j in range(CONV_K):
        sh = _shift_rows(xe, j - CONV_K // 2)
        acc = acc + sh[HALO:HALO + tm] * w_ref[j:j + 1, :]
    y = acc * jax.nn.sigmoid(acc) * scale
    o_ref[...] = (y.T if transpose_out else y).astype(BF16)


def _conv_call(hin, conv_w, conv_b, seq, *, col0, scale, transpose_out):
    t = hin.shape[0]
    tm = min(SEQ_TM, seq.s1, seq.s2)
    tc = 512
    j0 = col0 // tc
    prev = pl.BlockSpec((HALO, tc), lambda i, j: (jnp.maximum(i * (tm // HALO) - 1, 0), j + j0))
    cur = pl.BlockSpec((tm, tc), lambda i, j: (i, j + j0))
    nxt = pl.BlockSpec((HALO, tc), lambda i, j: (jnp.minimum((i + 1) * (tm // HALO), t // HALO - 1), j + j0))
    if transpose_out:
        out_spec = pl.BlockSpec((tc, tm), lambda i, j: (j, i))
        out_shape = jax.ShapeDtypeStruct((M_WIDTH, t), BF16)
    else:
        out_spec = pl.BlockSpec((tm, tc), lambda i, j: (i, j))
        out_shape = jax.ShapeDtypeStruct((t, M_WIDTH), BF16)
    return pl.pallas_call(
        functools.partial(_conv_kernel, tm=tm, tc=tc, seq=seq, scale=scale, transpose_out=transpose_out),
        grid=(t // tm, M_WIDTH // tc),
        in_specs=[prev, cur, nxt,
                  pl.BlockSpec((CONV_K, tc), lambda i, j: (0, j + j0)),
                  pl.BlockSpec((1, tc), lambda i, j: (0, j + j0))],
        out_specs=out_spec,
        out_shape=out_shape,
        compiler_params=_cparams("parallel", "parallel"),
        name="conv_silu_t" if transpose_out else "conv_silu",
    )(hin, hin, hin, conv_w, conv_b.reshape(1, 2 * M_WIDTH))


def _mlstm_kernel(q_ref, kt_ref, v_ref, gr_ref, gc_ref, o_ref, c_ref, m_ref, *, reverse, cb, seq):
    L, DK, DV = CHUNK, M_HEAD_DIM, M_HEAD_DIM
    step = pl.program_id(0)
    nsteps = pl.num_programs(0)
    blk = (nsteps - 1 - step) if reverse else step
    ti = lax.broadcasted_iota(I32, (L, L), 0)
    si = lax.broadcasted_iota(I32, (L, L), 1)
    causal = (si >= ti) if reverse else (si <= ti)
    slot0 = M_HEADS if reverse else 0

    @pl.when(step == 0)
    def _():
        c_ref[...] = jnp.zeros_like(c_ref)
        m_ref[...] = jnp.zeros_like(m_ref)

    m_vals = [m_ref[h, 0:1, 0:1] for h in range(M_HEADS)]
    for jj in range(cb):
        j = (cb - 1 - jj) if reverse else jj
        rows = slice(j * L, (j + 1) * L)
        pos, slen = seq.pos_len((blk * cb + j) * L)
        at_start = (pos + L == slen) if reverse else (pos == 0)
        grow = gr_ref[j]
        ones = jnp.ones((L, LANES), F32)
        for h in range(M_HEADS):
            cols = slice(h * DK, (h + 1) * DK)
            sl = slot0 + h
            q = q_ref[rows, cols]
            kt = kt_ref[cols, rows]
            v32 = v_ref[rows, cols]
            ib_row = grow[sl:sl + 1, :]
            g_tot = grow[GDIR + sl:GDIR + sl + 1, 0:1]
            lw_max = grow[2 * GDIR + sl:2 * GDIR + sl + 1, 0:1]
            b_col = gc_ref[rows, sl:sl + 1]
            cmax_col = gc_ref[rows, GDIR + sl:GDIR + sl + 1]
            lw_col = gc_ref[rows, 2 * GDIR + sl:2 * GDIR + sl + 1]
            m_prev = jnp.where(at_start, 0.0, m_vals[h])
            m_rel = jnp.maximum(m_prev, cmax_col)
            m_t = b_col + m_rel
            dexp = jnp.exp(jnp.where(causal, ib_row - m_rel, -jnp.inf))
            s = jnp.dot(q, kt, preferred_element_type=F32)
            a = (s * dexp).astype(BF16)
            inter = jnp.exp(m_prev - m_rel)
            c_aug = jnp.where(at_start, 0.0, c_ref[h])
            qc = jnp.dot(q, c_aug.astype(BF16), preferred_element_type=F32)
            v_ext = jnp.concatenate([v32, ones], axis=1).astype(BF16)
            nd = jnp.dot(a, v_ext, preferred_element_type=F32) + inter * qc
            den = nd[:, DV:DV + 1]
            hh = nd[:, :DV] / jnp.maximum(jnp.abs(den), jnp.exp(-m_t))
            m_new = jnp.maximum(g_tot + m_prev, lw_max)
            wk_col = jnp.exp(lw_col - m_new)
            decay = jnp.exp(g_tot + m_prev - m_new)
            v_aug = (jnp.concatenate([v32, ones], axis=1) * wk_col).astype(BF16)
            upd = jnp.dot(kt, v_aug, preferred_element_type=F32)
            c_ref[h] = decay * c_aug + upd
            m_vals[h] = m_new
            o_ref[rows, cols] = hh
    for h in range(M_HEADS):
        m_ref[h] = jnp.broadcast_to(m_vals[h], (SUBLANES, LANES))


def _mlstm_call(q, kt, hin, gr3, gc, seq, *, reverse):
    t = q.shape[0]
    cb = MLSTM_CB
    while (seq.s1 // CHUNK) % cb or (seq.s2 // CHUNK) % cb:
        cb //= 2
    rows = cb * CHUNK
    nsteps = t // rows
    blk = (lambda s: nsteps - 1 - s) if reverse else (lambda s: s)
    vblk = OFF_V // M_WIDTH
    return pl.pallas_call(
        functools.partial(_mlstm_kernel, reverse=reverse, cb=cb, seq=seq),
        grid=(nsteps,),
        in_specs=[pl.BlockSpec((rows, M_WIDTH), lambda s: (blk(s), 0)),
                  pl.BlockSpec((M_WIDTH, rows), lambda s: (0, blk(s))),
                  pl.BlockSpec((rows, M_WIDTH), lambda s: (blk(s), vblk)),
                  pl.BlockSpec((cb, 3 * GDIR, CHUNK), lambda s: (blk(s), 0, 0)),
                  pl.BlockSpec((rows, LANES), lambda s: (blk(s), 0))],
        out_specs=pl.BlockSpec((rows, M_WIDTH), lambda s: (blk(s), 0)),
        out_shape=jax.ShapeDtypeStruct((t, M_WIDTH), F32),
        scratch_shapes=[pltpu.VMEM((M_HEADS, M_HEAD_DIM, M_HEAD_DIM + LANES), F32),
                        pltpu.VMEM((M_HEADS, SUBLANES, LANES), F32)],
        compiler_params=_cparams("arbitrary"),
        name="mlstm_bwd" if reverse else "mlstm_fwd",
    )(q, kt, hin, gr3, gc)


def _headnorm_kernel(hf_ref, hb_ref, uo_ref, ng_ref, o_ref):
    for h in range(M_HEADS):
        cols = slice(h * M_HEAD_DIM, (h + 1) * M_HEAD_DIM)
        hs = hf_ref[:, cols] + hb_ref[:, cols]
        mu = jnp.mean(hs, axis=1, keepdims=True)
        d = hs - mu
        var = jnp.mean(d * d, axis=1, keepdims=True)
        hn = d * lax.rsqrt(var + HN_EPS) * ng_ref[:, cols]
        o_ref[:, cols] = (hn * jax.nn.sigmoid(uo_ref[:, cols])).astype(o_ref.dtype)


def _headnorm_call(h_f, h_b, hin, norm_g):
    t = h_f.shape[0]
    tm = min(LN_TM, t)
    return pl.pallas_call(
        _headnorm_kernel,
        grid=(t // tm,),
        in_specs=[pl.BlockSpec((tm, M_WIDTH), lambda i: (i, 0)),
                  pl.BlockSpec((tm, M_WIDTH), lambda i: (i, 0)),
                  pl.BlockSpec((tm, M_WIDTH), lambda i: (i, OFF_O // M_WIDTH)),
                  pl.BlockSpec((1, M_WIDTH), lambda i: (0, 0))],
        out_specs=pl.BlockSpec((tm, M_WIDTH), lambda i: (i, 0)),
        out_shape=jax.ShapeDtypeStruct((t, M_WIDTH), BF16),
        compiler_params=_cparams("parallel"),
        name="headnorm",
    )(h_f, h_b, hin, norm_g.reshape(1, M_WIDTH))


def _pool_attn_kernel(prev_ref, cur_ref, next_ref, xq_ref, kv_ref, pw_ref, ps_ref, o_ref, *, tm, seq):
    xe, pos0, slen = _extended_tile(prev_ref, cur_ref, next_ref, tm, seq)
    pos = pos0 + lax.broadcasted_iota(I32, (tm, 1), 0)
    for gi, w in enumerate(POOL_WINDOWS):
        cols = slice(gi * POOL_GROUP, (gi + 1) * POOL_GROUP)
        u = xe[:, cols]
        win = _shift_rows(u, -1) + u
        half = 1
        while 2 * half < w:
            win = _shift_rows(win, -half) + _shift_rows(win, half)
            half *= 2
        cnt = (jnp.minimum(pos + w // 2, slen) - jnp.maximum(pos - w // 2, 0)).astype(F32)
        mean = win[HALO:HALO + tm] / cnt
        diff = (mean - u[HALO:HALO + tm]).astype(BF16)
        y = jnp.dot(diff, pw_ref[gi], preferred_element_type=F32)
        o_ref[:, cols] = (y * ps_ref[:, cols]).astype(o_ref.dtype)
    xq = xq_ref[...].astype(BF16)
    for h in range(X_HEADS):
        cols = slice(h * X_HEAD_DIM, (h + 1) * X_HEAD_DIM)
        kh = kv_ref[0, :, h * X_HEAD_DIM:(h + 1) * X_HEAD_DIM]
        vh = kv_ref[0, :, X_WIDTH + h * X_HEAD_DIM:X_WIDTH + (h + 1) * X_HEAD_DIM]
        s = lax.dot_general(xq[:, cols], kh, (((1,), (1,)), ((), ())),
                            preferred_element_type=F32) * (X_HEAD_DIM ** -0.5)
        e = jnp.exp(s - jnp.max(s, axis=1, keepdims=True))
        p = e / jnp.sum(e, axis=1, keepdims=True)
        y = jnp.dot(p.astype(BF16), vh, preferred_element_type=F32)
        o_ref[:, POOL_WIDTH + h * X_HEAD_DIM:POOL_WIDTH + (h + 1) * X_HEAD_DIM] = y.astype(o_ref.dtype)


def _pool_attn_call(hin, kv, pool_w, pool_scale, seq):
    t = hin.shape[0]
    tm = min(SEQ_TM, seq.s1, seq.s2)
    n_mem = kv.shape[1]
    prev, cur, nxt = _halo_specs(tm, POOL_WIDTH, OFF_P // POOL_WIDTH, t)
    return pl.pallas_call(
        functools.partial(_pool_attn_kernel, tm=tm, seq=seq),
        grid=(t // tm,),
        in_specs=[prev, cur, nxt,
                  pl.BlockSpec((tm, X_WIDTH), lambda i: (i, OFF_XQ // X_WIDTH)),
                  pl.BlockSpec((1, n_mem, 2 * X_WIDTH), lambda i: (seq.seq_id(i * tm), 0, 0)),
                  pl.BlockSpec((len(POOL_WINDOWS), POOL_GROUP, POOL_GROUP), lambda i: (0, 0, 0)),
                  pl.BlockSpec((1, POOL_WIDTH), lambda i: (0, 0))],
        out_specs=pl.BlockSpec((tm, POOL_WIDTH + X_WIDTH), lambda i: (i, 0)),
        out_shape=jax.ShapeDtypeStruct((t, POOL_WIDTH + X_WIDTH), BF16),
        compiler_params=_cparams("parallel"),
        name="pool_attn",
    )(hin, hin, hin, hin, kv, pool_w, pool_scale.reshape(1, POOL_WIDTH))


def _pack_halves(yb):
    n = yb.shape[1] // 2
    bits = pltpu.bitcast(yb.astype(F32), jnp.uint32)
    return jnp.bitwise_or(bits[:, n:], jnp.right_shift(bits[:, :n], jnp.uint32(16)))


def _unpack_halves(w):
    lo = pltpu.bitcast(jnp.left_shift(w, jnp.uint32(16)), F32)
    hi = pltpu.bitcast(jnp.bitwise_and(w, jnp.uint32(0xFFFF0000)), F32)
    return jnp.concatenate([lo, hi], axis=1).astype(BF16)


def _outproj_kernel(m_ref, pa_ref, x_ref, w_ref, g_ref, b_ref, o_ref, ob_ref, op_ref):
    acc = jnp.dot(m_ref[...], w_ref[0:M_WIDTH, :], preferred_element_type=F32)
    acc = acc + jnp.dot(pa_ref[...], w_ref[M_WIDTH:, :], preferred_element_type=F32)
    y = _layer_norm(ALPHA * x_ref[...] + acc, g_ref[...], b_ref[...], LN_EPS)
    o_ref[...] = y
    yb = y.astype(BF16)
    ob_ref[...] = yb
    op_ref[...] = _pack_halves(yb)


def _outproj_call(m_out, pa_out, x, w_out, g, b):
    t, d = x.shape
    tm = min(OUT_TM, t)
    return pl.pallas_call(
        _outproj_kernel,
        grid=(t // tm,),
        in_specs=[pl.BlockSpec((tm, M_WIDTH), lambda i: (i, 0)),
                  pl.BlockSpec((tm, POOL_WIDTH + X_WIDTH), lambda i: (i, 0)),
                  pl.BlockSpec((tm, d), lambda i: (i, 0)),
                  pl.BlockSpec((d, d), lambda i: (0, 0)),
                  pl.BlockSpec((1, d), lambda i: (0, 0)),
                  pl.BlockSpec((1, d), lambda i: (0, 0))],
        out_specs=[pl.BlockSpec((tm, d), lambda i: (i, 0)),
                   pl.BlockSpec((tm, d), lambda i: (i, 0)),
                   pl.BlockSpec((tm, d // 2), lambda i: (i, 0))],
        out_shape=[jax.ShapeDtypeStruct((t, d), F32), jax.ShapeDtypeStruct((t, d), BF16),
                   jax.ShapeDtypeStruct((t, d // 2), jnp.uint32)],
        compiler_params=_cparams("parallel"),
        name="outproj_ln",
    )(m_out, pa_out, x, w_out, g.reshape(1, d), b.reshape(1, d))


def _first_index_of_max(vals, idx, axis_size):
    mx = jnp.max(vals, axis=0, keepdims=True)
    first = jnp.min(jnp.where(vals == mx, idx, axis_size), axis=0, keepdims=True)
    return mx, first


def _router_kernel(x_ref, w_ref, b_ref, ids_ref, gate_ref, rank_ref, cnt_ref, carry_ref, *, tm):
    @pl.when(pl.program_id(0) == 0)
    def _():
        carry_ref[...] = jnp.zeros_like(carry_ref)

    logits = lax.dot_general(w_ref[...], x_ref[...], (((1,), (1,)), ((), ())),
                             preferred_element_type=F32)
    s = jax.nn.sigmoid(logits)
    sel = s + b_ref[...]
    neg = -jnp.inf
    sub = lax.broadcasted_iota(I32, (GROUP_SIZE, tm), 0)
    gscores = []
    for g in range(N_GROUPS):
        blk = sel[g * GROUP_SIZE:(g + 1) * GROUP_SIZE, :]
        m1, i1 = _first_index_of_max(blk, sub, GROUP_SIZE)
        m2 = jnp.max(jnp.where(sub == i1, neg, blk), axis=0, keepdims=True)
        gscores.append(m1 + m2)
    gs = jnp.concatenate(gscores, axis=0)
    gidx = lax.broadcasted_iota(I32, (N_GROUPS, tm), 0)
    gmask = jnp.zeros((N_GROUPS, tm), jnp.bool_)
    for _ in range(TOPK_GROUPS):
        _, gi = _first_index_of_max(gs, gidx, N_GROUPS)
        hit = gidx == gi
        gmask = jnp.logical_or(gmask, hit)
        gs = jnp.where(hit, neg, gs)
    gmask_f = gmask.astype(F32)
    masked = jnp.concatenate(
        [jnp.where(gmask_f[g:g + 1, :] > 0.5, sel[g * GROUP_SIZE:(g + 1) * GROUP_SIZE, :], neg)
         for g in range(N_GROUPS)], axis=0)
    eidx = lax.broadcasted_iota(I32, (N_EXPERTS, tm), 0)
    chosen = jnp.zeros((N_EXPERTS, tm), F32)
    ids, gates, hits = [], [], []
    for _ in range(TOP_K):
        _, ei = _first_index_of_max(masked, eidx, N_EXPERTS)
        hit = eidx == ei
        ids.append(ei)
        gates.append(jnp.sum(jnp.where(hit, s, 0.0), axis=0, keepdims=True))
        hits.append(hit)
        chosen = jnp.where(hit, 1.0, chosen)
        masked = jnp.where(hit, neg, masked)
    gate = jnp.concatenate(gates, axis=0)
    gate = gate / jnp.sum(gate, axis=0, keepdims=True) * ROUTED_SCALE
    ids_ref[...] = jnp.concatenate(ids, axis=0)
    gate_ref[...] = gate
    r = lax.broadcasted_iota(I32, (tm, tm), 0)
    c = lax.broadcasted_iota(I32, (tm, tm), 1)
    before = (r < c).astype(BF16)
    cnt = jnp.dot(chosen.astype(BF16), before, preferred_element_type=F32) + carry_ref[:, 0:1]
    ranks = [jnp.sum(jnp.where(hit, cnt, 0.0), axis=0, keepdims=True) for hit in hits]
    rank_ref[...] = jnp.concatenate(ranks, axis=0).astype(I32)
    total = carry_ref[:, 0:1] + jnp.sum(chosen, axis=1, keepdims=True)
    carry_ref[...] = jnp.broadcast_to(total, carry_ref.shape)
    cnt_ref[...] = jnp.broadcast_to(total, cnt_ref.shape)


def _router_call(xb, wrt, br):
    t, d = xb.shape
    tm = min(ROUTE_TM, t)
    return pl.pallas_call(
        functools.partial(_router_kernel, tm=tm),
        grid=(t // tm,),
        in_specs=[pl.BlockSpec((tm, d), lambda i: (i, 0)),
                  pl.BlockSpec((N_EXPERTS, d), lambda i: (0, 0)),
                  pl.BlockSpec((N_EXPERTS, 1), lambda i: (0, 0))],
        out_specs=[pl.BlockSpec((TOP_K, tm), lambda i: (0, i)),
                   pl.BlockSpec((TOP_K, tm), lambda i: (0, i)),
                   pl.BlockSpec((TOP_K, tm), lambda i: (0, i)),
                   pl.BlockSpec((N_EXPERTS, LANES), lambda i: (0, 0))],
        out_shape=[jax.ShapeDtypeStruct((TOP_K, t), I32),
                   jax.ShapeDtypeStruct((TOP_K, t), F32),
                   jax.ShapeDtypeStruct((TOP_K, t), I32),
                   jax.ShapeDtypeStruct((N_EXPERTS, LANES), F32)],
        scratch_shapes=[pltpu.VMEM((N_EXPERTS, LANES), F32)],
        compiler_params=_cparams("arbitrary"),
        name="router",
    )(xb, wrt, br.reshape(N_EXPERTS, 1))


def _dispatch_kernel(pad_lo_ref, pad_hi_ref, dest_ref, x_ref, xs_hbm, zrow, sem, *, tm):
    i = pl.program_id(0)

    @pl.when(i == 0)
    def _():
        zrow[...] = jnp.zeros_like(zrow)

        def per_expert(e, carry):
            lo, hi = pad_lo_ref[e], pad_hi_ref[e]

            def start(p, c):
                pltpu.make_async_copy(zrow, xs_hbm.at[pl.ds(p, 1)], sem.at[1]).start()
                return c

            def wait(p, c):
                pltpu.make_async_copy(zrow, xs_hbm.at[pl.ds(0, 1)], sem.at[1]).wait()
                return c

            lax.fori_loop(lo, hi, start, 0)
            lax.fori_loop(lo, hi, wait, 0)
            return carry

        lax.fori_loop(0, N_EXPERTS, per_expert, 0)

    for r in range(tm):
        for k in range(TOP_K):
            pltpu.make_async_copy(x_ref.at[pl.ds(r, 1)], xs_hbm.at[pl.ds(dest_ref[0, k, r], 1)],
                                  sem.at[0]).start(priority=k % 2)
    for k in range(TOP_K):
        pltpu.make_async_copy(x_ref, xs_hbm.at[pl.ds(0, tm)], sem.at[0]).wait()


def _dispatch_call(pad_lo, pad_hi, dest3, xpk, n_rows):
    t, w = xpk.shape
    tm = dest3.shape[2]
    grid_spec = pltpu.PrefetchScalarGridSpec(
        num_scalar_prefetch=2,
        grid=(t // tm,),
        in_specs=[pl.BlockSpec((1, TOP_K, tm), lambda i, lo, hi: (i, 0, 0), memory_space=pltpu.SMEM),
                  pl.BlockSpec((tm, w), lambda i, lo, hi: (i, 0))],
        out_specs=pl.BlockSpec(memory_space=pl.ANY),
        scratch_shapes=[pltpu.VMEM((1, w), xpk.dtype), pltpu.SemaphoreType.DMA((2,))],
    )
    return pl.pallas_call(
        functools.partial(_dispatch_kernel, tm=tm),
        grid_spec=grid_spec,
        out_shape=jax.ShapeDtypeStruct((n_rows, w), xpk.dtype),
        compiler_params=_cparams("arbitrary"),
        name="dispatch",
    )(pad_lo, pad_hi, dest3, xpk)


def _expert_kernel(blk_e_ref, nused_ref, x_ref, wgu_ref, wdn_ref, o_ref, wgu_b, wdn_b):
    b = pl.program_id(0)

    @pl.when(jnp.logical_or(b == 0, blk_e_ref[b] != blk_e_ref[jnp.maximum(b - 1, 0)]))
    def _():
        wgu_b[...] = wgu_ref[0, 0].astype(BF16)
        wdn_b[...] = wdn_ref[0, 0].astype(BF16)

    @pl.when(b < nused_ref[0])
    def _():
        gu = jnp.dot(_unpack_halves(x_ref[...]), wgu_b[...], preferred_element_type=F32)
        g = gu[:, :D_EXPERT]
        u = gu[:, D_EXPERT:]
        hmid = (g * jax.nn.sigmoid(g) * u).astype(BF16)
        o_ref[...] = jnp.dot(hmid, wdn_b[...], preferred_element_type=F32)

    @pl.when(b >= nused_ref[0])
    def _():
        o_ref[...] = jnp.zeros_like(o_ref)


def _expert_call(blk_e, n_used, xs, wgu, wdn, bm, layer):
    n_blocks = xs.shape[0] // bm
    d = wdn.shape[-1]

    def x_block(b, be, nu):
        return (jnp.minimum(b, jnp.maximum(nu[0] - 1, 0)), 0)

    grid_spec = pltpu.PrefetchScalarGridSpec(
        num_scalar_prefetch=2,
        grid=(n_blocks,),
        in_specs=[pl.BlockSpec((bm, d // 2), x_block),
                  pl.BlockSpec((1, 1, d, 2 * D_EXPERT), lambda b, be, nu: (layer, be[b], 0, 0)),
                  pl.BlockSpec((1, 1, D_EXPERT, d), lambda b, be, nu: (layer, be[b], 0, 0))],
        out_specs=pl.BlockSpec((bm, d), lambda b, be, nu: (b, 0)),
        scratch_shapes=[pltpu.VMEM((d, 2 * D_EXPERT), BF16), pltpu.VMEM((D_EXPERT, d), BF16)],
    )
    return pl.pallas_call(
        _expert_kernel,
        grid_spec=grid_spec,
        out_shape=jax.ShapeDtypeStruct((n_blocks * bm, d), F32),
        compiler_params=_cparams("arbitrary"),
        name="experts",
    )(blk_e, n_used, xs, wgu, wdn)


def _combine_kernel(dest_ref, dest_next_ref, gate_ref, x_ref, xb_ref, ys_hbm, wsg_ref, wsd_ref, g_ref, b_ref,
                    *rest, tm, n1, nsteps):
    if n1 is None:
        o_ref, ob_ref, buf0, buf1, base_ref, sem = rest
    else:
        o1_ref, o2_ref, buf0, buf1, base_ref, sem = rest
    i = pl.program_id(0)
    bufs = (buf0, buf1)

    def issue(ref, s):
        for r in range(tm):
            for k in range(TOP_K):
                pltpu.make_async_copy(ys_hbm.at[pl.ds(ref[0, k, r], 1)],
                                      bufs[s].at[pl.ds(k * tm + r, 1)], sem.at[s]).start(priority=k % 2)

    def wait(s):
        pltpu.make_async_copy(ys_hbm.at[pl.ds(0, TOP_K * tm)], bufs[s], sem.at[s]).wait()

    @pl.when(i == 0)
    def _():
        issue(dest_ref, 0)

    gu = jnp.dot(xb_ref[...], wsg_ref[...], preferred_element_type=F32)
    g = gu[:, :D_SHARED]
    u = gu[:, D_SHARED:]
    hmid = (g * jax.nn.sigmoid(g) * u).astype(BF16)
    base_ref[...] = ALPHA * x_ref[...] + jnp.dot(hmid, wsd_ref[...], preferred_element_type=F32)

    def step(cur):
        wait(cur)
        issue(dest_next_ref, 1 - cur)
        acc = gate_ref[:, 0:1] * bufs[cur][0:tm, :]
        for k in range(1, TOP_K):
            acc = acc + gate_ref[:, k:k + 1] * bufs[cur][k * tm:(k + 1) * tm, :]
        y = _layer_norm(base_ref[...] + acc, g_ref[...], b_ref[...], LN_EPS)
        if n1 is None:
            o_ref[...] = y
            ob_ref[...] = y.astype(BF16)
        else:
            @pl.when(i < n1)
            def _():
                o1_ref[...] = y

            @pl.when(i >= n1)
            def _():
                o2_ref[...] = y

    for cur in range(2):
        pl.when(lax.rem(i, 2) == cur)(functools.partial(step, cur))

    @pl.when(i == nsteps - 1)
    def _():
        wait(nsteps % 2)


def _combine_call(dest3, gate_tk, x, xb, ys, wsg, wsd, g, b, split_rows=None):
    t, d = x.shape
    tm = dest3.shape[2]
    nsteps = t // tm
    if split_rows is None:
        n1 = None
        out_specs = [pl.BlockSpec((tm, d), lambda i: (i, 0)), pl.BlockSpec((tm, d), lambda i: (i, 0))]
        out_shape = [jax.ShapeDtypeStruct((t, d), F32), jax.ShapeDtypeStruct((t, d), BF16)]
    else:
        n1 = split_rows // tm
        out_specs = [pl.BlockSpec((tm, d), lambda i: (jnp.minimum(i, n1 - 1), 0)),
                     pl.BlockSpec((tm, d), lambda i: (jnp.maximum(i - n1, 0), 0))]
        out_shape = [jax.ShapeDtypeStruct((split_rows, d), F32), jax.ShapeDtypeStruct((t - split_rows, d), F32)]
    return pl.pallas_call(
        functools.partial(_combine_kernel, tm=tm, n1=n1, nsteps=nsteps),
        grid=(nsteps,),
        in_specs=[pl.BlockSpec((1, TOP_K, tm), lambda i: (i, 0, 0), memory_space=pltpu.SMEM),
                  pl.BlockSpec((1, TOP_K, tm), lambda i: (jnp.minimum(i + 1, nsteps - 1), 0, 0),
                               memory_space=pltpu.SMEM),
                  pl.BlockSpec((tm, TOP_K), lambda i: (i, 0)),
                  pl.BlockSpec((tm, d), lambda i: (i, 0)),
                  pl.BlockSpec((tm, d), lambda i: (i, 0)),
                  pl.BlockSpec(memory_space=pl.ANY),
                  pl.BlockSpec((d, 2 * D_SHARED), lambda i: (0, 0)),
                  pl.BlockSpec((D_SHARED, d), lambda i: (0, 0)),
                  pl.BlockSpec((1, d), lambda i: (0, 0)),
                  pl.BlockSpec((1, d), lambda i: (0, 0))],
        out_specs=out_specs,
        out_shape=out_shape,
        scratch_shapes=[pltpu.VMEM((TOP_K * tm, d), F32), pltpu.VMEM((TOP_K * tm, d), F32),
                        pltpu.VMEM((tm, d), F32), pltpu.SemaphoreType.DMA((2,))],
        compiler_params=_cparams("arbitrary"),
        name="combine_ln",
    )(dest3, dest3, gate_tk, x, xb, ys, wsg, wsd, g.reshape(1, d), b.reshape(1, d))


def _dispatch_tables(ids, rank, counts_f, bm):
    k, t = ids.shape
    tk = k * t
    n_blocks = -(-tk // bm) + N_EXPERTS
    counts = counts_f[:, 0].astype(I32)
    padded = (counts + bm - 1) // bm * bm
    pends = jnp.cumsum(padded)
    pstart = pends - padded
    experts = jnp.arange(N_EXPERTS, dtype=I32)
    dest = jnp.sum(jnp.where(ids[:, :, None] == experts, pstart, 0), axis=-1) + rank
    blk_first = jnp.arange(n_blocks, dtype=I32) * bm
    blk_e = jnp.minimum(jnp.sum(pends[None, :] <= blk_first[:, None], axis=1), N_EXPERTS - 1).astype(I32)
    n_used = (pends[-1:] // bm).astype(I32)
    return dest, blk_e, n_used, (pstart + counts).astype(I32), pends.astype(I32), n_blocks * bm


def _layer(x, xb, memb, seq, layer, w_in, b_gate, conv_w, conv_b, mh_norm_g, pool_w, pool_scale, w_mem_kv, w_out,
           ln1_g, ln1_b, w_router, b_router, w_gu_all, w_dn_all, w_sh_gu, w_sh_dn, ln2_g, ln2_b, last):
    t = x.shape[0]
    hin = _mm_call(xb, w_in[:, :OFF_G].astype(BF16), F32, "in_proj")
    g_rows, g_cols = _gates_call(xb, _gate_rows(w_in[:, OFF_G:].T).astype(BF16), _gate_rows(b_gate))
    gr3 = g_rows.reshape(3 * GDIR, t // CHUNK, CHUNK).transpose(1, 0, 2)
    q = _conv_call(hin, conv_w, conv_b, seq, col0=OFF_Q, scale=1.0, transpose_out=False)
    kt = _conv_call(hin, conv_w, conv_b, seq, col0=OFF_K, scale=M_HEAD_DIM ** -0.5, transpose_out=True)
    h_f = _mlstm_call(q, kt, hin, gr3, g_cols, seq, reverse=False)
    h_b = _mlstm_call(q, kt, hin, gr3, g_cols, seq, reverse=True)
    m_out = _headnorm_call(h_f, h_b, hin, mh_norm_g)
    kv = _mm_call(memb, w_mem_kv.astype(BF16), BF16, "mem_kv").reshape(seq.nseq, -1, 2 * X_WIDTH)
    pa_out = _pool_attn_call(hin, kv, pool_w.astype(BF16), pool_scale, seq)
    x1, x1b, x1p = _outproj_call(m_out, pa_out, x, w_out.astype(BF16), ln1_g, ln1_b)
    ids, gate, rank, counts = _router_call(x1b, w_router.T.astype(BF16), b_router)
    bm = MOE_BM
    dest, blk_e, n_used, pad_lo, pad_hi, n_rows = _dispatch_tables(ids, rank, counts, bm)
    ctm = min(COMB_TM, t)
    dest3 = dest.reshape(TOP_K, t // ctm, ctm).transpose(1, 0, 2)
    xs = _dispatch_call(pad_lo, pad_hi, dest3, x1p, n_rows)
    ys = _expert_call(blk_e, n_used, xs, w_gu_all, w_dn_all, bm, layer)
    return _combine_call(dest3, gate.T, x1, x1b, ys, w_sh_gu.astype(BF16), w_sh_dn.astype(BF16), ln2_g, ln2_b,
                         split_rows=seq.t1 if last else None)


def kernel(x_prompt, x_sample, mem_prompt, mem_sample, ln_in_g, ln_in_b, w_in, b_gate, conv_w, conv_b, mh_norm_g, pool_w, pool_scale, w_mem_kv, w_out, ln1_g, ln1_b, w_router, b_router, w_gu, w_dn, w_sh_gu, w_sh_dn, ln2_g, ln2_b):
    b1, s1, d = x_prompt.shape
    b2, s2, _ = x_sample.shape
    seq = SeqInfo(b1, s1, b2, s2)
    memb = jnp.concatenate([mem_prompt.reshape(-1, d), mem_sample.reshape(-1, d)], axis=0).astype(BF16)
    x, xb = _ln_call(x_prompt.reshape(b1 * s1, d), x_sample.reshape(b2 * s2, d), ln_in_g, ln_in_b)
    depth = w_in.shape[0]
    for l in range(depth):
        x, xb = _layer(x, xb, memb, seq, l, w_in[l], b_gate[l], conv_w[l], conv_b[l], mh_norm_g[l], pool_w[l],
                       pool_scale[l], w_mem_kv[l], w_out[l], ln1_g[l], ln1_b[l], w_router[l], b_router[l],
                       w_gu, w_dn, w_sh_gu[l], w_sh_dn[l], ln2_g[l], ln2_b[l], last=l == depth - 1)
    return (x.reshape(b1, s1, d), xb.reshape(b2, s2, d))
```

```python
import functools

import jax
import jax.numpy as jnp
from jax import lax
from jax.experimental import pallas as pl
from jax.experimental.pallas import tpu as pltpu

F32 = jnp.float32
BF16 = jnp.bfloat16
I32 = jnp.int32

DEPTH = 4
D_MODEL = 2048
M_HEADS = 4
M_WIDTH = 1024
M_HEAD_DIM = 256
CHUNK = 128
CONV_K = 5
POOL_WINDOWS = (2, 4, 8, 16)
POOL_WIDTH = 512
POOL_GROUP = 128
X_HEADS = 4
X_WIDTH = 512
X_HEAD_DIM = 128
OFF_Q = 0
OFF_K = OFF_Q + M_WIDTH
OFF_V = OFF_K + M_WIDTH
OFF_O = OFF_V + M_WIDTH
OFF_P = OFF_O + M_WIDTH
OFF_XQ = OFF_P + POOL_WIDTH
OFF_G = OFF_XQ + X_WIDTH
N_GATES = 4 * M_HEADS
N_EXPERTS = 64
TOP_K = 8
N_GROUPS = 8
GROUP_SIZE = N_EXPERTS // N_GROUPS
TOPK_GROUPS = 4
D_EXPERT = 512
D_SHARED = 512
ROUTED_SCALE = 2.5
ALPHA = (2 * DEPTH) ** 0.25
LN_EPS = 1e-5
HN_EPS = 1e-6

LANES = 128
SUBLANES = 8
HALO = SUBLANES
VMEM_LIMIT = 56 * 1024 * 1024

LN_TM = 512
MM_TM = 512
MM_TN = 1024
SEQ_TM = 512
MLSTM_CB = 4
OUT_TM = 256
ROUTE_TM = 512
MOE_BM = 512
COMB_TM = 128


def _cparams(*sem):
    return pltpu.CompilerParams(dimension_semantics=sem, vmem_limit_bytes=VMEM_LIMIT)


class SeqInfo:
    def __init__(self, b1, s1, b2, s2):
        self.b1, self.s1, self.b2, self.s2 = b1, s1, b2, s2
        self.t1 = b1 * s1
        self.t = b1 * s1 + b2 * s2
        self.nseq = b1 + b2

    def pos_len(self, row0):
        in1 = row0 < self.t1
        pos = jnp.where(in1, lax.rem(row0, self.s1), lax.rem(row0 - self.t1, self.s2))
        slen = jnp.where(in1, self.s1, self.s2)
        return pos, slen

    def seq_id(self, row0):
        in1 = row0 < self.t1
        return jnp.where(in1, row0 // self.s1, self.b1 + (row0 - self.t1) // self.s2)


def _layer_norm(y, g, b, eps):
    mu = jnp.mean(y, axis=-1, keepdims=True)
    d = y - mu
    var = jnp.mean(d * d, axis=-1, keepdims=True)
    return d * lax.rsqrt(var + eps) * g + b


def _ln_kernel(x1_ref, x2_ref, g_ref, b_ref, o_ref, ob_ref, *, n1):
    x = jnp.where(pl.program_id(0) < n1, x1_ref[...], x2_ref[...])
    y = _layer_norm(x, g_ref[...], b_ref[...], LN_EPS)
    o_ref[...] = y
    ob_ref[...] = y.astype(BF16)


def _ln_call(x1, x2, g, b):
    t1, d = x1.shape
    t2 = x2.shape[0]
    tm = min(LN_TM, t1, t2)
    n1, n2 = t1 // tm, t2 // tm
    t = t1 + t2
    return pl.pallas_call(
        functools.partial(_ln_kernel, n1=n1),
        grid=(n1 + n2,),
        in_specs=[pl.BlockSpec((tm, d), lambda i: (jnp.minimum(i, n1 - 1), 0)),
                  pl.BlockSpec((tm, d), lambda i: (jnp.maximum(i - n1, 0), 0)),
                  pl.BlockSpec((1, d), lambda i: (0, 0)),
                  pl.BlockSpec((1, d), lambda i: (0, 0))],
        out_specs=[pl.BlockSpec((tm, d), lambda i: (i, 0)),
                   pl.BlockSpec((tm, d), lambda i: (i, 0))],
        out_shape=[jax.ShapeDtypeStruct((t, d), F32), jax.ShapeDtypeStruct((t, d), BF16)],
        compiler_params=_cparams("arbitrary"),
        name="ln_in",
    )(x1, x2, g.reshape(1, d), b.reshape(1, d))


def _mm_kernel(x_ref, w_ref, o_ref):
    o_ref[...] = jnp.dot(x_ref[...], w_ref[...], preferred_element_type=F32).astype(o_ref.dtype)


def _mm_call(xb, wb, out_dtype, name):
    m, k = xb.shape
    n = wb.shape[1]
    tm = min(MM_TM, m)
    while m % tm:
        tm //= 2
    tn = min(MM_TN, n)
    return pl.pallas_call(
        _mm_kernel,
        grid=(n // tn, m // tm),
        in_specs=[pl.BlockSpec((tm, k), lambda j, i: (i, 0)),
                  pl.BlockSpec((k, tn), lambda j, i: (0, j))],
        out_specs=pl.BlockSpec((tm, tn), lambda j, i: (i, j)),
        out_shape=jax.ShapeDtypeStruct((m, n), out_dtype),
        compiler_params=_cparams("parallel", "parallel"),
        name=name,
    )(xb, wb)


GDIR = 2 * M_HEADS


def _chunk_scan(x, op, reverse):
    n = x.shape[1]
    pos = lax.rem(lax.broadcasted_iota(I32, x.shape, 1), CHUNK)
    s = 1
    while s < CHUNK:
        if reverse:
            x = jnp.where(pos < CHUNK - s, op(x, pltpu.roll(x, n - s, axis=1)), x)
        else:
            x = jnp.where(pos >= s, op(x, pltpu.roll(x, s, axis=1)), x)
        s *= 2
    return x


def _gates_kernel(x_ref, w_ref, b_ref, row_ref, col_ref):
    g = lax.dot_general(w_ref[...], x_ref[...], (((1,), (1,)), ((), ())),
                        preferred_element_type=F32) + b_ref[...]
    gi = g[0:GDIR]
    logf = jax.nn.log_sigmoid(g[GDIR:2 * GDIR])
    is_fwd = lax.broadcasted_iota(I32, gi.shape, 0) < M_HEADS
    pre = _chunk_scan(logf, jnp.add, False)
    suf = _chunk_scan(logf, jnp.add, True)
    b = jnp.where(is_fwd, pre, suf)
    total = pre + suf - logf
    ib = gi - b
    cmax = jnp.where(is_fwd, _chunk_scan(ib, jnp.maximum, False), _chunk_scan(ib, jnp.maximum, True))
    lw = total + ib
    lw_max = jnp.maximum(_chunk_scan(lw, jnp.maximum, False), _chunk_scan(lw, jnp.maximum, True))
    row_ref[...] = jnp.concatenate([ib, total, lw_max], axis=0)
    pad = jnp.zeros((LANES - 3 * GDIR, gi.shape[1]), F32)
    col_ref[...] = jnp.concatenate([b, cmax, lw, pad], axis=0).T


def _gates_call(xb, wgt, bg):
    t, d = xb.shape
    tm = min(MM_TM, t)
    return pl.pallas_call(
        _gates_kernel,
        grid=(t // tm,),
        in_specs=[pl.BlockSpec((tm, d), lambda i: (i, 0)),
                  pl.BlockSpec((N_GATES, d), lambda i: (0, 0)),
                  pl.BlockSpec((N_GATES, 1), lambda i: (0, 0))],
        out_specs=[pl.BlockSpec((3 * GDIR, tm), lambda i: (0, i)),
                   pl.BlockSpec((tm, LANES), lambda i: (i, 0))],
        out_shape=[jax.ShapeDtypeStruct((3 * GDIR, t), F32), jax.ShapeDtypeStruct((t, LANES), F32)],
        compiler_params=_cparams("parallel"),
        name="gates",
    )(xb, wgt, bg.reshape(N_GATES, 1))


def _gate_rows(a):
    h = M_HEADS
    return jnp.concatenate([a[0:h], a[2 * h:3 * h], a[h:2 * h], a[3 * h:4 * h]], axis=0)


def _halo_specs(tm, width, col_block, n_rows):
    per = tm // HALO
    last = n_rows // HALO - 1
    prev = pl.BlockSpec((HALO, width), lambda i, *_: (jnp.maximum(i * per - 1, 0), col_block))
    cur = pl.BlockSpec((tm, width), lambda i, *_: (i, col_block))
    nxt = pl.BlockSpec((HALO, width), lambda i, *_: (jnp.minimum((i + 1) * per, last), col_block))
    return prev, cur, nxt


def _extended_tile(prev_ref, cur_ref, next_ref, tm, seq):
    row0 = pl.program_id(0) * tm
    pos, slen = seq.pos_len(row0)
    prev = jnp.where(pos == 0, 0.0, prev_ref[...])
    nxt = jnp.where(pos + tm == slen, 0.0, next_ref[...])
    return jnp.concatenate([prev, cur_ref[...], nxt], axis=0), pos, slen


def _shift_rows(xe, d):
    n = xe.shape[0]
    if d == 0:
        return xe
    return pltpu.roll(xe, (n - d) % n, axis=0)


def _conv_kernel(prev_ref, cur_ref, next_ref, w_ref, b_ref, o_ref, *, tm, tc, seq, scale, transpose_out):
    xe, _, _ = _extended_tile(prev_ref, cur_ref, next_ref, tm, seq)
    acc = jnp.broadcast_to(b_ref[...], (tm, tc))
    for j in range(CONV_K):
        sh = _shift_rows(xe, j - CONV_K // 2)
        acc = acc + sh[HALO:HALO + tm] * w_ref[j:j + 1, :]
    y = acc * jax.nn.sigmoid(acc) * scale
    o_ref[...] = (y.T if transpose_out else y).astype(BF16)


def _conv_call(hin, conv_w, conv_b, seq, *, col0, scale, transpose_out):
    t = hin.shape[0]
    tm = min(SEQ_TM, seq.s1, seq.s2)
    tc = 512
    j0 = col0 // tc
    prev = pl.BlockSpec((HALO, tc), lambda i, j: (jnp.maximum(i * (tm // HALO) - 1, 0), j + j0))
    cur = pl.BlockSpec((tm, tc), lambda i, j: (i, j + j0))
    nxt = pl.BlockSpec((HALO, tc), lambda i, j: (jnp.minimum((i + 1) * (tm // HALO), t // HALO - 1), j + j0))
    if transpose_out:
        out_spec = pl.BlockSpec((tc, tm), lambda i, j: (j, i))
        out_shape = jax.ShapeDtypeStruct((M_WIDTH, t), BF16)
    else:
        out_spec = pl.BlockSpec((tm, tc), lambda i, j: (i, j))
        out_shape = jax.ShapeDtypeStruct((t, M_WIDTH), BF16)
    return pl.pallas_call(
        functools.partial(_conv_kernel, tm=tm, tc=tc, seq=seq, scale=scale, transpose_out=transpose_out),
        grid=(t // tm, M_WIDTH // tc),
        in_specs=[prev, cur, nxt,
                  pl.BlockSpec((CONV_K, tc), lambda i, j: (0, j + j0)),
                  pl.BlockSpec((1, tc), lambda i, j: (0, j + j0))],
        out_specs=out_spec,
        out_shape=out_shape,
        compiler_params=_cparams("parallel", "parallel"),
        name="conv_silu_t" if transpose_out else "conv_silu",
    )(hin, hin, hin, conv_w, conv_b.reshape(1, 2 * M_WIDTH))


def _mlstm_kernel(q_ref, kt_ref, v_ref, gr_ref, gc_ref, o_ref, c_ref, m_ref, *, reverse, cb, seq):
    L, DK, DV = CHUNK, M_HEAD_DIM, M_HEAD_DIM
    step = pl.program_id(0)
    nsteps = pl.num_programs(0)
    blk = (nsteps - 1 - step) if reverse else step
    ti = lax.broadcasted_iota(I32, (L, L), 0)
    si = lax.broadcasted_iota(I32, (L, L), 1)
    causal = (si >= ti) if reverse else (si <= ti)
    slot0 = M_HEADS if reverse else 0

    @pl.when(step == 0)
    def _():
        c_ref[...] = jnp.zeros_like(c_ref)
        m_ref[...] = jnp.zeros_like(m_ref)

    m_vals = [m_ref[h, 0:1, 0:1] for h in range(M_HEADS)]
    for jj in range(cb):
        j = (cb - 1 - jj) if reverse else jj
        rows = slice(j * L, (j + 1) * L)
        pos, slen = seq.pos_len((blk * cb + j) * L)
        at_start = (pos + L == slen) if reverse else (pos == 0)
        grow = gr_ref[j]
        ones = jnp.ones((L, LANES), F32)
        for h in range(M_HEADS):
            cols = slice(h * DK, (h + 1) * DK)
            sl = slot0 + h
            q = q_ref[rows, cols]
            kt = kt_ref[cols, rows]
            v32 = v_ref[rows, cols]
            ib_row = grow[sl:sl + 1, :]
            g_tot = grow[GDIR + sl:GDIR + sl + 1, 0:1]
            lw_max = grow[2 * GDIR + sl:2 * GDIR + sl + 1, 0:1]
            b_col = gc_ref[rows, sl:sl + 1]
            cmax_col = gc_ref[rows, GDIR + sl:GDIR + sl + 1]
            lw_col = gc_ref[rows, 2 * GDIR + sl:2 * GDIR + sl + 1]
            m_prev = jnp.where(at_start, 0.0, m_vals[h])
            m_rel = jnp.maximum(m_prev, cmax_col)
            m_t = b_col + m_rel
            dexp = jnp.exp(jnp.where(causal, ib_row - m_rel, -jnp.inf))
            s = jnp.dot(q, kt, preferred_element_type=F32)
            a = (s * dexp).astype(BF16)
            inter = jnp.exp(m_prev - m_rel)
            c_aug = jnp.where(at_start, 0.0, c_ref[h])
            qc = jnp.dot(q, c_aug.astype(BF16), preferred_element_type=F32)
            v_ext = jnp.concatenate([v32, ones], axis=1).astype(BF16)
            nd = jnp.dot(a, v_ext, preferred_element_type=F32) + inter * qc
            den = nd[:, DV:DV + 1]
            hh = nd[:, :DV] / jnp.maximum(jnp.abs(den), jnp.exp(-m_t))
            m_new = jnp.maximum(g_tot + m_prev, lw_max)
            wk_col = jnp.exp(lw_col - m_new)
            decay = jnp.exp(g_tot + m_prev - m_new)
            v_aug = (jnp.concatenate([v32, ones], axis=1) * wk_col).astype(BF16)
            upd = jnp.dot(kt, v_aug, preferred_element_type=F32)
            c_ref[h] = decay * c_aug + upd
            m_vals[h] = m_new
            o_ref[rows, cols] = hh
    for h in range(M_HEADS):
        m_ref[h] = jnp.broadcast_to(m_vals[h], (SUBLANES, LANES))


def _mlstm_call(q, kt, hin, gr3, gc, seq, *, reverse):
    t = q.shape[0]
    cb = MLSTM_CB
    while (seq.s1 // CHUNK) % cb or (seq.s2 // CHUNK) % cb:
        cb //= 2
    rows = cb * CHUNK
    nsteps = t // rows
    blk = (lambda s: nsteps - 1 - s) if reverse else (lambda s: s)
    vblk = OFF_V // M_WIDTH
    return pl.pallas_call(
        functools.partial(_mlstm_kernel, reverse=reverse, cb=cb, seq=seq),
        grid=(nsteps,),
        in_specs=[pl.BlockSpec((rows, M_WIDTH), lambda s: (blk(s), 0)),
                  pl.BlockSpec((M_WIDTH, rows), lambda s: (0, blk(s))),
                  pl.BlockSpec((rows, M_WIDTH), lambda s: (blk(s), vblk)),
                  pl.BlockSpec((cb, 3 * GDIR, CHUNK), lambda s: (blk(s), 0, 0)),
                  pl.BlockSpec((rows, LANES), lambda s: (blk(s), 0))],
        out_specs=pl.BlockSpec((rows, M_WIDTH), lambda s: (blk(s), 0)),
        out_shape=jax.ShapeDtypeStruct((t, M_WIDTH), F32),
        scratch_shapes=[pltpu.VMEM((M_HEADS, M_HEAD_DIM, M_HEAD_DIM + LANES), F32),
                        pltpu.VMEM((M_HEADS, SUBLANES, LANES), F32)],
        compiler_params=_cparams("arbitrary"),
        name="mlstm_bwd" if reverse else "mlstm_fwd",
    )(q, kt, hin, gr3, gc)


def _headnorm_kernel(hf_ref, hb_ref, uo_ref, ng_ref, o_ref):
    for h in range(M_HEADS):
        cols = slice(h * M_HEAD_DIM, (h + 1) * M_HEAD_DIM)
        hs = hf_ref[:, cols] + hb_ref[:, cols]
        mu = jnp.mean(hs, axis=1, keepdims=True)
        d = hs - mu
        var = jnp.mean(d * d, axis=1, keepdims=True)
        hn = d * lax.rsqrt(var + HN_EPS) * ng_ref[:, cols]
        o_ref[:, cols] = (hn * jax.nn.sigmoid(uo_ref[:, cols])).astype(o_ref.dtype)


def _headnorm_call(h_f, h_b, hin, norm_g):
    t = h_f.shape[0]
    tm = min(LN_TM, t)
    return pl.pallas_call(
        _headnorm_kernel,
        grid=(t // tm,),
        in_specs=[pl.BlockSpec((tm, M_WIDTH), lambda i: (i, 0)),
                  pl.BlockSpec((tm, M_WIDTH), lambda i: (i, 0)),
                  pl.BlockSpec((tm, M_WIDTH), lambda i: (i, OFF_O // M_WIDTH)),
                  pl.BlockSpec((1, M_WIDTH), lambda i: (0, 0))],
        out_specs=pl.BlockSpec((tm, M_WIDTH), lambda i: (i, 0)),
        out_shape=jax.ShapeDtypeStruct((t, M_WIDTH), BF16),
        compiler_params=_cparams("parallel"),
        name="headnorm",
    )(h_f, h_b, hin, norm_g.reshape(1, M_WIDTH))


def _pool_attn_kernel(prev_ref, cur_ref, next_ref, xq_ref, kv_ref, pw_ref, ps_ref, o_ref, *, tm, seq):
    xe, pos0, slen = _extended_tile(prev_ref, cur_ref, next_ref, tm, seq)
    pos = pos0 + lax.broadcasted_iota(I32, (tm, 1), 0)
    for gi, w in enumerate(POOL_WINDOWS):
        cols = slice(gi * POOL_GROUP, (gi + 1) * POOL_GROUP)
        u = xe[:, cols]
        win = _shift_rows(u, -1) + u
        half = 1
        while 2 * half < w:
            win = _shift_rows(win, -half) + _shift_rows(win, half)
            half *= 2
        cnt = (jnp.minimum(pos + w // 2, slen) - jnp.maximum(pos - w // 2, 0)).astype(F32)
        mean = win[HALO:HALO + tm] / cnt
        diff = (mean - u[HALO:HALO + tm]).astype(BF16)
        y = jnp.dot(diff, pw_ref[gi], preferred_element_type=F32)
        o_ref[:, cols] = (y * ps_ref[:, cols]).astype(o_ref.dtype)
    xq = xq_ref[...].astype(BF16)
    for h in range(X_HEADS):
        cols = slice(h * X_HEAD_DIM, (h + 1) * X_HEAD_DIM)
        kh = kv_ref[0, :, h * X_HEAD_DIM:(h + 1) * X_HEAD_DIM]
        vh = kv_ref[0, :, X_WIDTH + h * X_HEAD_DIM:X_WIDTH + (h + 1) * X_HEAD_DIM]
        s = lax.dot_general(xq[:, cols], kh, (((1,), (1,)), ((), ())),
                            preferred_element_type=F32) * (X_HEAD_DIM ** -0.5)
        e = jnp.exp(s - jnp.max(s, axis=1, keepdims=True))
        p = e / jnp.sum(e, axis=1, keepdims=True)
        y = jnp.dot(p.astype(BF16), vh, preferred_element_type=F32)
        o_ref[:, POOL_WIDTH + h * X_HEAD_DIM:POOL_WIDTH + (h + 1) * X_HEAD_DIM] = y.astype(o_ref.dtype)


def _pool_attn_call(hin, kv, pool_w, pool_scale, seq):
    t = hin.shape[0]
    tm = min(SEQ_TM, seq.s1, seq.s2)
    n_mem = kv.shape[1]
    prev, cur, nxt = _halo_specs(tm, POOL_WIDTH, OFF_P // POOL_WIDTH, t)
    return pl.pallas_call(
        functools.partial(_pool_attn_kernel, tm=tm, seq=seq),
        grid=(t // tm,),
        in_specs=[prev, cur, nxt,
                  pl.BlockSpec((tm, X_WIDTH), lambda i: (i, OFF_XQ // X_WIDTH)),
                  pl.BlockSpec((1, n_mem, 2 * X_WIDTH), lambda i: (seq.seq_id(i * tm), 0, 0)),
                  pl.BlockSpec((len(POOL_WINDOWS), POOL_GROUP, POOL_GROUP), lambda i: (0, 0, 0)),
                  pl.BlockSpec((1, POOL_WIDTH), lambda i: (0, 0))],
        out_specs=pl.BlockSpec((tm, POOL_WIDTH + X_WIDTH), lambda i: (i, 0)),
        out_shape=jax.ShapeDtypeStruct((t, POOL_WIDTH + X_WIDTH), BF16),
        compiler_params=_cparams("parallel"),
        name="pool_attn",
    )(hin, hin, hin, hin, kv, pool_w, pool_scale.reshape(1, POOL_WIDTH))


def _pack_halves(yb):
    n = yb.shape[1] // 2
    bits = pltpu.bitcast(yb.astype(F32), jnp.uint32)
    return jnp.bitwise_or(bits[:, n:], jnp.right_shift(bits[:, :n], jnp.uint32(16)))


SLAB = SUBLANES


def _store_slabs(o_ref, packed):
    rows = packed.shape[0]
    for s in range(SLAB):
        o_ref[pl.ds(s, rows, stride=SLAB), :] = packed[:, s * LANES:(s + 1) * LANES]


def _load_slabs_bf16(x_ref, rows):
    words = [x_ref[pl.ds(s, rows, stride=SLAB), :] for s in range(SLAB)]
    lo = [pltpu.bitcast(jnp.left_shift(w, jnp.uint32(16)), F32) for w in words]
    hi = [pltpu.bitcast(jnp.bitwise_and(w, jnp.uint32(0xFFFF0000)), F32) for w in words]
    return jnp.concatenate(lo + hi, axis=1).astype(BF16)


def _outproj_kernel(m_ref, pa_ref, x_ref, w_ref, g_ref, b_ref, o_ref, ob_ref, op_ref):
    acc = jnp.dot(m_ref[...], w_ref[0:M_WIDTH, :], preferred_element_type=F32)
    acc = acc + jnp.dot(pa_ref[...], w_ref[M_WIDTH:, :], preferred_element_type=F32)
    y = _layer_norm(ALPHA * x_ref[...] + acc, g_ref[...], b_ref[...], LN_EPS)
    o_ref[...] = y
    yb = y.astype(BF16)
    ob_ref[...] = yb
    _store_slabs(op_ref, _pack_halves(yb))


def _outproj_call(m_out, pa_out, x, w_out, g, b):
    t, d = x.shape
    tm = min(OUT_TM, t)
    return pl.pallas_call(
        _outproj_kernel,
        grid=(t // tm,),
        in_specs=[pl.BlockSpec((tm, M_WIDTH), lambda i: (i, 0)),
                  pl.BlockSpec((tm, POOL_WIDTH + X_WIDTH), lambda i: (i, 0)),
                  pl.BlockSpec((tm, d), lambda i: (i, 0)),
                  pl.BlockSpec((d, d), lambda i: (0, 0)),
                  pl.BlockSpec((1, d), lambda i: (0, 0)),
                  pl.BlockSpec((1, d), lambda i: (0, 0))],
        out_specs=[pl.BlockSpec((tm, d), lambda i: (i, 0)),
                   pl.BlockSpec((tm, d), lambda i: (i, 0)),
                   pl.BlockSpec((tm * SLAB, LANES), lambda i: (i, 0))],
        out_shape=[jax.ShapeDtypeStruct((t, d), F32), jax.ShapeDtypeStruct((t, d), BF16),
                   jax.ShapeDtypeStruct((t * SLAB, LANES), jnp.uint32)],
        compiler_params=_cparams("parallel"),
        name="outproj_ln",
    )(m_out, pa_out, x, w_out, g.reshape(1, d), b.reshape(1, d))


def _first_index_of_max(vals, idx, axis_size):
    mx = jnp.max(vals, axis=0, keepdims=True)
    first = jnp.min(jnp.where(vals == mx, idx, axis_size), axis=0, keepdims=True)
    return mx, first


def _router_kernel(x_ref, w_ref, b_ref, ids_ref, gate_ref, rank_ref, cnt_ref, carry_ref, *, tm):
    @pl.when(pl.program_id(0) == 0)
    def _():
        carry_ref[...] = jnp.zeros_like(carry_ref)

    logits = lax.dot_general(w_ref[...], x_ref[...], (((1,), (1,)), ((), ())),
                             preferred_element_type=F32)
    s = jax.nn.sigmoid(logits)
    sel = s + b_ref[...]
    neg = -jnp.inf
    sub = lax.broadcasted_iota(I32, (GROUP_SIZE, tm), 0)
    gscores = []
    for g in range(N_GROUPS):
        blk = sel[g * GROUP_SIZE:(g + 1) * GROUP_SIZE, :]
        m1, i1 = _first_index_of_max(blk, sub, GROUP_SIZE)
        m2 = jnp.max(jnp.where(sub == i1, neg, blk), axis=0, keepdims=True)
        gscores.append(m1 + m2)
    gs = jnp.concatenate(gscores, axis=0)
    gidx = lax.broadcasted_iota(I32, (N_GROUPS, tm), 0)
    gmask = jnp.zeros((N_GROUPS, tm), jnp.bool_)
    for _ in range(TOPK_GROUPS):
        _, gi = _first_index_of_max(gs, gidx, N_GROUPS)
        hit = gidx == gi
        gmask = jnp.logical_or(gmask, hit)
        gs = jnp.where(hit, neg, gs)
    gmask_f = gmask.astype(F32)
    masked = jnp.concatenate(
        [jnp.where(gmask_f[g:g + 1, :] > 0.5, sel[g * GROUP_SIZE:(g + 1) * GROUP_SIZE, :], neg)
         for g in range(N_GROUPS)], axis=0)
    eidx = lax.broadcasted_iota(I32, (N_EXPERTS, tm), 0)
    chosen = jnp.zeros((N_EXPERTS, tm), F32)
    ids, gates, hits = [], [], []
    for _ in range(TOP_K):
        _, ei = _first_index_of_max(masked, eidx, N_EXPERTS)
        hit = eidx == ei
        ids.append(ei)
        gates.append(jnp.sum(jnp.where(hit, s, 0.0), axis=0, keepdims=True))
        hits.append(hit)
        chosen = jnp.where(hit, 1.0, chosen)
        masked = jnp.where(hit, neg, masked)
    gate = jnp.concatenate(gates, axis=0)
    gate = gate / jnp.sum(gate, axis=0, keepdims=True) * ROUTED_SCALE
    ids_ref[...] = jnp.concatenate(ids, axis=0)
    gate_ref[...] = gate
    r = lax.broadcasted_iota(I32, (tm, tm), 0)
    c = lax.broadcasted_iota(I32, (tm, tm), 1)
    before = (r < c).astype(BF16)
    cnt = jnp.dot(chosen.astype(BF16), before, preferred_element_type=F32) + carry_ref[:, 0:1]
    ranks = [jnp.sum(jnp.where(hit, cnt, 0.0), axis=0, keepdims=True) for hit in hits]
    rank_ref[...] = jnp.concatenate(ranks, axis=0).astype(I32)
    total = carry_ref[:, 0:1] + jnp.sum(chosen, axis=1, keepdims=True)
    carry_ref[...] = jnp.broadcast_to(total, carry_ref.shape)
    cnt_ref[...] = jnp.broadcast_to(total, cnt_ref.shape)


def _router_call(xb, wrt, br):
    t, d = xb.shape
    tm = min(ROUTE_TM, t)
    return pl.pallas_call(
        functools.partial(_router_kernel, tm=tm),
        grid=(t // tm,),
        in_specs=[pl.BlockSpec((tm, d), lambda i: (i, 0)),
                  pl.BlockSpec((N_EXPERTS, d), lambda i: (0, 0)),
                  pl.BlockSpec((N_EXPERTS, 1), lambda i: (0, 0))],
        out_specs=[pl.BlockSpec((TOP_K, tm), lambda i: (0, i)),
                   pl.BlockSpec((TOP_K, tm), lambda i: (0, i)),
                   pl.BlockSpec((TOP_K, tm), lambda i: (0, i)),
                   pl.BlockSpec((N_EXPERTS, LANES), lambda i: (0, 0))],
        out_shape=[jax.ShapeDtypeStruct((TOP_K, t), I32),
                   jax.ShapeDtypeStruct((TOP_K, t), F32),
                   jax.ShapeDtypeStruct((TOP_K, t), I32),
                   jax.ShapeDtypeStruct((N_EXPERTS, LANES), F32)],
        scratch_shapes=[pltpu.VMEM((N_EXPERTS, LANES), F32)],
        compiler_params=_cparams("arbitrary"),
        name="router",
    )(xb, wrt, br.reshape(N_EXPERTS, 1))


def _dispatch_kernel(pad_lo_ref, pad_hi_ref, dest_ref, x_ref, xs_hbm, zrow, sem, *, tm):
    i = pl.program_id(0)

    @pl.when(i == 0)
    def _():
        zrow[...] = jnp.zeros_like(zrow)

        def per_expert(e, carry):
            lo, hi = pad_lo_ref[e], pad_hi_ref[e]

            def start(p, c):
                pltpu.make_async_copy(zrow, xs_hbm.at[p], sem.at[1]).start()
                return c

            def wait(p, c):
                pltpu.make_async_copy(zrow, xs_hbm.at[0], sem.at[1]).wait()
                return c

            lax.fori_loop(lo, hi, start, 0)
            lax.fori_loop(lo, hi, wait, 0)
            return carry

        lax.fori_loop(0, N_EXPERTS, per_expert, 0)

    for r in range(tm):
        for k in range(TOP_K):
            pltpu.make_async_copy(x_ref.at[r], xs_hbm.at[dest_ref[0, k, r]], sem.at[0]).start(priority=k % 2)
    for k in range(TOP_K):
        pltpu.make_async_copy(x_ref, xs_hbm.at[pl.ds(0, tm)], sem.at[0]).wait()


def _dispatch_call(pad_lo, pad_hi, dest3, xpk, n_rows):
    t = xpk.shape[0]
    tm = dest3.shape[2]
    grid_spec = pltpu.PrefetchScalarGridSpec(
        num_scalar_prefetch=2,
        grid=(t // tm,),
        in_specs=[pl.BlockSpec((1, TOP_K, tm), lambda i, lo, hi: (i, 0, 0), memory_space=pltpu.SMEM),
                  pl.BlockSpec((tm, SLAB, LANES), lambda i, lo, hi: (i, 0, 0))],
        out_specs=pl.BlockSpec(memory_space=pl.ANY),
        scratch_shapes=[pltpu.VMEM((SLAB, LANES), xpk.dtype), pltpu.SemaphoreType.DMA((2,))],
    )
    return pl.pallas_call(
        functools.partial(_dispatch_kernel, tm=tm),
        grid_spec=grid_spec,
        out_shape=jax.ShapeDtypeStruct((n_rows, SLAB, LANES), xpk.dtype),
        compiler_params=_cparams("arbitrary"),
        name="dispatch",
    )(pad_lo, pad_hi, dest3, xpk)


def _expert_kernel(blk_e_ref, nused_ref, x_ref, wgu_ref, wdn_ref, o_ref, wgu_b, wdn_b, *, bm):
    b = pl.program_id(0)

    @pl.when(jnp.logical_or(b == 0, blk_e_ref[b] != blk_e_ref[jnp.maximum(b - 1, 0)]))
    def _():
        wgu_b[...] = wgu_ref[0, 0].astype(BF16)
        wdn_b[...] = wdn_ref[0, 0].astype(BF16)

    @pl.when(b < nused_ref[0])
    def _():
        gu = jnp.dot(_load_slabs_bf16(x_ref, bm), wgu_b[...], preferred_element_type=F32)
        g = gu[:, :D_EXPERT]
        u = gu[:, D_EXPERT:]
        hmid = (g * jax.nn.sigmoid(g) * u).astype(BF16)
        o_ref[...] = jnp.dot(hmid, wdn_b[...], preferred_element_type=F32)

    @pl.when(b >= nused_ref[0])
    def _():
        o_ref[...] = jnp.zeros_like(o_ref)


def _expert_call(blk_e, n_used, xs, wgu, wdn, bm, layer):
    n_blocks = xs.shape[0] // (bm * SLAB)
    d = wdn.shape[-1]

    def x_block(b, be, nu):
        return (jnp.minimum(b, jnp.maximum(nu[0] - 1, 0)), 0)

    grid_spec = pltpu.PrefetchScalarGridSpec(
        num_scalar_prefetch=2,
        grid=(n_blocks,),
        in_specs=[pl.BlockSpec((bm * SLAB, LANES), x_block),
                  pl.BlockSpec((1, 1, d, 2 * D_EXPERT), lambda b, be, nu: (layer, be[b], 0, 0)),
                  pl.BlockSpec((1, 1, D_EXPERT, d), lambda b, be, nu: (layer, be[b], 0, 0))],
        out_specs=pl.BlockSpec((bm, d), lambda b, be, nu: (b, 0)),
        scratch_shapes=[pltpu.VMEM((d, 2 * D_EXPERT), BF16), pltpu.VMEM((D_EXPERT, d), BF16)],
    )
    return pl.pallas_call(
        functools.partial(_expert_kernel, bm=bm),
        grid_spec=grid_spec,
        out_shape=jax.ShapeDtypeStruct((n_blocks * bm, d), F32),
        compiler_params=_cparams("arbitrary"),
        name="experts",
    )(blk_e, n_used, xs, wgu, wdn)


def _combine_kernel(dest_ref, dest_next_ref, gate_ref, x_ref, xb_ref, ys_hbm, wsg_ref, wsd_ref, g_ref, b_ref,
                    *rest, tm, n1, nsteps):
    if n1 is None:
        o_ref, ob_ref, buf0, buf1, base_ref, sem = rest
    else:
        o1_ref, o2_ref, buf0, buf1, base_ref, sem = rest
    i = pl.program_id(0)
    bufs = (buf0, buf1)

    def issue(ref, s):
        for r in range(tm):
            for k in range(TOP_K):
                pltpu.make_async_copy(ys_hbm.at[pl.ds(ref[0, k, r], 1)],
                                      bufs[s].at[pl.ds(k * tm + r, 1)], sem.at[s]).start(priority=k % 2)

    def wait(s):
        pltpu.make_async_copy(ys_hbm.at[pl.ds(0, TOP_K * tm)], bufs[s], sem.at[s]).wait()

    @pl.when(i == 0)
    def _():
        issue(dest_ref, 0)

    gu = jnp.dot(xb_ref[...], wsg_ref[...], preferred_element_type=F32)
    g = gu[:, :D_SHARED]
    u = gu[:, D_SHARED:]
    hmid = (g * jax.nn.sigmoid(g) * u).astype(BF16)
    base_ref[...] = ALPHA * x_ref[...] + jnp.dot(hmid, wsd_ref[...], preferred_element_type=F32)

    def step(cur):
        wait(cur)
        issue(dest_next_ref, 1 - cur)
        acc = gate_ref[:, 0:1] * bufs[cur][0:tm, :]
        for k in range(1, TOP_K):
            acc = acc + gate_ref[:, k:k + 1] * bufs[cur][k * tm:(k + 1) * tm, :]
        y = _layer_norm(base_ref[...] + acc, g_ref[...], b_ref[...], LN_EPS)
        if n1 is None:
            o_ref[...] = y
            ob_ref[...] = y.astype(BF16)
        else:
            @pl.when(i < n1)
            def _():
                o1_ref[...] = y

            @pl.when(i >= n1)
            def _():
                o2_ref[...] = y

    for cur in range(2):
        pl.when(lax.rem(i, 2) == cur)(functools.partial(step, cur))

    @pl.when(i == nsteps - 1)
    def _():
        wait(nsteps % 2)


def _combine_call(dest3, gate_tk, x, xb, ys, wsg, wsd, g, b, split_rows=None):
    t, d = x.shape
    tm = dest3.shape[2]
    nsteps = t // tm
    if split_rows is None:
        n1 = None
        out_specs = [pl.BlockSpec((tm, d), lambda i: (i, 0)), pl.BlockSpec((tm, d), lambda i: (i, 0))]
        out_shape = [jax.ShapeDtypeStruct((t, d), F32), jax.ShapeDtypeStruct((t, d), BF16)]
    else:
        n1 = split_rows // tm
        out_specs = [pl.BlockSpec((tm, d), lambda i: (jnp.minimum(i, n1 - 1), 0)),
                     pl.BlockSpec((tm, d), lambda i: (jnp.maximum(i - n1, 0), 0))]
        out_shape = [jax.ShapeDtypeStruct((split_rows, d), F32), jax.ShapeDtypeStruct((t - split_rows, d), F32)]
    return pl.pallas_call(
        functools.partial(_combine_kernel, tm=tm, n1=n1, nsteps=nsteps),
        grid=(nsteps,),
        in_specs=[pl.BlockSpec((1, TOP_K, tm), lambda i: (i, 0, 0), memory_space=pltpu.SMEM),
                  pl.BlockSpec((1, TOP_K, tm), lambda i: (jnp.minimum(i + 1, nsteps - 1), 0, 0),
                               memory_space=pltpu.SMEM),
                  pl.BlockSpec((tm, TOP_K), lambda i: (i, 0)),
                  pl.BlockSpec((tm, d), lambda i: (i, 0)),
                  pl.BlockSpec((tm, d), lambda i: (i, 0)),
                  pl.BlockSpec(memory_space=pl.ANY),
                  pl.BlockSpec((d, 2 * D_SHARED), lambda i: (0, 0)),
                  pl.BlockSpec((D_SHARED, d), lambda i: (0, 0)),
                  pl.BlockSpec((1, d), lambda i: (0, 0)),
                  pl.BlockSpec((1, d), lambda i: (0, 0))],
        out_specs=out_specs,
        out_shape=out_shape,
        scratch_shapes=[pltpu.VMEM((TOP_K * tm, d), F32), pltpu.VMEM((TOP_K * tm, d), F32),
                        pltpu.VMEM((tm, d), F32), pltpu.SemaphoreType.DMA((2,))],
        compiler_params=_cparams("arbitrary"),
        name="combine_ln",
    )(dest3, dest3, gate_tk, x, xb, ys, wsg, wsd, g.reshape(1, d), b.reshape(1, d))


def _dispatch_tables(ids, rank, counts_f, bm):
    k, t = ids.shape
    tk = k * t
    n_blocks = -(-tk // bm) + N_EXPERTS
    counts = counts_f[:, 0].astype(I32)
    padded = (counts + bm - 1) // bm * bm
    pends = jnp.cumsum(padded)
    pstart = pends - padded
    experts = jnp.arange(N_EXPERTS, dtype=I32)
    dest = jnp.sum(jnp.where(ids[:, :, None] == experts, pstart, 0), axis=-1) + rank
    blk_first = jnp.arange(n_blocks, dtype=I32) * bm
    blk_e = jnp.minimum(jnp.sum(pends[None, :] <= blk_first[:, None], axis=1), N_EXPERTS - 1).astype(I32)
    n_used = (pends[-1:] // bm).astype(I32)
    return dest, blk_e, n_used, (pstart + counts).astype(I32), pends.astype(I32), n_blocks * bm


def _layer(x, xb, memb, seq, layer, w_in, b_gate, conv_w, conv_b, mh_norm_g, pool_w, pool_scale, w_mem_kv, w_out,
           ln1_g, ln1_b, w_router, b_router, w_gu_all, w_dn_all, w_sh_gu, w_sh_dn, ln2_g, ln2_b, last):
    t = x.shape[0]
    hin = _mm_call(xb, w_in[:, :OFF_G].astype(BF16), F32, "in_proj")
    g_rows, g_cols = _gates_call(xb, _gate_rows(w_in[:, OFF_G:].T).astype(BF16), _gate_rows(b_gate))
    gr3 = g_rows.reshape(3 * GDIR, t // CHUNK, CHUNK).transpose(1, 0, 2)
    q = _conv_call(hin, conv_w, conv_b, seq, col0=OFF_Q, scale=1.0, transpose_out=False)
    kt = _conv_call(hin, conv_w, conv_b, seq, col0=OFF_K, scale=M_HEAD_DIM ** -0.5, transpose_out=True)
    h_f = _mlstm_call(q, kt, hin, gr3, g_cols, seq, reverse=False)
    h_b = _mlstm_call(q, kt, hin, gr3, g_cols, seq, reverse=True)
    m_out = _headnorm_call(h_f, h_b, hin, mh_norm_g)
    kv = _mm_call(memb, w_mem_kv.astype(BF16), BF16, "mem_kv").reshape(seq.nseq, -1, 2 * X_WIDTH)
    pa_out = _pool_attn_call(hin, kv, pool_w.astype(BF16), pool_scale, seq)
    x1, x1b, x1p = _outproj_call(m_out, pa_out, x, w_out.astype(BF16), ln1_g, ln1_b)
    ids, gate, rank, counts = _router_call(x1b, w_router.T.astype(BF16), b_router)
    bm = MOE_BM
    dest, blk_e, n_used, pad_lo, pad_hi, n_rows = _dispatch_tables(ids, rank, counts, bm)
    ctm = min(COMB_TM, t)
    dest3 = dest.reshape(TOP_K, t // ctm, ctm).transpose(1, 0, 2)
    xs = _dispatch_call(pad_lo, pad_hi, dest3, x1p.reshape(t, SLAB, LANES), n_rows)
    ys = _expert_call(blk_e, n_used, xs.reshape(n_rows * SLAB, LANES), w_gu_all, w_dn_all, bm, layer)
    return _combine_call(dest3, gate.T, x1, x1b, ys, w_sh_gu.astype(BF16), w_sh_dn.astype(BF16), ln2_g, ln2_b,
                         split_rows=seq.t1 if last else None)


def kernel(x_prompt, x_sample, mem_prompt, mem_sample, ln_in_g, ln_in_b, w_in, b_gate, conv_w, conv_b, mh_norm_g, pool_w, pool_scale, w_mem_kv, w_out, ln1_g, ln1_b, w_router, b_router, w_gu, w_dn, w_sh_gu, w_sh_dn, ln2_g, ln2_b):
    b1, s1, d = x_prompt.shape
    b2, s2, _ = x_sample.shape
    seq = SeqInfo(b1, s1, b2, s2)
    memb = jnp.concatenate([mem_prompt.reshape(-1, d), mem_sample.reshape(-1, d)], axis=0).astype(BF16)
    x, xb = _ln_call(x_prompt.reshape(b1 * s1, d), x_sample.reshape(b2 * s2, d), ln_in_g, ln_in_b)
    depth = w_in.shape[0]
    for l in range(depth):
        x, xb = _layer(x, xb, memb, seq, l, w_in[l], b_gate[l], conv_w[l], conv_b[l], mh_norm_g[l], pool_w[l],
                       pool_scale[l], w_mem_kv[l], w_out[l], ln1_g[l], ln1_b[l], w_router[l], b_router[l],
                       w_gu, w_dn, w_sh_gu[l], w_sh_dn[l], ln2_g[l], ln2_b[l], last=l == depth - 1)
    return (x.reshape(b1, s1, d), xb.reshape(b2, s2, d))
```

```python
import functools

import jax
import jax.numpy as jnp
from jax import lax
from jax.experimental import pallas as pl
from jax.experimental.pallas import tpu as pltpu

F32 = jnp.float32
BF16 = jnp.bfloat16
I32 = jnp.int32

DEPTH = 4
D_MODEL = 2048
M_HEADS = 4
M_WIDTH = 1024
M_HEAD_DIM = 256
CHUNK = 128
CONV_K = 5
POOL_WINDOWS = (2, 4, 8, 16)
POOL_WIDTH = 512
POOL_GROUP = 128
X_HEADS = 4
X_WIDTH = 512
X_HEAD_DIM = 128
OFF_Q = 0
OFF_K = OFF_Q + M_WIDTH
OFF_V = OFF_K + M_WIDTH
OFF_O = OFF_V + M_WIDTH
OFF_P = OFF_O + M_WIDTH
OFF_XQ = OFF_P + POOL_WIDTH
OFF_G = OFF_XQ + X_WIDTH
N_GATES = 4 * M_HEADS
N_EXPERTS = 64
TOP_K = 8
N_GROUPS = 8
GROUP_SIZE = N_EXPERTS // N_GROUPS
TOPK_GROUPS = 4
D_EXPERT = 512
D_SHARED = 512
ROUTED_SCALE = 2.5
ALPHA = (2 * DEPTH) ** 0.25
LN_EPS = 1e-5
HN_EPS = 1e-6

LANES = 128
SUBLANES = 8
HALO = SUBLANES
VMEM_LIMIT = 56 * 1024 * 1024

LN_TM = 512
MM_TM = 512
MM_TN = 1024
SEQ_TM = 512
MLSTM_CB = 4
OUT_TM = 256
ROUTE_TM = 512
MOE_BM = 512
FFN_SLICE = 256
DISP_TM = 256
COMB_TM = 128


def _cparams(*sem):
    return pltpu.CompilerParams(dimension_semantics=sem, vmem_limit_bytes=VMEM_LIMIT)


class SeqInfo:
    def __init__(self, b1, s1, b2, s2):
        self.b1, self.s1, self.b2, self.s2 = b1, s1, b2, s2
        self.t1 = b1 * s1
        self.t = b1 * s1 + b2 * s2
        self.nseq = b1 + b2

    def pos_len(self, row0):
        in1 = row0 < self.t1
        pos = jnp.where(in1, lax.rem(row0, self.s1), lax.rem(row0 - self.t1, self.s2))
        slen = jnp.where(in1, self.s1, self.s2)
        return pos, slen

    def seq_id(self, row0):
        in1 = row0 < self.t1
        return jnp.where(in1, row0 // self.s1, self.b1 + (row0 - self.t1) // self.s2)


def _layer_norm(y, g, b, eps):
    mu = jnp.mean(y, axis=-1, keepdims=True)
    d = y - mu
    var = jnp.mean(d * d, axis=-1, keepdims=True)
    return d * lax.rsqrt(var + eps) * g + b


def _ln_kernel(x1_ref, x2_ref, g_ref, b_ref, o_ref, ob_ref, *, n1):
    x = jnp.where(pl.program_id(0) < n1, x1_ref[...], x2_ref[...])
    y = _layer_norm(x, g_ref[...], b_ref[...], LN_EPS)
    o_ref[...] = y
    ob_ref[...] = y.astype(BF16)


def _ln_call(x1, x2, g, b):
    t1, d = x1.shape
    t2 = x2.shape[0]
    tm = min(LN_TM, t1, t2)
    n1, n2 = t1 // tm, t2 // tm
    t = t1 + t2
    return pl.pallas_call(
        functools.partial(_ln_kernel, n1=n1),
        grid=(n1 + n2,),
        in_specs=[pl.BlockSpec((tm, d), lambda i: (jnp.minimum(i, n1 - 1), 0)),
                  pl.BlockSpec((tm, d), lambda i: (jnp.maximum(i - n1, 0), 0)),
                  pl.BlockSpec((1, d), lambda i: (0, 0)),
                  pl.BlockSpec((1, d), lambda i: (0, 0))],
        out_specs=[pl.BlockSpec((tm, d), lambda i: (i, 0)),
                   pl.BlockSpec((tm, d), lambda i: (i, 0))],
        out_shape=[jax.ShapeDtypeStruct((t, d), F32), jax.ShapeDtypeStruct((t, d), BF16)],
        compiler_params=_cparams("arbitrary"),
        name="ln_in",
    )(x1, x2, g.reshape(1, d), b.reshape(1, d))


def _mm_kernel(x_ref, w_ref, o_ref):
    o_ref[...] = jnp.dot(x_ref[...], w_ref[...], preferred_element_type=F32).astype(o_ref.dtype)


def _mm_call(xb, wb, out_dtype, name):
    m, k = xb.shape
    n = wb.shape[1]
    tm = min(MM_TM, m)
    while m % tm:
        tm //= 2
    tn = min(MM_TN, n)
    return pl.pallas_call(
        _mm_kernel,
        grid=(n // tn, m // tm),
        in_specs=[pl.BlockSpec((tm, k), lambda j, i: (i, 0)),
                  pl.BlockSpec((k, tn), lambda j, i: (0, j))],
        out_specs=pl.BlockSpec((tm, tn), lambda j, i: (i, j)),
        out_shape=jax.ShapeDtypeStruct((m, n), out_dtype),
        compiler_params=_cparams("parallel", "parallel"),
        name=name,
    )(xb, wb)


GDIR = 2 * M_HEADS


def _chunk_scan(x, op, reverse):
    n = x.shape[1]
    pos = lax.rem(lax.broadcasted_iota(I32, x.shape, 1), CHUNK)
    s = 1
    while s < CHUNK:
        if reverse:
            x = jnp.where(pos < CHUNK - s, op(x, pltpu.roll(x, n - s, axis=1)), x)
        else:
            x = jnp.where(pos >= s, op(x, pltpu.roll(x, s, axis=1)), x)
        s *= 2
    return x


def _gates_kernel(x_ref, w_ref, b_ref, row_ref, col_ref):
    g = lax.dot_general(w_ref[...], x_ref[...], (((1,), (1,)), ((), ())),
                        preferred_element_type=F32) + b_ref[...]
    gi = g[0:GDIR]
    logf = jax.nn.log_sigmoid(g[GDIR:2 * GDIR])
    is_fwd = lax.broadcasted_iota(I32, gi.shape, 0) < M_HEADS
    pre = _chunk_scan(logf, jnp.add, False)
    suf = _chunk_scan(logf, jnp.add, True)
    b = jnp.where(is_fwd, pre, suf)
    total = pre + suf - logf
    ib = gi - b
    cmax = jnp.where(is_fwd, _chunk_scan(ib, jnp.maximum, False), _chunk_scan(ib, jnp.maximum, True))
    lw = total + ib
    lw_max = jnp.maximum(_chunk_scan(lw, jnp.maximum, False), _chunk_scan(lw, jnp.maximum, True))
    row_ref[...] = jnp.concatenate([ib, total, lw_max], axis=0)
    pad = jnp.zeros((LANES - 3 * GDIR, gi.shape[1]), F32)
    col_ref[...] = jnp.concatenate([b, cmax, lw, pad], axis=0).T


def _gates_call(xb, wgt, bg):
    t, d = xb.shape
    tm = min(MM_TM, t)
    return pl.pallas_call(
        _gates_kernel,
        grid=(t // tm,),
        in_specs=[pl.BlockSpec((tm, d), lambda i: (i, 0)),
                  pl.BlockSpec((N_GATES, d), lambda i: (0, 0)),
                  pl.BlockSpec((N_GATES, 1), lambda i: (0, 0))],
        out_specs=[pl.BlockSpec((3 * GDIR, tm), lambda i: (0, i)),
                   pl.BlockSpec((tm, LANES), lambda i: (i, 0))],
        out_shape=[jax.ShapeDtypeStruct((3 * GDIR, t), F32), jax.ShapeDtypeStruct((t, LANES), F32)],
        compiler_params=_cparams("parallel"),
        name="gates",
    )(xb, wgt, bg.reshape(N_GATES, 1))


def _gate_rows(a):
    h = M_HEADS
    return jnp.concatenate([a[0:h], a[2 * h:3 * h], a[h:2 * h], a[3 * h:4 * h]], axis=0)


def _halo_specs(tm, width, col_block, n_rows):
    per = tm // HALO
    last = n_rows // HALO - 1
    prev = pl.BlockSpec((HALO, width), lambda i, *_: (jnp.maximum(i * per - 1, 0), col_block))
    cur = pl.BlockSpec((tm, width), lambda i, *_: (i, col_block))
    nxt = pl.BlockSpec((HALO, width), lambda i, *_: (jnp.minimum((i + 1) * per, last), col_block))
    return prev, cur, nxt


def _extended_tile(prev_ref, cur_ref, next_ref, tm, seq):
    row0 = pl.program_id(0) * tm
    pos, slen = seq.pos_len(row0)
    prev = jnp.where(pos == 0, 0.0, prev_ref[...])
    nxt = jnp.where(pos + tm == slen, 0.0, next_ref[...])
    return jnp.concatenate([prev, cur_ref[...], nxt], axis=0), pos, slen


def _shift_rows(xe, d):
    n = xe.shape[0]
    if d == 0:
        return xe
    return pltpu.roll(xe, (n - d) % n, axis=0)


def _conv_kernel(prev_ref, cur_ref, next_ref, w_ref, b_ref, o_ref, *, tm, tc, seq, scale, transpose_out):
    xe, _, _ = _extended_tile(prev_ref, cur_ref, next_ref, tm, seq)
    acc = jnp.broadcast_to(b_ref[...], (tm, tc))
    for j in range(CONV_K):
        sh = _shift_rows(xe, j - CONV_K // 2)
        acc = acc + sh[HALO:HALO + tm] * w_ref[j:j + 1, :]
    y = acc * jax.nn.sigmoid(acc) * scale
    o_ref[...] = (y.T if transpose_out else y).astype(BF16)


def _conv_call(hin, conv_w, conv_b, seq, *, col0, scale, transpose_out):
    t = hin.shape[0]
    tm = min(SEQ_TM, seq.s1, seq.s2)
    tc = 512
    j0 = col0 // tc
    prev = pl.BlockSpec((HALO, tc), lambda i, j: (jnp.maximum(i * (tm // HALO) - 1, 0), j + j0))
    cur = pl.BlockSpec((tm, tc), lambda i, j: (i, j + j0))
    nxt = pl.BlockSpec((HALO, tc), lambda i, j: (jnp.minimum((i + 1) * (tm // HALO), t // HALO - 1), j + j0))
    if transpose_out:
        out_spec = pl.BlockSpec((tc, tm), lambda i, j: (j, i))
        out_shape = jax.ShapeDtypeStruct((M_WIDTH, t), BF16)
    else:
        out_spec = pl.BlockSpec((tm, tc), lambda i, j: (i, j))
        out_shape = jax.ShapeDtypeStruct((t, M_WIDTH), BF16)
    return pl.pallas_call(
        functools.partial(_conv_kernel, tm=tm, tc=tc, seq=seq, scale=scale, transpose_out=transpose_out),
        grid=(t // tm, M_WIDTH // tc),
        in_specs=[prev, cur, nxt,
                  pl.BlockSpec((CONV_K, tc), lambda i, j: (0, j + j0)),
                  pl.BlockSpec((1, tc), lambda i, j: (0, j + j0))],
        out_specs=out_spec,
        out_shape=out_shape,
        compiler_params=_cparams("parallel", "parallel"),
        name="conv_silu_t" if transpose_out else "conv_silu",
    )(hin, hin, hin, conv_w, conv_b.reshape(1, 2 * M_WIDTH))


def _mlstm_kernel(q_ref, kt_ref, v_ref, gr_ref, gc_ref, o_ref, c_ref, m_ref, *, reverse, cb, seq):
    L, DK, DV = CHUNK, M_HEAD_DIM, M_HEAD_DIM
    step = pl.program_id(0)
    nsteps = pl.num_programs(0)
    blk = (nsteps - 1 - step) if reverse else step
    ti = lax.broadcasted_iota(I32, (L, L), 0)
    si = lax.broadcasted_iota(I32, (L, L), 1)
    causal = (si >= ti) if reverse else (si <= ti)
    slot0 = M_HEADS if reverse else 0

    @pl.when(step == 0)
    def _():
        c_ref[...] = jnp.zeros_like(c_ref)
        m_ref[...] = jnp.zeros_like(m_ref)

    m_vals = [m_ref[h, 0:1, 0:1] for h in range(M_HEADS)]
    for jj in range(cb):
        j = (cb - 1 - jj) if reverse else jj
        rows = slice(j * L, (j + 1) * L)
        pos, slen = seq.pos_len((blk * cb + j) * L)
        at_start = (pos + L == slen) if reverse else (pos == 0)
        grow = gr_ref[j]
        ones = jnp.ones((L, LANES), F32)
        for h in range(M_HEADS):
            cols = slice(h * DK, (h + 1) * DK)
            sl = slot0 + h
            q = q_ref[rows, cols]
            kt = kt_ref[cols, rows]
            v32 = v_ref[rows, cols]
            ib_row = grow[sl:sl + 1, :]
            g_tot = grow[GDIR + sl:GDIR + sl + 1, 0:1]
            lw_max = grow[2 * GDIR + sl:2 * GDIR + sl + 1, 0:1]
            b_col = gc_ref[rows, sl:sl + 1]
            cmax_col = gc_ref[rows, GDIR + sl:GDIR + sl + 1]
            lw_col = gc_ref[rows, 2 * GDIR + sl:2 * GDIR + sl + 1]
            m_prev = jnp.where(at_start, 0.0, m_vals[h])
            m_rel = jnp.maximum(m_prev, cmax_col)
            m_t = b_col + m_rel
            dexp = jnp.exp(jnp.where(causal, ib_row - m_rel, -jnp.inf))
            s = jnp.dot(q, kt, preferred_element_type=F32)
            a = (s * dexp).astype(BF16)
            inter = jnp.exp(m_prev - m_rel)
            c_aug = jnp.where(at_start, 0.0, c_ref[h])
            qc = jnp.dot(q, c_aug.astype(BF16), preferred_element_type=F32)
            v_ext = jnp.concatenate([v32, ones], axis=1).astype(BF16)
            nd = jnp.dot(a, v_ext, preferred_element_type=F32) + inter * qc
            den = nd[:, DV:DV + 1]
            hh = nd[:, :DV] / jnp.maximum(jnp.abs(den), jnp.exp(-m_t))
            m_new = jnp.maximum(g_tot + m_prev, lw_max)
            wk_col = jnp.exp(lw_col - m_new)
            decay = jnp.exp(g_tot + m_prev - m_new)
            v_aug = (jnp.concatenate([v32, ones], axis=1) * wk_col).astype(BF16)
            upd = jnp.dot(kt, v_aug, preferred_element_type=F32)
            c_ref[h] = decay * c_aug + upd
            m_vals[h] = m_new
            o_ref[rows, cols] = hh
    for h in range(M_HEADS):
        m_ref[h] = jnp.broadcast_to(m_vals[h], (SUBLANES, LANES))


def _mlstm_call(q, kt, hin, gr3, gc, seq, *, reverse):
    t = q.shape[0]
    cb = MLSTM_CB
    while (seq.s1 // CHUNK) % cb or (seq.s2 // CHUNK) % cb:
        cb //= 2
    rows = cb * CHUNK
    nsteps = t // rows
    blk = (lambda s: nsteps - 1 - s) if reverse else (lambda s: s)
    vblk = OFF_V // M_WIDTH
    return pl.pallas_call(
        functools.partial(_mlstm_kernel, reverse=reverse, cb=cb, seq=seq),
        grid=(nsteps,),
        in_specs=[pl.BlockSpec((rows, M_WIDTH), lambda s: (blk(s), 0)),
                  pl.BlockSpec((M_WIDTH, rows), lambda s: (0, blk(s))),
                  pl.BlockSpec((rows, M_WIDTH), lambda s: (blk(s), vblk)),
                  pl.BlockSpec((cb, 3 * GDIR, CHUNK), lambda s: (blk(s), 0, 0)),
                  pl.BlockSpec((rows, LANES), lambda s: (blk(s), 0))],
        out_specs=pl.BlockSpec((rows, M_WIDTH), lambda s: (blk(s), 0)),
        out_shape=jax.ShapeDtypeStruct((t, M_WIDTH), F32),
        scratch_shapes=[pltpu.VMEM((M_HEADS, M_HEAD_DIM, M_HEAD_DIM + LANES), F32),
                        pltpu.VMEM((M_HEADS, SUBLANES, LANES), F32)],
        compiler_params=_cparams("arbitrary"),
        name="mlstm_bwd" if reverse else "mlstm_fwd",
    )(q, kt, hin, gr3, gc)


def _headnorm_kernel(hf_ref, hb_ref, uo_ref, ng_ref, o_ref):
    for h in range(M_HEADS):
        cols = slice(h * M_HEAD_DIM, (h + 1) * M_HEAD_DIM)
        hs = hf_ref[:, cols] + hb_ref[:, cols]
        mu = jnp.mean(hs, axis=1, keepdims=True)
        d = hs - mu
        var = jnp.mean(d * d, axis=1, keepdims=True)
        hn = d * lax.rsqrt(var + HN_EPS) * ng_ref[:, cols]
        o_ref[:, cols] = (hn * jax.nn.sigmoid(uo_ref[:, cols])).astype(o_ref.dtype)


def _headnorm_call(h_f, h_b, hin, norm_g):
    t = h_f.shape[0]
    tm = min(LN_TM, t)
    return pl.pallas_call(
        _headnorm_kernel,
        grid=(t // tm,),
        in_specs=[pl.BlockSpec((tm, M_WIDTH), lambda i: (i, 0)),
                  pl.BlockSpec((tm, M_WIDTH), lambda i: (i, 0)),
                  pl.BlockSpec((tm, M_WIDTH), lambda i: (i, OFF_O // M_WIDTH)),
                  pl.BlockSpec((1, M_WIDTH), lambda i: (0, 0))],
        out_specs=pl.BlockSpec((tm, M_WIDTH), lambda i: (i, 0)),
        out_shape=jax.ShapeDtypeStruct((t, M_WIDTH), BF16),
        compiler_params=_cparams("parallel"),
        name="headnorm",
    )(h_f, h_b, hin, norm_g.reshape(1, M_WIDTH))


def _pool_attn_kernel(prev_ref, cur_ref, next_ref, xq_ref, kv_ref, pw_ref, ps_ref, o_ref, *, tm, seq):
    xe, pos0, slen = _extended_tile(prev_ref, cur_ref, next_ref, tm, seq)
    pos = pos0 + lax.broadcasted_iota(I32, (tm, 1), 0)
    for gi, w in enumerate(POOL_WINDOWS):
        cols = slice(gi * POOL_GROUP, (gi + 1) * POOL_GROUP)
        u = xe[:, cols]
        win = _shift_rows(u, -1) + u
        half = 1
        while 2 * half < w:
            win = _shift_rows(win, -half) + _shift_rows(win, half)
            half *= 2
        cnt = (jnp.minimum(pos + w // 2, slen) - jnp.maximum(pos - w // 2, 0)).astype(F32)
        mean = win[HALO:HALO + tm] / cnt
        diff = (mean - u[HALO:HALO + tm]).astype(BF16)
        y = jnp.dot(diff, pw_ref[gi], preferred_element_type=F32)
        o_ref[:, cols] = (y * ps_ref[:, cols]).astype(o_ref.dtype)
    xq = xq_ref[...].astype(BF16)
    for h in range(X_HEADS):
        cols = slice(h * X_HEAD_DIM, (h + 1) * X_HEAD_DIM)
        kh = kv_ref[0, :, h * X_HEAD_DIM:(h + 1) * X_HEAD_DIM]
        vh = kv_ref[0, :, X_WIDTH + h * X_HEAD_DIM:X_WIDTH + (h + 1) * X_HEAD_DIM]
        s = lax.dot_general(xq[:, cols], kh, (((1,), (1,)), ((), ())),
                            preferred_element_type=F32) * (X_HEAD_DIM ** -0.5)
        e = jnp.exp(s - jnp.max(s, axis=1, keepdims=True))
        p = e / jnp.sum(e, axis=1, keepdims=True)
        y = jnp.dot(p.astype(BF16), vh, preferred_element_type=F32)
        o_ref[:, POOL_WIDTH + h * X_HEAD_DIM:POOL_WIDTH + (h + 1) * X_HEAD_DIM] = y.astype(o_ref.dtype)


def _pool_attn_call(hin, kv, pool_w, pool_scale, seq):
    t = hin.shape[0]
    tm = min(SEQ_TM, seq.s1, seq.s2)
    n_mem = kv.shape[1]
    prev, cur, nxt = _halo_specs(tm, POOL_WIDTH, OFF_P // POOL_WIDTH, t)
    return pl.pallas_call(
        functools.partial(_pool_attn_kernel, tm=tm, seq=seq),
        grid=(t // tm,),
        in_specs=[prev, cur, nxt,
                  pl.BlockSpec((tm, X_WIDTH), lambda i: (i, OFF_XQ // X_WIDTH)),
                  pl.BlockSpec((1, n_mem, 2 * X_WIDTH), lambda i: (seq.seq_id(i * tm), 0, 0)),
                  pl.BlockSpec((len(POOL_WINDOWS), POOL_GROUP, POOL_GROUP), lambda i: (0, 0, 0)),
                  pl.BlockSpec((1, POOL_WIDTH), lambda i: (0, 0))],
        out_specs=pl.BlockSpec((tm, POOL_WIDTH + X_WIDTH), lambda i: (i, 0)),
        out_shape=jax.ShapeDtypeStruct((t, POOL_WIDTH + X_WIDTH), BF16),
        compiler_params=_cparams("parallel"),
        name="pool_attn",
    )(hin, hin, hin, hin, kv, pool_w, pool_scale.reshape(1, POOL_WIDTH))


def _pack_halves(yb):
    n = yb.shape[1] // 2
    bits = pltpu.bitcast(yb.astype(F32), jnp.uint32)
    return jnp.bitwise_or(bits[:, n:], jnp.right_shift(bits[:, :n], jnp.uint32(16)))


SLAB = SUBLANES


def _store_slabs(o_ref, packed):
    rows = packed.shape[0]
    for s in range(SLAB):
        o_ref[pl.ds(s, rows, stride=SLAB), :] = packed[:, s * LANES:(s + 1) * LANES]


def _load_slabs_bf16(x_ref, rows):
    words = [x_ref[pl.ds(s, rows, stride=SLAB), :] for s in range(SLAB)]
    lo = [pltpu.bitcast(jnp.left_shift(w, jnp.uint32(16)), F32) for w in words]
    hi = [pltpu.bitcast(jnp.bitwise_and(w, jnp.uint32(0xFFFF0000)), F32) for w in words]
    return jnp.concatenate(lo + hi, axis=1).astype(BF16)


def _outproj_kernel(m_ref, pa_ref, x_ref, w_ref, g_ref, b_ref, o_ref, ob_ref, op_ref):
    acc = jnp.dot(m_ref[...], w_ref[0:M_WIDTH, :], preferred_element_type=F32)
    acc = acc + jnp.dot(pa_ref[...], w_ref[M_WIDTH:, :], preferred_element_type=F32)
    y = _layer_norm(ALPHA * x_ref[...] + acc, g_ref[...], b_ref[...], LN_EPS)
    o_ref[...] = y
    yb = y.astype(BF16)
    ob_ref[...] = yb
    _store_slabs(op_ref, _pack_halves(yb))


def _outproj_call(m_out, pa_out, x, w_out, g, b):
    t, d = x.shape
    tm = min(OUT_TM, t)
    return pl.pallas_call(
        _outproj_kernel,
        grid=(t // tm,),
        in_specs=[pl.BlockSpec((tm, M_WIDTH), lambda i: (i, 0)),
                  pl.BlockSpec((tm, POOL_WIDTH + X_WIDTH), lambda i: (i, 0)),
                  pl.BlockSpec((tm, d), lambda i: (i, 0)),
                  pl.BlockSpec((d, d), lambda i: (0, 0)),
                  pl.BlockSpec((1, d), lambda i: (0, 0)),
                  pl.BlockSpec((1, d), lambda i: (0, 0))],
        out_specs=[pl.BlockSpec((tm, d), lambda i: (i, 0)),
                   pl.BlockSpec((tm, d), lambda i: (i, 0)),
                   pl.BlockSpec((tm * SLAB, LANES), lambda i: (i, 0))],
        out_shape=[jax.ShapeDtypeStruct((t, d), F32), jax.ShapeDtypeStruct((t, d), BF16),
                   jax.ShapeDtypeStruct((t * SLAB, LANES), jnp.uint32)],
        compiler_params=_cparams("parallel"),
        name="outproj_ln",
    )(m_out, pa_out, x, w_out, g.reshape(1, d), b.reshape(1, d))


def _first_index_of_max(vals, idx, axis_size):
    mx = jnp.max(vals, axis=0, keepdims=True)
    first = jnp.min(jnp.where(vals == mx, idx, axis_size), axis=0, keepdims=True)
    return mx, first


def _router_kernel(x_ref, w_ref, b_ref, ids_ref, gate_ref, rank_ref, cnt_ref, carry_ref, *, tm):
    @pl.when(pl.program_id(0) == 0)
    def _():
        carry_ref[...] = jnp.zeros_like(carry_ref)

    logits = lax.dot_general(w_ref[...], x_ref[...], (((1,), (1,)), ((), ())),
                             preferred_element_type=F32)
    s = jax.nn.sigmoid(logits)
    sel = s + b_ref[...]
    neg = -jnp.inf
    sub = lax.broadcasted_iota(I32, (GROUP_SIZE, tm), 0)
    gscores = []
    for g in range(N_GROUPS):
        blk = sel[g * GROUP_SIZE:(g + 1) * GROUP_SIZE, :]
        m1, i1 = _first_index_of_max(blk, sub, GROUP_SIZE)
        m2 = jnp.max(jnp.where(sub == i1, neg, blk), axis=0, keepdims=True)
        gscores.append(m1 + m2)
    gs = jnp.concatenate(gscores, axis=0)
    gidx = lax.broadcasted_iota(I32, (N_GROUPS, tm), 0)
    gmask = jnp.zeros((N_GROUPS, tm), jnp.bool_)
    for _ in range(TOPK_GROUPS):
        _, gi = _first_index_of_max(gs, gidx, N_GROUPS)
        hit = gidx == gi
        gmask = jnp.logical_or(gmask, hit)
        gs = jnp.where(hit, neg, gs)
    gmask_f = gmask.astype(F32)
    masked = jnp.concatenate(
        [jnp.where(gmask_f[g:g + 1, :] > 0.5, sel[g * GROUP_SIZE:(g + 1) * GROUP_SIZE, :], neg)
         for g in range(N_GROUPS)], axis=0)
    eidx = lax.broadcasted_iota(I32, (N_EXPERTS, tm), 0)
    chosen = jnp.zeros((N_EXPERTS, tm), F32)
    ids, gates, hits = [], [], []
    for _ in range(TOP_K):
        _, ei = _first_index_of_max(masked, eidx, N_EXPERTS)
        hit = eidx == ei
        ids.append(ei)
        gates.append(jnp.sum(jnp.where(hit, s, 0.0), axis=0, keepdims=True))
        hits.append(hit)
        chosen = jnp.where(hit, 1.0, chosen)
        masked = jnp.where(hit, neg, masked)
    gate = jnp.concatenate(gates, axis=0)
    gate = gate / jnp.sum(gate, axis=0, keepdims=True) * ROUTED_SCALE
    ids_ref[...] = jnp.concatenate(ids, axis=0)
    gate_ref[...] = gate
    r = lax.broadcasted_iota(I32, (tm, tm), 0)
    c = lax.broadcasted_iota(I32, (tm, tm), 1)
    before = (r < c).astype(BF16)
    cnt = jnp.dot(chosen.astype(BF16), before, preferred_element_type=F32) + carry_ref[:, 0:1]
    ranks = [jnp.sum(jnp.where(hit, cnt, 0.0), axis=0, keepdims=True) for hit in hits]
    rank_ref[...] = jnp.concatenate(ranks, axis=0).astype(I32)
    total = carry_ref[:, 0:1] + jnp.sum(chosen, axis=1, keepdims=True)
    carry_ref[...] = jnp.broadcast_to(total, carry_ref.shape)
    cnt_ref[...] = jnp.broadcast_to(total, cnt_ref.shape)


def _router_call(xb, wrt, br):
    t, d = xb.shape
    tm = min(ROUTE_TM, t)
    return pl.pallas_call(
        functools.partial(_router_kernel, tm=tm),
        grid=(t // tm,),
        in_specs=[pl.BlockSpec((tm, d), lambda i: (i, 0)),
                  pl.BlockSpec((N_EXPERTS, d), lambda i: (0, 0)),
                  pl.BlockSpec((N_EXPERTS, 1), lambda i: (0, 0))],
        out_specs=[pl.BlockSpec((TOP_K, tm), lambda i: (0, i)),
                   pl.BlockSpec((TOP_K, tm), lambda i: (0, i)),
                   pl.BlockSpec((TOP_K, tm), lambda i: (0, i)),
                   pl.BlockSpec((N_EXPERTS, LANES), lambda i: (0, 0))],
        out_shape=[jax.ShapeDtypeStruct((TOP_K, t), I32),
                   jax.ShapeDtypeStruct((TOP_K, t), F32),
                   jax.ShapeDtypeStruct((TOP_K, t), I32),
                   jax.ShapeDtypeStruct((N_EXPERTS, LANES), F32)],
        scratch_shapes=[pltpu.VMEM((N_EXPERTS, LANES), F32)],
        compiler_params=_cparams("arbitrary"),
        name="router",
    )(xb, wrt, br.reshape(N_EXPERTS, 1))


def _dispatch_kernel(pad_lo_ref, pad_hi_ref, dest_ref, x_ref, xs_hbm, zrow, sem, *, tm):
    i = pl.program_id(0)

    @pl.when(i == 0)
    def _():
        zrow[...] = jnp.zeros_like(zrow)

        def per_expert(e, carry):
            lo, hi = pad_lo_ref[e], pad_hi_ref[e]

            def start(p, c):
                pltpu.make_async_copy(zrow, xs_hbm.at[p], sem.at[1]).start()
                return c

            def wait(p, c):
                pltpu.make_async_copy(zrow, xs_hbm.at[0], sem.at[1]).wait()
                return c

            lax.fori_loop(lo, hi, start, 0)
            lax.fori_loop(lo, hi, wait, 0)
            return carry

        lax.fori_loop(0, N_EXPERTS, per_expert, 0)

    for r in range(tm):
        for k in range(TOP_K):
            pltpu.make_async_copy(x_ref.at[r], xs_hbm.at[dest_ref[0, k, r]], sem.at[0]).start(priority=k % 2)
    for k in range(TOP_K):
        pltpu.make_async_copy(x_ref, xs_hbm.at[pl.ds(0, tm)], sem.at[0]).wait()


def _dispatch_call(pad_lo, pad_hi, dest3, xpk, n_rows):
    t = xpk.shape[0]
    tm = dest3.shape[2]
    grid_spec = pltpu.PrefetchScalarGridSpec(
        num_scalar_prefetch=2,
        grid=(t // tm,),
        in_specs=[pl.BlockSpec((1, TOP_K, tm), lambda i, lo, hi: (i, 0, 0), memory_space=pltpu.SMEM),
                  pl.BlockSpec((tm, SLAB, LANES), lambda i, lo, hi: (i, 0, 0))],
        out_specs=pl.BlockSpec(memory_space=pl.ANY),
        scratch_shapes=[pltpu.VMEM((SLAB, LANES), xpk.dtype), pltpu.SemaphoreType.DMA((2,))],
    )
    return pl.pallas_call(
        functools.partial(_dispatch_kernel, tm=tm),
        grid_spec=grid_spec,
        out_shape=jax.ShapeDtypeStruct((n_rows, SLAB, LANES), xpk.dtype),
        compiler_params=_cparams("arbitrary"),
        name="dispatch",
    )(pad_lo, pad_hi, dest3, xpk)


def _expert_kernel(blk_e_ref, nused_ref, x_ref, wgu_ref, wdn_ref, o_ref, wgu_b, wdn_b, *, bm):
    b = pl.program_id(0)

    @pl.when(jnp.logical_or(b == 0, blk_e_ref[b] != blk_e_ref[jnp.maximum(b - 1, 0)]))
    def _():
        wgu_b[...] = wgu_ref[0, 0].astype(BF16)
        wdn_b[...] = wdn_ref[0, 0].astype(BF16)

    @pl.when(b < nused_ref[0])
    def _():
        x = _load_slabs_bf16(x_ref, bm)
        y = None
        for c0 in range(0, D_EXPERT, FFN_SLICE):
            g = jnp.dot(x, wgu_b[:, c0:c0 + FFN_SLICE], preferred_element_type=F32)
            u = jnp.dot(x, wgu_b[:, D_EXPERT + c0:D_EXPERT + c0 + FFN_SLICE], preferred_element_type=F32)
            hmid = (g * jax.nn.sigmoid(g) * u).astype(BF16)
            part = jnp.dot(hmid, wdn_b[c0:c0 + FFN_SLICE, :], preferred_element_type=F32)
            y = part if y is None else y + part
        o_ref[...] = y

    @pl.when(b >= nused_ref[0])
    def _():
        o_ref[...] = jnp.zeros_like(o_ref)


def _expert_call(blk_e, n_used, xs, wgu, wdn, bm, layer):
    n_blocks = xs.shape[0] // (bm * SLAB)
    d = wdn.shape[-1]

    def x_block(b, be, nu):
        return (jnp.minimum(b, jnp.maximum(nu[0] - 1, 0)), 0)

    grid_spec = pltpu.PrefetchScalarGridSpec(
        num_scalar_prefetch=2,
        grid=(n_blocks,),
        in_specs=[pl.BlockSpec((bm * SLAB, LANES), x_block),
                  pl.BlockSpec((1, 1, d, 2 * D_EXPERT), lambda b, be, nu: (layer, be[b], 0, 0)),
                  pl.BlockSpec((1, 1, D_EXPERT, d), lambda b, be, nu: (layer, be[b], 0, 0))],
        out_specs=pl.BlockSpec((bm, d), lambda b, be, nu: (b, 0)),
        scratch_shapes=[pltpu.VMEM((d, 2 * D_EXPERT), BF16), pltpu.VMEM((D_EXPERT, d), BF16)],
    )
    return pl.pallas_call(
        functools.partial(_expert_kernel, bm=bm),
        grid_spec=grid_spec,
        out_shape=jax.ShapeDtypeStruct((n_blocks * bm, d), F32),
        compiler_params=_cparams("arbitrary"),
        name="experts",
    )(blk_e, n_used, xs, wgu, wdn)


def _combine_kernel(dest_ref, dest_next_ref, gate_ref, x_ref, xb_ref, ys_hbm, wsg_ref, wsd_ref, g_ref, b_ref,
                    *rest, tm, n1, nsteps):
    if n1 is None:
        o_ref, ob_ref, buf0, buf1, base_ref, sem = rest
    else:
        o1_ref, o2_ref, buf0, buf1, base_ref, sem = rest
    i = pl.program_id(0)
    bufs = (buf0, buf1)

    def issue(ref, s):
        for r in range(tm):
            for k in range(TOP_K):
                pltpu.make_async_copy(ys_hbm.at[pl.ds(ref[0, k, r], 1)],
                                      bufs[s].at[pl.ds(k * tm + r, 1)], sem.at[s]).start(priority=k % 2)

    def wait(s):
        pltpu.make_async_copy(ys_hbm.at[pl.ds(0, TOP_K * tm)], bufs[s], sem.at[s]).wait()

    @pl.when(i == 0)
    def _():
        issue(dest_ref, 0)

    gu = jnp.dot(xb_ref[...], wsg_ref[...], preferred_element_type=F32)
    g = gu[:, :D_SHARED]
    u = gu[:, D_SHARED:]
    hmid = (g * jax.nn.sigmoid(g) * u).astype(BF16)
    base_ref[...] = ALPHA * x_ref[...] + jnp.dot(hmid, wsd_ref[...], preferred_element_type=F32)

    def step(cur):
        wait(cur)
        issue(dest_next_ref, 1 - cur)
        acc = gate_ref[:, 0:1] * bufs[cur][0:tm, :]
        for k in range(1, TOP_K):
            acc = acc + gate_ref[:, k:k + 1] * bufs[cur][k * tm:(k + 1) * tm, :]
        y = _layer_norm(base_ref[...] + acc, g_ref[...], b_ref[...], LN_EPS)
        if n1 is None:
            o_ref[...] = y
            ob_ref[...] = y.astype(BF16)
        else:
            @pl.when(i < n1)
            def _():
                o1_ref[...] = y

            @pl.when(i >= n1)
            def _():
                o2_ref[...] = y

    for cur in range(2):
        pl.when(lax.rem(i, 2) == cur)(functools.partial(step, cur))

    @pl.when(i == nsteps - 1)
    def _():
        wait(nsteps % 2)


def _combine_call(dest3, gate_tk, x, xb, ys, wsg, wsd, g, b, split_rows=None):
    t, d = x.shape
    tm = dest3.shape[2]
    nsteps = t // tm
    if split_rows is None:
        n1 = None
        out_specs = [pl.BlockSpec((tm, d), lambda i: (i, 0)), pl.BlockSpec((tm, d), lambda i: (i, 0))]
        out_shape = [jax.ShapeDtypeStruct((t, d), F32), jax.ShapeDtypeStruct((t, d), BF16)]
    else:
        n1 = split_rows // tm
        out_specs = [pl.BlockSpec((tm, d), lambda i: (jnp.minimum(i, n1 - 1), 0)),
                     pl.BlockSpec((tm, d), lambda i: (jnp.maximum(i - n1, 0), 0))]
        out_shape = [jax.ShapeDtypeStruct((split_rows, d), F32), jax.ShapeDtypeStruct((t - split_rows, d), F32)]
    return pl.pallas_call(
        functools.partial(_combine_kernel, tm=tm, n1=n1, nsteps=nsteps),
        grid=(nsteps,),
        in_specs=[pl.BlockSpec((1, TOP_K, tm), lambda i: (i, 0, 0), memory_space=pltpu.SMEM),
                  pl.BlockSpec((1, TOP_K, tm), lambda i: (jnp.minimum(i + 1, nsteps - 1), 0, 0),
                               memory_space=pltpu.SMEM),
                  pl.BlockSpec((tm, TOP_K), lambda i: (i, 0)),
                  pl.BlockSpec((tm, d), lambda i: (i, 0)),
                  pl.BlockSpec((tm, d), lambda i: (i, 0)),
                  pl.BlockSpec(memory_space=pl.ANY),
                  pl.BlockSpec((d, 2 * D_SHARED), lambda i: (0, 0)),
                  pl.BlockSpec((D_SHARED, d), lambda i: (0, 0)),
                  pl.BlockSpec((1, d), lambda i: (0, 0)),
                  pl.BlockSpec((1, d), lambda i: (0, 0))],
        out_specs=out_specs,
        out_shape=out_shape,
        scratch_shapes=[pltpu.VMEM((TOP_K * tm, d), F32), pltpu.VMEM((TOP_K * tm, d), F32),
                        pltpu.VMEM((tm, d), F32), pltpu.SemaphoreType.DMA((2,))],
        compiler_params=_cparams("arbitrary"),
        name="combine_ln",
    )(dest3, dest3, gate_tk, x, xb, ys, wsg, wsd, g.reshape(1, d), b.reshape(1, d))


def _dispatch_tables(ids, rank, counts_f, bm):
    k, t = ids.shape
    tk = k * t
    n_blocks = -(-tk // bm) + N_EXPERTS
    counts = counts_f[:, 0].astype(I32)
    padded = (counts + bm - 1) // bm * bm
    pends = jnp.cumsum(padded)
    pstart = pends - padded
    experts = jnp.arange(N_EXPERTS, dtype=I32)
    dest = jnp.sum(jnp.where(ids[:, :, None] == experts, pstart, 0), axis=-1) + rank
    blk_first = jnp.arange(n_blocks, dtype=I32) * bm
    blk_e = jnp.minimum(jnp.sum(pends[None, :] <= blk_first[:, None], axis=1), N_EXPERTS - 1).astype(I32)
    n_used = (pends[-1:] // bm).astype(I32)
    return dest, blk_e, n_used, (pstart + counts).astype(I32), pends.astype(I32), n_blocks * bm


def _layer(x, xb, memb, seq, layer, w_in, b_gate, conv_w, conv_b, mh_norm_g, pool_w, pool_scale, w_mem_kv, w_out,
           ln1_g, ln1_b, w_router, b_router, w_gu_all, w_dn_all, w_sh_gu, w_sh_dn, ln2_g, ln2_b, last):
    t = x.shape[0]
    hin = _mm_call(xb, w_in[:, :OFF_G].astype(BF16), F32, "in_proj")
    g_rows, g_cols = _gates_call(xb, _gate_rows(w_in[:, OFF_G:].T).astype(BF16), _gate_rows(b_gate))
    gr3 = g_rows.reshape(3 * GDIR, t // CHUNK, CHUNK).transpose(1, 0, 2)
    q = _conv_call(hin, conv_w, conv_b, seq, col0=OFF_Q, scale=1.0, transpose_out=False)
    kt = _conv_call(hin, conv_w, conv_b, seq, col0=OFF_K, scale=M_HEAD_DIM ** -0.5, transpose_out=True)
    h_f = _mlstm_call(q, kt, hin, gr3, g_cols, seq, reverse=False)
    h_b = _mlstm_call(q, kt, hin, gr3, g_cols, seq, reverse=True)
    m_out = _headnorm_call(h_f, h_b, hin, mh_norm_g)
    kv = _mm_call(memb, w_mem_kv.astype(BF16), BF16, "mem_kv").reshape(seq.nseq, -1, 2 * X_WIDTH)
    pa_out = _pool_attn_call(hin, kv, pool_w.astype(BF16), pool_scale, seq)
    x1, x1b, x1p = _outproj_call(m_out, pa_out, x, w_out.astype(BF16), ln1_g, ln1_b)
    ids, gate, rank, counts = _router_call(x1b, w_router.T.astype(BF16), b_router)
    bm = MOE_BM
    dest, blk_e, n_used, pad_lo, pad_hi, n_rows = _dispatch_tables(ids, rank, counts, bm)
    ctm = min(COMB_TM, t)
    dest3 = dest.reshape(TOP_K, t // ctm, ctm).transpose(1, 0, 2)
    dtm = min(DISP_TM, t)
    dest3d = dest.reshape(TOP_K, t // dtm, dtm).transpose(1, 0, 2)
    xs = _dispatch_call(pad_lo, pad_hi, dest3d, x1p.reshape(t, SLAB, LANES), n_rows)
    ys = _expert_call(blk_e, n_used, xs.reshape(n_rows * SLAB, LANES), w_gu_all, w_dn_all, bm, layer)
    return _combine_call(dest3, gate.T, x1, x1b, ys, w_sh_gu.astype(BF16), w_sh_dn.astype(BF16), ln2_g, ln2_b,
                         split_rows=seq.t1 if last else None)


def kernel(x_prompt, x_sample, mem_prompt, mem_sample, ln_in_g, ln_in_b, w_in, b_gate, conv_w, conv_b, mh_norm_g, pool_w, pool_scale, w_mem_kv, w_out, ln1_g, ln1_b, w_router, b_router, w_gu, w_dn, w_sh_gu, w_sh_dn, ln2_g, ln2_b):
    b1, s1, d = x_prompt.shape
    b2, s2, _ = x_sample.shape
    seq = SeqInfo(b1, s1, b2, s2)
    memb = jnp.concatenate([mem_prompt.reshape(-1, d), mem_sample.reshape(-1, d)], axis=0).astype(BF16)
    x, xb = _ln_call(x_prompt.reshape(b1 * s1, d), x_sample.reshape(b2 * s2, d), ln_in_g, ln_in_b)
    depth = w_in.shape[0]
    for l in range(depth):
        x, xb = _layer(x, xb, memb, seq, l, w_in[l], b_gate[l], conv_w[l], conv_b[l], mh_norm_g[l], pool_w[l],
                       pool_scale[l], w_mem_kv[l], w_out[l], ln1_g[l], ln1_b[l], w_router[l], b_router[l],
                       w_gu, w_dn, w_sh_gu[l], w_sh_dn[l], ln2_g[l], ln2_b[l], last=l == depth - 1)
    return (x.reshape(b1, s1, d), xb.reshape(b2, s2, d))
```

```python
import functools

import jax
import jax.numpy as jnp
from jax import lax
from jax.experimental import pallas as pl
from jax.experimental.pallas import tpu as pltpu

F32 = jnp.float32
BF16 = jnp.bfloat16
I32 = jnp.int32

DEPTH = 4
D_MODEL = 2048
M_HEADS = 4
M_WIDTH = 1024
M_HEAD_DIM = 256
CHUNK = 128
CONV_K = 5
POOL_WINDOWS = (2, 4, 8, 16)
POOL_WIDTH = 512
POOL_GROUP = 128
X_HEADS = 4
X_WIDTH = 512
X_HEAD_DIM = 128
OFF_Q = 0
OFF_K = OFF_Q + M_WIDTH
OFF_V = OFF_K + M_WIDTH
OFF_O = OFF_V + M_WIDTH
OFF_P = OFF_O + M_WIDTH
OFF_XQ = OFF_P + POOL_WIDTH
OFF_G = OFF_XQ + X_WIDTH
N_GATES = 4 * M_HEADS
N_EXPERTS = 64
TOP_K = 8
N_GROUPS = 8
GROUP_SIZE = N_EXPERTS // N_GROUPS
TOPK_GROUPS = 4
D_EXPERT = 512
D_SHARED = 512
ROUTED_SCALE = 2.5
ALPHA = (2 * DEPTH) ** 0.25
LN_EPS = 1e-5
HN_EPS = 1e-6

LANES = 128
SUBLANES = 8
HALO = SUBLANES
VMEM_LIMIT = 58 * 1024 * 1024

LN_TM = 512
MM_TM = 512
MM_TN = 1024
SEQ_TM = 512
MLSTM_CB = 4
OUT_TM = 512
OUT_SUB = 256
ROUTE_TM = 512
MOE_BM = 512
DISP_TM = 256
COMB_TM = 128


def _cparams(*sem):
    return pltpu.CompilerParams(dimension_semantics=sem, vmem_limit_bytes=VMEM_LIMIT)


class SeqInfo:
    def __init__(self, b1, s1, b2, s2):
        self.b1, self.s1, self.b2, self.s2 = b1, s1, b2, s2
        self.t1 = b1 * s1
        self.t = b1 * s1 + b2 * s2
        self.nseq = b1 + b2

    def pos_len(self, row0):
        in1 = row0 < self.t1
        pos = jnp.where(in1, lax.rem(row0, self.s1), lax.rem(row0 - self.t1, self.s2))
        slen = jnp.where(in1, self.s1, self.s2)
        return pos, slen

    def seq_id(self, row0):
        in1 = row0 < self.t1
        return jnp.where(in1, row0 // self.s1, self.b1 + (row0 - self.t1) // self.s2)


def _layer_norm(y, g, b, eps):
    mu = jnp.mean(y, axis=-1, keepdims=True)
    d = y - mu
    var = jnp.mean(d * d, axis=-1, keepdims=True)
    return d * lax.rsqrt(var + eps) * g + b


def _ln_kernel(x1_ref, x2_ref, g_ref, b_ref, o_ref, ob_ref, *, n1):
    x = jnp.where(pl.program_id(0) < n1, x1_ref[...], x2_ref[...])
    y = _layer_norm(x, g_ref[...], b_ref[...], LN_EPS)
    o_ref[...] = y
    ob_ref[...] = y.astype(BF16)


def _ln_call(x1, x2, g, b):
    t1, d = x1.shape
    t2 = x2.shape[0]
    tm = min(LN_TM, t1, t2)
    n1, n2 = t1 // tm, t2 // tm
    t = t1 + t2
    return pl.pallas_call(
        functools.partial(_ln_kernel, n1=n1),
        grid=(n1 + n2,),
        in_specs=[pl.BlockSpec((tm, d), lambda i: (jnp.minimum(i, n1 - 1), 0)),
                  pl.BlockSpec((tm, d), lambda i: (jnp.maximum(i - n1, 0), 0)),
                  pl.BlockSpec((1, d), lambda i: (0, 0)),
                  pl.BlockSpec((1, d), lambda i: (0, 0))],
        out_specs=[pl.BlockSpec((tm, d), lambda i: (i, 0)),
                   pl.BlockSpec((tm, d), lambda i: (i, 0))],
        out_shape=[jax.ShapeDtypeStruct((t, d), F32), jax.ShapeDtypeStruct((t, d), BF16)],
        compiler_params=_cparams("arbitrary"),
        name="ln_in",
    )(x1, x2, g.reshape(1, d), b.reshape(1, d))


def _mm_kernel(x_ref, w_ref, o_ref):
    o_ref[...] = jnp.dot(x_ref[...], w_ref[...], preferred_element_type=F32).astype(o_ref.dtype)


def _mm_call(xb, wb, out_dtype, name):
    m, k = xb.shape
    n = wb.shape[1]
    tm = min(MM_TM, m)
    while m % tm:
        tm //= 2
    tn = min(MM_TN, n)
    return pl.pallas_call(
        _mm_kernel,
        grid=(n // tn, m // tm),
        in_specs=[pl.BlockSpec((tm, k), lambda j, i: (i, 0)),
                  pl.BlockSpec((k, tn), lambda j, i: (0, j))],
        out_specs=pl.BlockSpec((tm, tn), lambda j, i: (i, j)),
        out_shape=jax.ShapeDtypeStruct((m, n), out_dtype),
        compiler_params=_cparams("parallel", "parallel"),
        name=name,
    )(xb, wb)


GDIR = 2 * M_HEADS


def _chunk_scan(x, op, reverse):
    n = x.shape[1]
    pos = lax.rem(lax.broadcasted_iota(I32, x.shape, 1), CHUNK)
    s = 1
    while s < CHUNK:
        if reverse:
            x = jnp.where(pos < CHUNK - s, op(x, pltpu.roll(x, n - s, axis=1)), x)
        else:
            x = jnp.where(pos >= s, op(x, pltpu.roll(x, s, axis=1)), x)
        s *= 2
    return x


def _gates_kernel(x_ref, w_ref, b_ref, row_ref, col_ref):
    g = lax.dot_general(w_ref[...], x_ref[...], (((1,), (1,)), ((), ())),
                        preferred_element_type=F32) + b_ref[...]
    gi = g[0:GDIR]
    logf = jax.nn.log_sigmoid(g[GDIR:2 * GDIR])
    is_fwd = lax.broadcasted_iota(I32, gi.shape, 0) < M_HEADS
    pre = _chunk_scan(logf, jnp.add, False)
    suf = _chunk_scan(logf, jnp.add, True)
    b = jnp.where(is_fwd, pre, suf)
    total = pre + suf - logf
    ib = gi - b
    cmax = jnp.where(is_fwd, _chunk_scan(ib, jnp.maximum, False), _chunk_scan(ib, jnp.maximum, True))
    lw = total + ib
    lw_max = jnp.maximum(_chunk_scan(lw, jnp.maximum, False), _chunk_scan(lw, jnp.maximum, True))
    row_ref[...] = jnp.concatenate([ib, total, lw_max], axis=0)
    pad = jnp.zeros((LANES - 3 * GDIR, gi.shape[1]), F32)
    col_ref[...] = jnp.concatenate([b, cmax, lw, pad], axis=0).T


def _gates_call(xb, wgt, bg):
    t, d = xb.shape
    tm = min(MM_TM, t)
    return pl.pallas_call(
        _gates_kernel,
        grid=(t // tm,),
        in_specs=[pl.BlockSpec((tm, d), lambda i: (i, 0)),
                  pl.BlockSpec((N_GATES, d), lambda i: (0, 0)),
                  pl.BlockSpec((N_GATES, 1), lambda i: (0, 0))],
        out_specs=[pl.BlockSpec((3 * GDIR, tm), lambda i: (0, i)),
                   pl.BlockSpec((tm, LANES), lambda i: (i, 0))],
        out_shape=[jax.ShapeDtypeStruct((3 * GDIR, t), F32), jax.ShapeDtypeStruct((t, LANES), F32)],
        compiler_params=_cparams("parallel"),
        name="gates",
    )(xb, wgt, bg.reshape(N_GATES, 1))


def _gate_rows(a):
    h = M_HEADS
    return jnp.concatenate([a[0:h], a[2 * h:3 * h], a[h:2 * h], a[3 * h:4 * h]], axis=0)


def _halo_specs(tm, width, col_block, n_rows):
    per = tm // HALO
    last = n_rows // HALO - 1
    prev = pl.BlockSpec((HALO, width), lambda i, *_: (jnp.maximum(i * per - 1, 0), col_block))
    cur = pl.BlockSpec((tm, width), lambda i, *_: (i, col_block))
    nxt = pl.BlockSpec((HALO, width), lambda i, *_: (jnp.minimum((i + 1) * per, last), col_block))
    return prev, cur, nxt


def _extended_tile(prev_ref, cur_ref, next_ref, tm, seq):
    row0 = pl.program_id(0) * tm
    pos, slen = seq.pos_len(row0)
    prev = jnp.where(pos == 0, 0.0, prev_ref[...])
    nxt = jnp.where(pos + tm == slen, 0.0, next_ref[...])
    return jnp.concatenate([prev, cur_ref[...], nxt], axis=0), pos, slen


def _shift_rows(xe, d):
    n = xe.shape[0]
    if d == 0:
        return xe
    return pltpu.roll(xe, (n - d) % n, axis=0)


def _conv_kernel(prev_ref, cur_ref, next_ref, w_ref, b_ref, o_ref, *, tm, tc, seq, scale, transpose_out):
    xe, _, _ = _extended_tile(prev_ref, cur_ref, next_ref, tm, seq)
    acc = jnp.broadcast_to(b_ref[...], (tm, tc))
    for j in range(CONV_K):
        sh = _shift_rows(xe, j - CONV_K // 2)
        acc = acc + sh[HALO:HALO + tm] * w_ref[j:j + 1, :]
    y = acc * jax.nn.sigmoid(acc) * scale
    o_ref[...] = (y.T if transpose_out else y).astype(BF16)


def _conv_call(hin, conv_w, conv_b, seq, *, col0, scale, transpose_out):
    t = hin.shape[0]
    tm = min(SEQ_TM, seq.s1, seq.s2)
    tc = 512
    j0 = col0 // tc
    prev = pl.BlockSpec((HALO, tc), lambda i, j: (jnp.maximum(i * (tm // HALO) - 1, 0), j + j0))
    cur = pl.BlockSpec((tm, tc), lambda i, j: (i, j + j0))
    nxt = pl.BlockSpec((HALO, tc), lambda i, j: (jnp.minimum((i + 1) * (tm // HALO), t // HALO - 1), j + j0))
    if transpose_out:
        out_spec = pl.BlockSpec((tc, tm), lambda i, j: (j, i))
        out_shape = jax.ShapeDtypeStruct((M_WIDTH, t), BF16)
    else:
        out_spec = pl.BlockSpec((tm, tc), lambda i, j: (i, j))
        out_shape = jax.ShapeDtypeStruct((t, M_WIDTH), BF16)
    return pl.pallas_call(
        functools.partial(_conv_kernel, tm=tm, tc=tc, seq=seq, scale=scale, transpose_out=transpose_out),
        grid=(t // tm, M_WIDTH // tc),
        in_specs=[prev, cur, nxt,
                  pl.BlockSpec((CONV_K, tc), lambda i, j: (0, j + j0)),
                  pl.BlockSpec((1, tc), lambda i, j: (0, j + j0))],
        out_specs=out_spec,
        out_shape=out_shape,
        compiler_params=_cparams("parallel", "parallel"),
        name="conv_silu_t" if transpose_out else "conv_silu",
    )(hin, hin, hin, conv_w, conv_b.reshape(1, 2 * M_WIDTH))


def _mlstm_kernel(q_ref, kt_ref, v_ref, gr_ref, gc_ref, o_ref, c_ref, m_ref, *, reverse, cb, seq):
    L, DK, DV = CHUNK, M_HEAD_DIM, M_HEAD_DIM
    step = pl.program_id(0)
    nsteps = pl.num_programs(0)
    blk = (nsteps - 1 - step) if reverse else step
    ti = lax.broadcasted_iota(I32, (L, L), 0)
    si = lax.broadcasted_iota(I32, (L, L), 1)
    causal = (si >= ti) if reverse else (si <= ti)
    slot0 = M_HEADS if reverse else 0

    @pl.when(step == 0)
    def _():
        c_ref[...] = jnp.zeros_like(c_ref)
        m_ref[...] = jnp.zeros_like(m_ref)

    m_vals = [m_ref[h, 0:1, 0:1] for h in range(M_HEADS)]
    for jj in range(cb):
        j = (cb - 1 - jj) if reverse else jj
        rows = slice(j * L, (j + 1) * L)
        pos, slen = seq.pos_len((blk * cb + j) * L)
        at_start = (pos + L == slen) if reverse else (pos == 0)
        grow = gr_ref[j]
        ones = jnp.ones((L, LANES), F32)
        for h in range(M_HEADS):
            cols = slice(h * DK, (h + 1) * DK)
            sl = slot0 + h
            q = q_ref[rows, cols]
            kt = kt_ref[cols, rows]
            v32 = v_ref[rows, cols]
            ib_row = grow[sl:sl + 1, :]
            g_tot = grow[GDIR + sl:GDIR + sl + 1, 0:1]
            lw_max = grow[2 * GDIR + sl:2 * GDIR + sl + 1, 0:1]
            b_col = gc_ref[rows, sl:sl + 1]
            cmax_col = gc_ref[rows, GDIR + sl:GDIR + sl + 1]
            lw_col = gc_ref[rows, 2 * GDIR + sl:2 * GDIR + sl + 1]
            m_prev = jnp.where(at_start, 0.0, m_vals[h])
            m_rel = jnp.maximum(m_prev, cmax_col)
            m_t = b_col + m_rel
            dexp = jnp.exp(jnp.where(causal, ib_row - m_rel, -jnp.inf))
            s = jnp.dot(q, kt, preferred_element_type=F32)
            a = (s * dexp).astype(BF16)
            inter = jnp.exp(m_prev - m_rel)
            c_aug = jnp.where(at_start, 0.0, c_ref[h])
            qc = jnp.dot(q, c_aug.astype(BF16), preferred_element_type=F32)
            v_ext = jnp.concatenate([v32, ones], axis=1).astype(BF16)
            nd = jnp.dot(a, v_ext, preferred_element_type=F32) + inter * qc
            den = nd[:, DV:DV + 1]
            hh = nd[:, :DV] / jnp.maximum(jnp.abs(den), jnp.exp(-m_t))
            m_new = jnp.maximum(g_tot + m_prev, lw_max)
            wk_col = jnp.exp(lw_col - m_new)
            decay = jnp.exp(g_tot + m_prev - m_new)
            v_aug = (jnp.concatenate([v32, ones], axis=1) * wk_col).astype(BF16)
            upd = jnp.dot(kt, v_aug, preferred_element_type=F32)
            c_ref[h] = decay * c_aug + upd
            m_vals[h] = m_new
            o_ref[rows, cols] = hh
    for h in range(M_HEADS):
        m_ref[h] = jnp.broadcast_to(m_vals[h], (SUBLANES, LANES))


def _mlstm_call(q, kt, hin, gr3, gc, seq, *, reverse):
    t = q.shape[0]
    cb = MLSTM_CB
    while (seq.s1 // CHUNK) % cb or (seq.s2 // CHUNK) % cb:
        cb //= 2
    rows = cb * CHUNK
    nsteps = t // rows
    blk = (lambda s: nsteps - 1 - s) if reverse else (lambda s: s)
    vblk = OFF_V // M_WIDTH
    return pl.pallas_call(
        functools.partial(_mlstm_kernel, reverse=reverse, cb=cb, seq=seq),
        grid=(nsteps,),
        in_specs=[pl.BlockSpec((rows, M_WIDTH), lambda s: (blk(s), 0)),
                  pl.BlockSpec((M_WIDTH, rows), lambda s: (0, blk(s))),
                  pl.BlockSpec((rows, M_WIDTH), lambda s: (blk(s), vblk)),
                  pl.BlockSpec((cb, 3 * GDIR, CHUNK), lambda s: (blk(s), 0, 0)),
                  pl.BlockSpec((rows, LANES), lambda s: (blk(s), 0))],
        out_specs=pl.BlockSpec((rows, M_WIDTH), lambda s: (blk(s), 0)),
        out_shape=jax.ShapeDtypeStruct((t, M_WIDTH), F32),
        scratch_shapes=[pltpu.VMEM((M_HEADS, M_HEAD_DIM, M_HEAD_DIM + LANES), F32),
                        pltpu.VMEM((M_HEADS, SUBLANES, LANES), F32)],
        compiler_params=_cparams("arbitrary"),
        name="mlstm_bwd" if reverse else "mlstm_fwd",
    )(q, kt, hin, gr3, gc)


def _headnorm_kernel(hf_ref, hb_ref, uo_ref, ng_ref, o_ref):
    for h in range(M_HEADS):
        cols = slice(h * M_HEAD_DIM, (h + 1) * M_HEAD_DIM)
        hs = hf_ref[:, cols] + hb_ref[:, cols]
        mu = jnp.mean(hs, axis=1, keepdims=True)
        d = hs - mu
        var = jnp.mean(d * d, axis=1, keepdims=True)
        hn = d * lax.rsqrt(var + HN_EPS) * ng_ref[:, cols]
        o_ref[:, cols] = (hn * jax.nn.sigmoid(uo_ref[:, cols])).astype(o_ref.dtype)


def _headnorm_call(h_f, h_b, hin, norm_g):
    t = h_f.shape[0]
    tm = min(LN_TM, t)
    return pl.pallas_call(
        _headnorm_kernel,
        grid=(t // tm,),
        in_specs=[pl.BlockSpec((tm, M_WIDTH), lambda i: (i, 0)),
                  pl.BlockSpec((tm, M_WIDTH), lambda i: (i, 0)),
                  pl.BlockSpec((tm, M_WIDTH), lambda i: (i, OFF_O // M_WIDTH)),
                  pl.BlockSpec((1, M_WIDTH), lambda i: (0, 0))],
        out_specs=pl.BlockSpec((tm, M_WIDTH), lambda i: (i, 0)),
        out_shape=jax.ShapeDtypeStruct((t, M_WIDTH), BF16),
        compiler_params=_cparams("parallel"),
        name="headnorm",
    )(h_f, h_b, hin, norm_g.reshape(1, M_WIDTH))


def _pool_attn_kernel(prev_ref, cur_ref, next_ref, xq_ref, kv_ref, pw_ref, ps_ref, o_ref, *, tm, seq):
    xe, pos0, slen = _extended_tile(prev_ref, cur_ref, next_ref, tm, seq)
    pos = pos0 + lax.broadcasted_iota(I32, (tm, 1), 0)
    for gi, w in enumerate(POOL_WINDOWS):
        cols = slice(gi * POOL_GROUP, (gi + 1) * POOL_GROUP)
        u = xe[:, cols]
        win = _shift_rows(u, -1) + u
        half = 1
        while 2 * half < w:
            win = _shift_rows(win, -half) + _shift_rows(win, half)
            half *= 2
        cnt = (jnp.minimum(pos + w // 2, slen) - jnp.maximum(pos - w // 2, 0)).astype(F32)
        mean = win[HALO:HALO + tm] / cnt
        diff = (mean - u[HALO:HALO + tm]).astype(BF16)
        y = jnp.dot(diff, pw_ref[gi], preferred_element_type=F32)
        o_ref[:, cols] = (y * ps_ref[:, cols]).astype(o_ref.dtype)
    xq = xq_ref[...].astype(BF16)
    for h in range(X_HEADS):
        cols = slice(h * X_HEAD_DIM, (h + 1) * X_HEAD_DIM)
        kh = kv_ref[0, :, h * X_HEAD_DIM:(h + 1) * X_HEAD_DIM]
        vh = kv_ref[0, :, X_WIDTH + h * X_HEAD_DIM:X_WIDTH + (h + 1) * X_HEAD_DIM]
        s = lax.dot_general(xq[:, cols], kh, (((1,), (1,)), ((), ())),
                            preferred_element_type=F32) * (X_HEAD_DIM ** -0.5)
        e = jnp.exp(s - jnp.max(s, axis=1, keepdims=True))
        p = e / jnp.sum(e, axis=1, keepdims=True)
        y = jnp.dot(p.astype(BF16), vh, preferred_element_type=F32)
        o_ref[:, POOL_WIDTH + h * X_HEAD_DIM:POOL_WIDTH + (h + 1) * X_HEAD_DIM] = y.astype(o_ref.dtype)


def _pool_attn_call(hin, kv, pool_w, pool_scale, seq):
    t = hin.shape[0]
    tm = min(SEQ_TM, seq.s1, seq.s2)
    n_mem = kv.shape[1]
    prev, cur, nxt = _halo_specs(tm, POOL_WIDTH, OFF_P // POOL_WIDTH, t)
    return pl.pallas_call(
        functools.partial(_pool_attn_kernel, tm=tm, seq=seq),
        grid=(t // tm,),
        in_specs=[prev, cur, nxt,
                  pl.BlockSpec((tm, X_WIDTH), lambda i: (i, OFF_XQ // X_WIDTH)),
                  pl.BlockSpec((1, n_mem, 2 * X_WIDTH), lambda i: (seq.seq_id(i * tm), 0, 0)),
                  pl.BlockSpec((len(POOL_WINDOWS), POOL_GROUP, POOL_GROUP), lambda i: (0, 0, 0)),
                  pl.BlockSpec((1, POOL_WIDTH), lambda i: (0, 0))],
        out_specs=pl.BlockSpec((tm, POOL_WIDTH + X_WIDTH), lambda i: (i, 0)),
        out_shape=jax.ShapeDtypeStruct((t, POOL_WIDTH + X_WIDTH), BF16),
        compiler_params=_cparams("parallel"),
        name="pool_attn",
    )(hin, hin, hin, hin, kv, pool_w, pool_scale.reshape(1, POOL_WIDTH))


def _pack_halves(yb):
    n = yb.shape[1] // 2
    bits = pltpu.bitcast(yb.astype(F32), jnp.uint32)
    return jnp.bitwise_or(bits[:, n:], jnp.right_shift(bits[:, :n], jnp.uint32(16)))


SLAB = SUBLANES


def _store_slabs(o_ref, packed, row0=0):
    rows = packed.shape[0]
    for s in range(SLAB):
        o_ref[pl.ds(row0 * SLAB + s, rows, stride=SLAB), :] = packed[:, s * LANES:(s + 1) * LANES]


def _load_slabs_bf16(x_ref, rows):
    words = [x_ref[pl.ds(s, rows, stride=SLAB), :] for s in range(SLAB)]
    lo = [pltpu.bitcast(jnp.left_shift(w, jnp.uint32(16)), F32) for w in words]
    hi = [pltpu.bitcast(jnp.bitwise_and(w, jnp.uint32(0xFFFF0000)), F32) for w in words]
    return jnp.concatenate(lo + hi, axis=1).astype(BF16)


def _outproj_kernel(m_ref, pa_ref, x_ref, w_ref, g_ref, b_ref, o_ref, ob_ref, op_ref, *, tm, sub):
    for r0 in range(0, tm, sub):
        rows = slice(r0, r0 + sub)
        acc = jnp.dot(m_ref[rows, :], w_ref[0:M_WIDTH, :], preferred_element_type=F32)
        acc = acc + jnp.dot(pa_ref[rows, :], w_ref[M_WIDTH:, :], preferred_element_type=F32)
        y = _layer_norm(ALPHA * x_ref[rows, :] + acc, g_ref[...], b_ref[...], LN_EPS)
        o_ref[rows, :] = y
        yb = y.astype(BF16)
        ob_ref[rows, :] = yb
        _store_slabs(op_ref, _pack_halves(yb), r0)


def _outproj_call(m_out, pa_out, x, w_out, g, b):
    t, d = x.shape
    tm = min(OUT_TM, t)
    return pl.pallas_call(
        functools.partial(_outproj_kernel, tm=tm, sub=min(OUT_SUB, tm)),
        grid=(t // tm,),
        in_specs=[pl.BlockSpec((tm, M_WIDTH), lambda i: (i, 0)),
                  pl.BlockSpec((tm, POOL_WIDTH + X_WIDTH), lambda i: (i, 0)),
                  pl.BlockSpec((tm, d), lambda i: (i, 0)),
                  pl.BlockSpec((d, d), lambda i: (0, 0)),
                  pl.BlockSpec((1, d), lambda i: (0, 0)),
                  pl.BlockSpec((1, d), lambda i: (0, 0))],
        out_specs=[pl.BlockSpec((tm, d), lambda i: (i, 0)),
                   pl.BlockSpec((tm, d), lambda i: (i, 0)),
                   pl.BlockSpec((tm * SLAB, LANES), lambda i: (i, 0))],
        out_shape=[jax.ShapeDtypeStruct((t, d), F32), jax.ShapeDtypeStruct((t, d), BF16),
                   jax.ShapeDtypeStruct((t * SLAB, LANES), jnp.uint32)],
        compiler_params=_cparams("parallel"),
        name="outproj_ln",
    )(m_out, pa_out, x, w_out, g.reshape(1, d), b.reshape(1, d))


def _first_index_of_max(vals, idx, axis_size):
    mx = jnp.max(vals, axis=0, keepdims=True)
    first = jnp.min(jnp.where(vals == mx, idx, axis_size), axis=0, keepdims=True)
    return mx, first


def _router_kernel(x_ref, w_ref, b_ref, ids_ref, gate_ref, rank_ref, cnt_ref, carry_ref, *, tm):
    @pl.when(pl.program_id(0) == 0)
    def _():
        carry_ref[...] = jnp.zeros_like(carry_ref)

    logits = lax.dot_general(w_ref[...], x_ref[...], (((1,), (1,)), ((), ())),
                             preferred_element_type=F32)
    s = jax.nn.sigmoid(logits)
    sel = s + b_ref[...]
    neg = -jnp.inf
    sub = lax.broadcasted_iota(I32, (GROUP_SIZE, tm), 0)
    gscores = []
    for g in range(N_GROUPS):
        blk = sel[g * GROUP_SIZE:(g + 1) * GROUP_SIZE, :]
        m1, i1 = _first_index_of_max(blk, sub, GROUP_SIZE)
        m2 = jnp.max(jnp.where(sub == i1, neg, blk), axis=0, keepdims=True)
        gscores.append(m1 + m2)
    gs = jnp.concatenate(gscores, axis=0)
    gidx = lax.broadcasted_iota(I32, (N_GROUPS, tm), 0)
    gmask = jnp.zeros((N_GROUPS, tm), jnp.bool_)
    for _ in range(TOPK_GROUPS):
        _, gi = _first_index_of_max(gs, gidx, N_GROUPS)
        hit = gidx == gi
        gmask = jnp.logical_or(gmask, hit)
        gs = jnp.where(hit, neg, gs)
    gmask_f = gmask.astype(F32)
    masked = jnp.concatenate(
        [jnp.where(gmask_f[g:g + 1, :] > 0.5, sel[g * GROUP_SIZE:(g + 1) * GROUP_SIZE, :], neg)
         for g in range(N_GROUPS)], axis=0)
    eidx = lax.broadcasted_iota(I32, (N_EXPERTS, tm), 0)
    chosen = jnp.zeros((N_EXPERTS, tm), F32)
    ids, gates, hits = [], [], []
    for _ in range(TOP_K):
        _, ei = _first_index_of_max(masked, eidx, N_EXPERTS)
        hit = eidx == ei
        ids.append(ei)
        gates.append(jnp.sum(jnp.where(hit, s, 0.0), axis=0, keepdims=True))
        hits.append(hit)
        chosen = jnp.where(hit, 1.0, chosen)
        masked = jnp.where(hit, neg, masked)
    gate = jnp.concatenate(gates, axis=0)
    gate = gate / jnp.sum(gate, axis=0, keepdims=True) * ROUTED_SCALE
    ids_ref[...] = jnp.concatenate(ids, axis=0)
    gate_ref[...] = gate
    r = lax.broadcasted_iota(I32, (tm, tm), 0)
    c = lax.broadcasted_iota(I32, (tm, tm), 1)
    before = (r < c).astype(BF16)
    cnt = jnp.dot(chosen.astype(BF16), before, preferred_element_type=F32) + carry_ref[:, 0:1]
    ranks = [jnp.sum(jnp.where(hit, cnt, 0.0), axis=0, keepdims=True) for hit in hits]
    rank_ref[...] = jnp.concatenate(ranks, axis=0).astype(I32)
    total = carry_ref[:, 0:1] + jnp.sum(chosen, axis=1, keepdims=True)
    carry_ref[...] = jnp.broadcast_to(total, carry_ref.shape)
    cnt_ref[...] = jnp.broadcast_to(total, cnt_ref.shape)


def _router_call(xb, wrt, br):
    t, d = xb.shape
    tm = min(ROUTE_TM, t)
    return pl.pallas_call(
        functools.partial(_router_kernel, tm=tm),
        grid=(t // tm,),
        in_specs=[pl.BlockSpec((tm, d), lambda i: (i, 0)),
                  pl.BlockSpec((N_EXPERTS, d), lambda i: (0, 0)),
                  pl.BlockSpec((N_EXPERTS, 1), lambda i: (0, 0))],
        out_specs=[pl.BlockSpec((TOP_K, tm), lambda i: (0, i)),
                   pl.BlockSpec((TOP_K, tm), lambda i: (0, i)),
                   pl.BlockSpec((TOP_K, tm), lambda i: (0, i)),
                   pl.BlockSpec((N_EXPERTS, LANES), lambda i: (0, 0))],
        out_shape=[jax.ShapeDtypeStruct((TOP_K, t), I32),
                   jax.ShapeDtypeStruct((TOP_K, t), F32),
                   jax.ShapeDtypeStruct((TOP_K, t), I32),
                   jax.ShapeDtypeStruct((N_EXPERTS, LANES), F32)],
        scratch_shapes=[pltpu.VMEM((N_EXPERTS, LANES), F32)],
        compiler_params=_cparams("arbitrary"),
        name="router",
    )(xb, wrt, br.reshape(N_EXPERTS, 1))


def _dispatch_kernel(pad_lo_ref, pad_hi_ref, dest_ref, x_ref, xs_hbm, zrow, sem, *, tm):
    i = pl.program_id(0)

    @pl.when(i == 0)
    def _():
        zrow[...] = jnp.zeros_like(zrow)

        def per_expert(e, carry):
            lo, hi = pad_lo_ref[e], pad_hi_ref[e]

            def start(p, c):
                pltpu.make_async_copy(zrow, xs_hbm.at[p], sem.at[1]).start()
                return c

            def wait(p, c):
                pltpu.make_async_copy(zrow, xs_hbm.at[0], sem.at[1]).wait()
                return c

            lax.fori_loop(lo, hi, start, 0)
            lax.fori_loop(lo, hi, wait, 0)
            return carry

        lax.fori_loop(0, N_EXPERTS, per_expert, 0)

    for r in range(tm):
        for k in range(TOP_K):
            pltpu.make_async_copy(x_ref.at[r], xs_hbm.at[dest_ref[0, k, r]], sem.at[0]).start(priority=k % 2)
    for k in range(TOP_K):
        pltpu.make_async_copy(x_ref, xs_hbm.at[pl.ds(0, tm)], sem.at[0]).wait()


def _dispatch_call(pad_lo, pad_hi, dest3, xpk, n_rows):
    t = xpk.shape[0]
    tm = dest3.shape[2]
    grid_spec = pltpu.PrefetchScalarGridSpec(
        num_scalar_prefetch=2,
        grid=(t // tm,),
        in_specs=[pl.BlockSpec((1, TOP_K, tm), lambda i, lo, hi: (i, 0, 0), memory_space=pltpu.SMEM),
                  pl.BlockSpec((tm, SLAB, LANES), lambda i, lo, hi: (i, 0, 0))],
        out_specs=pl.BlockSpec(memory_space=pl.ANY),
        scratch_shapes=[pltpu.VMEM((SLAB, LANES), xpk.dtype), pltpu.SemaphoreType.DMA((2,))],
    )
    return pl.pallas_call(
        functools.partial(_dispatch_kernel, tm=tm),
        grid_spec=grid_spec,
        out_shape=jax.ShapeDtypeStruct((n_rows, SLAB, LANES), xpk.dtype),
        compiler_params=_cparams("arbitrary"),
        name="dispatch",
    )(pad_lo, pad_hi, dest3, xpk)


def _expert_kernel(blk_e_ref, nused_ref, x0_ref, xn_ref, wgu_ref, wdn_ref, o_ref, wgu_b, wdn_b, xb0, xb1, *, bm):
    b = pl.program_id(0)
    xbs = (xb0, xb1)

    @pl.when(b == 0)
    def _():
        xb0[...] = _load_slabs_bf16(x0_ref, bm)

    @pl.when(jnp.logical_or(b == 0, blk_e_ref[b] != blk_e_ref[jnp.maximum(b - 1, 0)]))
    def _():
        wgu_b[...] = wgu_ref[0, 0].astype(BF16)
        wdn_b[...] = wdn_ref[0, 0].astype(BF16)

    def step(cur):
        xbs[1 - cur][...] = _load_slabs_bf16(xn_ref, bm)
        gu = jnp.dot(xbs[cur][...], wgu_b[...], preferred_element_type=F32)
        g = gu[:, :D_EXPERT]
        u = gu[:, D_EXPERT:]
        hmid = (g * jax.nn.sigmoid(g) * u).astype(BF16)
        o_ref[...] = jnp.dot(hmid, wdn_b[...], preferred_element_type=F32)

    for cur in range(2):
        pl.when(jnp.logical_and(b < nused_ref[0], lax.rem(b, 2) == cur))(functools.partial(step, cur))

    @pl.when(b >= nused_ref[0])
    def _():
        o_ref[...] = jnp.zeros_like(o_ref)


def _expert_call(blk_e, n_used, xs, wgu, wdn, bm, layer):
    n_blocks = xs.shape[0] // (bm * SLAB)
    d = wdn.shape[-1]

    def next_block(b, be, nu):
        return (jnp.minimum(b + 1, jnp.maximum(nu[0] - 1, 0)), 0)

    grid_spec = pltpu.PrefetchScalarGridSpec(
        num_scalar_prefetch=2,
        grid=(n_blocks,),
        in_specs=[pl.BlockSpec((bm * SLAB, LANES), lambda b, be, nu: (0, 0)),
                  pl.BlockSpec((bm * SLAB, LANES), next_block),
                  pl.BlockSpec((1, 1, d, 2 * D_EXPERT), lambda b, be, nu: (layer, be[b], 0, 0)),
                  pl.BlockSpec((1, 1, D_EXPERT, d), lambda b, be, nu: (layer, be[b], 0, 0))],
        out_specs=pl.BlockSpec((bm, d), lambda b, be, nu: (b, 0)),
        scratch_shapes=[pltpu.VMEM((d, 2 * D_EXPERT), BF16), pltpu.VMEM((D_EXPERT, d), BF16),
                        pltpu.VMEM((bm, d), BF16), pltpu.VMEM((bm, d), BF16)],
    )
    return pl.pallas_call(
        functools.partial(_expert_kernel, bm=bm),
        grid_spec=grid_spec,
        out_shape=jax.ShapeDtypeStruct((n_blocks * bm, d), F32),
        compiler_params=_cparams("arbitrary"),
        name="experts",
    )(blk_e, n_used, xs, xs, wgu, wdn)


def _combine_kernel(dest_ref, dest_next_ref, gate_ref, x_ref, xb_ref, ys_hbm, wsg_ref, wsd_ref, g_ref, b_ref,
                    *rest, tm, n1, nsteps):
    if n1 is None:
        o_ref, ob_ref, buf0, buf1, base_ref, sem = rest
    else:
        o1_ref, o2_ref, buf0, buf1, base_ref, sem = rest
    i = pl.program_id(0)
    bufs = (buf0, buf1)

    def issue(ref, s):
        for r in range(tm):
            for k in range(TOP_K):
                pltpu.make_async_copy(ys_hbm.at[pl.ds(ref[0, k, r], 1)],
                                      bufs[s].at[pl.ds(k * tm + r, 1)], sem.at[s]).start(priority=k % 2)

    def wait(s):
        pltpu.make_async_copy(ys_hbm.at[pl.ds(0, TOP_K * tm)], bufs[s], sem.at[s]).wait()

    @pl.when(i == 0)
    def _():
        issue(dest_ref, 0)

    gu = jnp.dot(xb_ref[...], wsg_ref[...], preferred_element_type=F32)
    g = gu[:, :D_SHARED]
    u = gu[:, D_SHARED:]
    hmid = (g * jax.nn.sigmoid(g) * u).astype(BF16)
    base_ref[...] = ALPHA * x_ref[...] + jnp.dot(hmid, wsd_ref[...], preferred_element_type=F32)

    def step(cur):
        wait(cur)
        issue(dest_next_ref, 1 - cur)
        acc = gate_ref[:, 0:1] * bufs[cur][0:tm, :]
        for k in range(1, TOP_K):
            acc = acc + gate_ref[:, k:k + 1] * bufs[cur][k * tm:(k + 1) * tm, :]
        y = _layer_norm(base_ref[...] + acc, g_ref[...], b_ref[...], LN_EPS)
        if n1 is None:
            o_ref[...] = y
            ob_ref[...] = y.astype(BF16)
        else:
            @pl.when(i < n1)
            def _():
                o1_ref[...] = y

            @pl.when(i >= n1)
            def _():
                o2_ref[...] = y

    for cur in range(2):
        pl.when(lax.rem(i, 2) == cur)(functools.partial(step, cur))

    @pl.when(i == nsteps - 1)
    def _():
        wait(nsteps % 2)


def _combine_call(dest3, gate_tk, x, xb, ys, wsg, wsd, g, b, split_rows=None):
    t, d = x.shape
    tm = dest3.shape[2]
    nsteps = t // tm
    if split_rows is None:
        n1 = None
        out_specs = [pl.BlockSpec((tm, d), lambda i: (i, 0)), pl.BlockSpec((tm, d), lambda i: (i, 0))]
        out_shape = [jax.ShapeDtypeStruct((t, d), F32), jax.ShapeDtypeStruct((t, d), BF16)]
    else:
        n1 = split_rows // tm
        out_specs = [pl.BlockSpec((tm, d), lambda i: (jnp.minimum(i, n1 - 1), 0)),
                     pl.BlockSpec((tm, d), lambda i: (jnp.maximum(i - n1, 0), 0))]
        out_shape = [jax.ShapeDtypeStruct((split_rows, d), F32), jax.ShapeDtypeStruct((t - split_rows, d), F32)]
    return pl.pallas_call(
        functools.partial(_combine_kernel, tm=tm, n1=n1, nsteps=nsteps),
        grid=(nsteps,),
        in_specs=[pl.BlockSpec((1, TOP_K, tm), lambda i: (i, 0, 0), memory_space=pltpu.SMEM),
                  pl.BlockSpec((1, TOP_K, tm), lambda i: (jnp.minimum(i + 1, nsteps - 1), 0, 0),
                               memory_space=pltpu.SMEM),
                  pl.BlockSpec((tm, TOP_K), lambda i: (i, 0)),
                  pl.BlockSpec((tm, d), lambda i: (i, 0)),
                  pl.BlockSpec((tm, d), lambda i: (i, 0)),
                  pl.BlockSpec(memory_space=pl.ANY),
                  pl.BlockSpec((d, 2 * D_SHARED), lambda i: (0, 0)),
                  pl.BlockSpec((D_SHARED, d), lambda i: (0, 0)),
                  pl.BlockSpec((1, d), lambda i: (0, 0)),
                  pl.BlockSpec((1, d), lambda i: (0, 0))],
        out_specs=out_specs,
        out_shape=out_shape,
        scratch_shapes=[pltpu.VMEM((TOP_K * tm, d), F32), pltpu.VMEM((TOP_K * tm, d), F32),
                        pltpu.VMEM((tm, d), F32), pltpu.SemaphoreType.DMA((2,))],
        compiler_params=_cparams("arbitrary"),
        name="combine_ln",
    )(dest3, dest3, gate_tk, x, xb, ys, wsg, wsd, g.reshape(1, d), b.reshape(1, d))


def _dispatch_tables(ids, rank, counts_f, bm):
    k, t = ids.shape
    tk = k * t
    n_blocks = -(-tk // bm) + N_EXPERTS
    counts = counts_f[:, 0].astype(I32)
    padded = (counts + bm - 1) // bm * bm
    pends = jnp.cumsum(padded)
    pstart = pends - padded
    experts = jnp.arange(N_EXPERTS, dtype=I32)
    dest = jnp.sum(jnp.where(ids[:, :, None] == experts, pstart, 0), axis=-1) + rank
    blk_first = jnp.arange(n_blocks, dtype=I32) * bm
    blk_e = jnp.minimum(jnp.sum(pends[None, :] <= blk_first[:, None], axis=1), N_EXPERTS - 1).astype(I32)
    n_used = (pends[-1:] // bm).astype(I32)
    return dest, blk_e, n_used, (pstart + counts).astype(I32), pends.astype(I32), n_blocks * bm


def _layer(x, xb, memb, seq, layer, w_in, b_gate, conv_w, conv_b, mh_norm_g, pool_w, pool_scale, w_mem_kv, w_out,
           ln1_g, ln1_b, w_router, b_router, w_gu_all, w_dn_all, w_sh_gu, w_sh_dn, ln2_g, ln2_b, last):
    t = x.shape[0]
    hin = _mm_call(xb, w_in[:, :OFF_G].astype(BF16), F32, "in_proj")
    g_rows, g_cols = _gates_call(xb, _gate_rows(w_in[:, OFF_G:].T).astype(BF16), _gate_rows(b_gate))
    gr3 = g_rows.reshape(3 * GDIR, t // CHUNK, CHUNK).transpose(1, 0, 2)
    q = _conv_call(hin, conv_w, conv_b, seq, col0=OFF_Q, scale=1.0, transpose_out=False)
    kt = _conv_call(hin, conv_w, conv_b, seq, col0=OFF_K, scale=M_HEAD_DIM ** -0.5, transpose_out=True)
    h_f = _mlstm_call(q, kt, hin, gr3, g_cols, seq, reverse=False)
    h_b = _mlstm_call(q, kt, hin, gr3, g_cols, seq, reverse=True)
    m_out = _headnorm_call(h_f, h_b, hin, mh_norm_g)
    kv = _mm_call(memb, w_mem_kv.astype(BF16), BF16, "mem_kv").reshape(seq.nseq, -1, 2 * X_WIDTH)
    pa_out = _pool_attn_call(hin, kv, pool_w.astype(BF16), pool_scale, seq)
    x1, x1b, x1p = _outproj_call(m_out, pa_out, x, w_out.astype(BF16), ln1_g, ln1_b)
    ids, gate, rank, counts = _router_call(x1b, w_router.T.astype(BF16), b_router)
    bm = MOE_BM
    dest, blk_e, n_used, pad_lo, pad_hi, n_rows = _dispatch_tables(ids, rank, counts, bm)
    ctm = min(COMB_TM, t)
    dest3 = dest.reshape(TOP_K, t // ctm, ctm).transpose(1, 0, 2)
    dtm = min(DISP_TM, t)
    dest3d = dest.reshape(TOP_K, t // dtm, dtm).transpose(1, 0, 2)
    xs = _dispatch_call(pad_lo, pad_hi, dest3d, x1p.reshape(t, SLAB, LANES), n_rows)
    ys = _expert_call(blk_e, n_used, xs.reshape(n_rows * SLAB, LANES), w_gu_all, w_dn_all, bm, layer)
    return _combine_call(dest3, gate.T, x1, x1b, ys, w_sh_gu.astype(BF16), w_sh_dn.astype(BF16), ln2_g, ln2_b,
                         split_rows=seq.t1 if last else None)


def kernel(x_prompt, x_sample, mem_prompt, mem_sample, ln_in_g, ln_in_b, w_in, b_gate, conv_w, conv_b, mh_norm_g, pool_w, pool_scale, w_mem_kv, w_out, ln1_g, ln1_b, w_router, b_router, w_gu, w_dn, w_sh_gu, w_sh_dn, ln2_g, ln2_b):
    b1, s1, d = x_prompt.shape
    b2, s2, _ = x_sample.shape
    seq = SeqInfo(b1, s1, b2, s2)
    memb = jnp.concatenate([mem_prompt.reshape(-1, d), mem_sample.reshape(-1, d)], axis=0).astype(BF16)
    x, xb = _ln_call(x_prompt.reshape(b1 * s1, d), x_sample.reshape(b2 * s2, d), ln_in_g, ln_in_b)
    depth = w_in.shape[0]
    for l in range(depth):
        x, xb = _layer(x, xb, memb, seq, l, w_in[l], b_gate[l], conv_w[l], conv_b[l], mh_norm_g[l], pool_w[l],
                       pool_scale[l], w_mem_kv[l], w_out[l], ln1_g[l], ln1_b[l], w_router[l], b_router[l],
                       w_gu, w_dn, w_sh_gu[l], w_sh_dn[l], ln2_g[l], ln2_b[l], last=l == depth - 1)
    return (x.reshape(b1, s1, d), xb.reshape(b2, s2, d))
```

```python
import functools

import jax
import jax.numpy as jnp
from jax import lax
from jax.experimental import pallas as pl
from jax.experimental.pallas import tpu as pltpu

F32 = jnp.float32
BF16 = jnp.bfloat16
I32 = jnp.int32

DEPTH = 4
D_MODEL = 2048
M_HEADS = 4
M_WIDTH = 1024
M_HEAD_DIM = 256
CHUNK = 128
CONV_K = 5
POOL_WINDOWS = (2, 4, 8, 16)
POOL_WIDTH = 512
POOL_GROUP = 128
X_HEADS = 4
X_WIDTH = 512
X_HEAD_DIM = 128
OFF_Q = 0
OFF_K = OFF_Q + M_WIDTH
OFF_V = OFF_K + M_WIDTH
OFF_O = OFF_V + M_WIDTH
OFF_P = OFF_O + M_WIDTH
OFF_XQ = OFF_P + POOL_WIDTH
OFF_G = OFF_XQ + X_WIDTH
N_GATES = 4 * M_HEADS
N_EXPERTS = 64
TOP_K = 8
N_GROUPS = 8
GROUP_SIZE = N_EXPERTS // N_GROUPS
TOPK_GROUPS = 4
D_EXPERT = 512
D_SHARED = 512
ROUTED_SCALE = 2.5
ALPHA = (2 * DEPTH) ** 0.25
LN_EPS = 1e-5
HN_EPS = 1e-6

LANES = 128
SUBLANES = 8
HALO = SUBLANES
VMEM_LIMIT = 58 * 1024 * 1024

LN_TM = 512
MM_TM = 512
MM_TN = 1024
SEQ_TM = 512
MLSTM_CB = 4
OUT_TM = 512
OUT_SUB = 256
ROUTE_TM = 512
MOE_BM = 512
FFN_SLICE = 256
DISP_TM = 256
COMB_TM = 128


def _cparams(*sem):
    return pltpu.CompilerParams(dimension_semantics=sem, vmem_limit_bytes=VMEM_LIMIT)


class SeqInfo:
    def __init__(self, b1, s1, b2, s2):
        self.b1, self.s1, self.b2, self.s2 = b1, s1, b2, s2
        self.t1 = b1 * s1
        self.t = b1 * s1 + b2 * s2
        self.nseq = b1 + b2

    def pos_len(self, row0):
        in1 = row0 < self.t1
        pos = jnp.where(in1, lax.rem(row0, self.s1), lax.rem(row0 - self.t1, self.s2))
        slen = jnp.where(in1, self.s1, self.s2)
        return pos, slen

    def seq_id(self, row0):
        in1 = row0 < self.t1
        return jnp.where(in1, row0 // self.s1, self.b1 + (row0 - self.t1) // self.s2)


def _layer_norm(y, g, b, eps):
    mu = jnp.mean(y, axis=-1, keepdims=True)
    d = y - mu
    var = jnp.mean(d * d, axis=-1, keepdims=True)
    return d * lax.rsqrt(var + eps) * g + b


def _ln_kernel(x1_ref, x2_ref, g_ref, b_ref, o_ref, ob_ref, *, n1):
    x = jnp.where(pl.program_id(0) < n1, x1_ref[...], x2_ref[...])
    y = _layer_norm(x, g_ref[...], b_ref[...], LN_EPS)
    o_ref[...] = y
    ob_ref[...] = y.astype(BF16)


def _ln_call(x1, x2, g, b):
    t1, d = x1.shape
    t2 = x2.shape[0]
    tm = min(LN_TM, t1, t2)
    n1, n2 = t1 // tm, t2 // tm
    t = t1 + t2
    return pl.pallas_call(
        functools.partial(_ln_kernel, n1=n1),
        grid=(n1 + n2,),
        in_specs=[pl.BlockSpec((tm, d), lambda i: (jnp.minimum(i, n1 - 1), 0)),
                  pl.BlockSpec((tm, d), lambda i: (jnp.maximum(i - n1, 0), 0)),
                  pl.BlockSpec((1, d), lambda i: (0, 0)),
                  pl.BlockSpec((1, d), lambda i: (0, 0))],
        out_specs=[pl.BlockSpec((tm, d), lambda i: (i, 0)),
                   pl.BlockSpec((tm, d), lambda i: (i, 0))],
        out_shape=[jax.ShapeDtypeStruct((t, d), F32), jax.ShapeDtypeStruct((t, d), BF16)],
        compiler_params=_cparams("arbitrary"),
        name="ln_in",
    )(x1, x2, g.reshape(1, d), b.reshape(1, d))


def _mm_kernel(x_ref, w_ref, o_ref):
    o_ref[...] = jnp.dot(x_ref[...], w_ref[...], preferred_element_type=F32).astype(o_ref.dtype)


def _mm_call(xb, wb, out_dtype, name):
    m, k = xb.shape
    n = wb.shape[1]
    tm = min(MM_TM, m)
    while m % tm:
        tm //= 2
    tn = min(MM_TN, n)
    return pl.pallas_call(
        _mm_kernel,
        grid=(n // tn, m // tm),
        in_specs=[pl.BlockSpec((tm, k), lambda j, i: (i, 0)),
                  pl.BlockSpec((k, tn), lambda j, i: (0, j))],
        out_specs=pl.BlockSpec((tm, tn), lambda j, i: (i, j)),
        out_shape=jax.ShapeDtypeStruct((m, n), out_dtype),
        compiler_params=_cparams("parallel", "parallel"),
        name=name,
    )(xb, wb)


GDIR = 2 * M_HEADS


def _chunk_scan(x, op, reverse):
    n = x.shape[1]
    pos = lax.rem(lax.broadcasted_iota(I32, x.shape, 1), CHUNK)
    s = 1
    while s < CHUNK:
        if reverse:
            x = jnp.where(pos < CHUNK - s, op(x, pltpu.roll(x, n - s, axis=1)), x)
        else:
            x = jnp.where(pos >= s, op(x, pltpu.roll(x, s, axis=1)), x)
        s *= 2
    return x


def _gates_kernel(x_ref, w_ref, b_ref, row_ref, col_ref):
    g = lax.dot_general(w_ref[...], x_ref[...], (((1,), (1,)), ((), ())),
                        preferred_element_type=F32) + b_ref[...]
    gi = g[0:GDIR]
    logf = jax.nn.log_sigmoid(g[GDIR:2 * GDIR])
    is_fwd = lax.broadcasted_iota(I32, gi.shape, 0) < M_HEADS
    pre = _chunk_scan(logf, jnp.add, False)
    suf = _chunk_scan(logf, jnp.add, True)
    b = jnp.where(is_fwd, pre, suf)
    total = pre + suf - logf
    ib = gi - b
    cmax = jnp.where(is_fwd, _chunk_scan(ib, jnp.maximum, False), _chunk_scan(ib, jnp.maximum, True))
    lw = total + ib
    lw_max = jnp.maximum(_chunk_scan(lw, jnp.maximum, False), _chunk_scan(lw, jnp.maximum, True))
    row_ref[...] = jnp.concatenate([ib, total, lw_max], axis=0)
    pad = jnp.zeros((LANES - 3 * GDIR, gi.shape[1]), F32)
    col_ref[...] = jnp.concatenate([b, cmax, lw, pad], axis=0).T


def _gates_call(xb, wgt, bg):
    t, d = xb.shape
    tm = min(MM_TM, t)
    return pl.pallas_call(
        _gates_kernel,
        grid=(t // tm,),
        in_specs=[pl.BlockSpec((tm, d), lambda i: (i, 0)),
                  pl.BlockSpec((N_GATES, d), lambda i: (0, 0)),
                  pl.BlockSpec((N_GATES, 1), lambda i: (0, 0))],
        out_specs=[pl.BlockSpec((3 * GDIR, tm), lambda i: (0, i)),
                   pl.BlockSpec((tm, LANES), lambda i: (i, 0))],
        out_shape=[jax.ShapeDtypeStruct((3 * GDIR, t), F32), jax.ShapeDtypeStruct((t, LANES), F32)],
        compiler_params=_cparams("parallel"),
        name="gates",
    )(xb, wgt, bg.reshape(N_GATES, 1))


def _gate_rows(a):
    h = M_HEADS
    return jnp.concatenate([a[0:h], a[2 * h:3 * h], a[h:2 * h], a[3 * h:4 * h]], axis=0)


def _halo_specs(tm, width, col_block, n_rows):
    per = tm // HALO
    last = n_rows // HALO - 1
    prev = pl.BlockSpec((HALO, width), lambda i, *_: (jnp.maximum(i * per - 1, 0), col_block))
    cur = pl.BlockSpec((tm, width), lambda i, *_: (i, col_block))
    nxt = pl.BlockSpec((HALO, width), lambda i, *_: (jnp.minimum((i + 1) * per, last), col_block))
    return prev, cur, nxt


def _extended_tile(prev_ref, cur_ref, next_ref, tm, seq):
    row0 = pl.program_id(0) * tm
    pos, slen = seq.pos_len(row0)
    prev = jnp.where(pos == 0, 0.0, prev_ref[...])
    nxt = jnp.where(pos + tm == slen, 0.0, next_ref[...])
    return jnp.concatenate([prev, cur_ref[...], nxt], axis=0), pos, slen


def _shift_rows(xe, d):
    n = xe.shape[0]
    if d == 0:
        return xe
    return pltpu.roll(xe, (n - d) % n, axis=0)


def _conv_kernel(prev_ref, cur_ref, next_ref, w_ref, b_ref, o_ref, *, tm, tc, seq, scale, transpose_out):
    xe, _, _ = _extended_tile(prev_ref, cur_ref, next_ref, tm, seq)
    acc = jnp.broadcast_to(b_ref[...], (tm, tc))
    for j in range(CONV_K):
        sh = _shift_rows(xe, j - CONV_K // 2)
        acc = acc + sh[HALO:HALO + tm] * w_ref[j:j + 1, :]
    y = acc * jax.nn.sigmoid(acc) * scale
    o_ref[...] = (y.T if transpose_out else y).astype(BF16)


def _conv_call(hin, conv_w, conv_b, seq, *, col0, scale, transpose_out):
    t = hin.shape[0]
    tm = min(SEQ_TM, seq.s1, seq.s2)
    tc = 512
    j0 = col0 // tc
    prev = pl.BlockSpec((HALO, tc), lambda i, j: (jnp.maximum(i * (tm // HALO) - 1, 0), j + j0))
    cur = pl.BlockSpec((tm, tc), lambda i, j: (i, j + j0))
    nxt = pl.BlockSpec((HALO, tc), lambda i, j: (jnp.minimum((i + 1) * (tm // HALO), t // HALO - 1), j + j0))
    if transpose_out:
        out_spec = pl.BlockSpec((tc, tm), lambda i, j: (j, i))
        out_shape = jax.ShapeDtypeStruct((M_WIDTH, t), BF16)
    else:
        out_spec = pl.BlockSpec((tm, tc), lambda i, j: (i, j))
        out_shape = jax.ShapeDtypeStruct((t, M_WIDTH), BF16)
    return pl.pallas_call(
        functools.partial(_conv_kernel, tm=tm, tc=tc, seq=seq, scale=scale, transpose_out=transpose_out),
        grid=(t // tm, M_WIDTH // tc),
        in_specs=[prev, cur, nxt,
                  pl.BlockSpec((CONV_K, tc), lambda i, j: (0, j + j0)),
                  pl.BlockSpec((1, tc), lambda i, j: (0, j + j0))],
        out_specs=out_spec,
        out_shape=out_shape,
        compiler_params=_cparams("parallel", "parallel"),
        name="conv_silu_t" if transpose_out else "conv_silu",
    )(hin, hin, hin, conv_w, conv_b.reshape(1, 2 * M_WIDTH))


def _mlstm_kernel(qf, ktf, vf, grf, gcf, qb, ktb, vb, grb, gcb, of_ref, ob_ref, cf, mf, cb_ref, mb, *, cb, seq):
    @pl.when(pl.program_id(0) == 0)
    def _():
        for ref in (cf, mf, cb_ref, mb):
            ref[...] = jnp.zeros_like(ref)

    _mlstm_direction(qf, ktf, vf, grf, gcf, of_ref, cf, mf, reverse=False, cb=cb, seq=seq)
    _mlstm_direction(qb, ktb, vb, grb, gcb, ob_ref, cb_ref, mb, reverse=True, cb=cb, seq=seq)


def _mlstm_direction(q_ref, kt_ref, v_ref, gr_ref, gc_ref, o_ref, c_ref, m_ref, *, reverse, cb, seq):
    L, DK, DV = CHUNK, M_HEAD_DIM, M_HEAD_DIM
    step = pl.program_id(0)
    nsteps = pl.num_programs(0)
    blk = (nsteps - 1 - step) if reverse else step
    ti = lax.broadcasted_iota(I32, (L, L), 0)
    si = lax.broadcasted_iota(I32, (L, L), 1)
    causal = (si >= ti) if reverse else (si <= ti)
    slot0 = M_HEADS if reverse else 0

    m_vals = [m_ref[h, 0:1, 0:1] for h in range(M_HEADS)]
    for jj in range(cb):
        j = (cb - 1 - jj) if reverse else jj
        rows = slice(j * L, (j + 1) * L)
        pos, slen = seq.pos_len((blk * cb + j) * L)
        at_start = (pos + L == slen) if reverse else (pos == 0)
        grow = gr_ref[j]
        ones = jnp.ones((L, LANES), F32)
        for h in range(M_HEADS):
            cols = slice(h * DK, (h + 1) * DK)
            sl = slot0 + h
            q = q_ref[rows, cols]
            kt = kt_ref[cols, rows]
            v32 = v_ref[rows, cols]
            ib_row = grow[sl:sl + 1, :]
            g_tot = grow[GDIR + sl:GDIR + sl + 1, 0:1]
            lw_max = grow[2 * GDIR + sl:2 * GDIR + sl + 1, 0:1]
            b_col = gc_ref[rows, sl:sl + 1]
            cmax_col = gc_ref[rows, GDIR + sl:GDIR + sl + 1]
            lw_col = gc_ref[rows, 2 * GDIR + sl:2 * GDIR + sl + 1]
            m_prev = jnp.where(at_start, 0.0, m_vals[h])
            m_rel = jnp.maximum(m_prev, cmax_col)
            m_t = b_col + m_rel
            dexp = jnp.exp(jnp.where(causal, ib_row - m_rel, -jnp.inf))
            s = jnp.dot(q, kt, preferred_element_type=F32)
            a = (s * dexp).astype(BF16)
            inter = jnp.exp(m_prev - m_rel)
            c_aug = jnp.where(at_start, 0.0, c_ref[h])
            qc = jnp.dot(q, c_aug.astype(BF16), preferred_element_type=F32)
            v_ext = jnp.concatenate([v32, ones], axis=1).astype(BF16)
            nd = jnp.dot(a, v_ext, preferred_element_type=F32) + inter * qc
            den = nd[:, DV:DV + 1]
            hh = nd[:, :DV] / jnp.maximum(jnp.abs(den), jnp.exp(-m_t))
            m_new = jnp.maximum(g_tot + m_prev, lw_max)
            wk_col = jnp.exp(lw_col - m_new)
            decay = jnp.exp(g_tot + m_prev - m_new)
            v_aug = (jnp.concatenate([v32, ones], axis=1) * wk_col).astype(BF16)
            upd = jnp.dot(kt, v_aug, preferred_element_type=F32)
            c_ref[h] = decay * c_aug + upd
            m_vals[h] = m_new
            o_ref[rows, cols] = hh
    for h in range(M_HEADS):
        m_ref[h] = jnp.broadcast_to(m_vals[h], (SUBLANES, LANES))


def _mlstm_call(q, kt, hin, gr3, gc, seq):
    t = q.shape[0]
    cb = MLSTM_CB
    while (seq.s1 // CHUNK) % cb or (seq.s2 // CHUNK) % cb:
        cb //= 2
    rows = cb * CHUNK
    nsteps = t // rows
    vblk = OFF_V // M_WIDTH

    def specs(blk):
        return [pl.BlockSpec((rows, M_WIDTH), lambda s: (blk(s), 0)),
                pl.BlockSpec((M_WIDTH, rows), lambda s: (0, blk(s))),
                pl.BlockSpec((rows, M_WIDTH), lambda s: (blk(s), vblk)),
                pl.BlockSpec((cb, 3 * GDIR, CHUNK), lambda s: (blk(s), 0, 0)),
                pl.BlockSpec((rows, LANES), lambda s: (blk(s), 0))]

    fwd = lambda s: s
    bwd = lambda s: nsteps - 1 - s
    state = [pltpu.VMEM((M_HEADS, M_HEAD_DIM, M_HEAD_DIM + LANES), F32), pltpu.VMEM((M_HEADS, SUBLANES, LANES), F32)]
    return pl.pallas_call(
        functools.partial(_mlstm_kernel, cb=cb, seq=seq),
        grid=(nsteps,),
        in_specs=specs(fwd) + specs(bwd),
        out_specs=[pl.BlockSpec((rows, M_WIDTH), lambda s: (fwd(s), 0)),
                   pl.BlockSpec((rows, M_WIDTH), lambda s: (bwd(s), 0))],
        out_shape=[jax.ShapeDtypeStruct((t, M_WIDTH), F32), jax.ShapeDtypeStruct((t, M_WIDTH), F32)],
        scratch_shapes=state + state,
        compiler_params=_cparams("arbitrary"),
        name="mlstm",
    )(q, kt, hin, gr3, gc, q, kt, hin, gr3, gc)


def _headnorm_kernel(hf_ref, hb_ref, uo_ref, ng_ref, o_ref):
    for h in range(M_HEADS):
        cols = slice(h * M_HEAD_DIM, (h + 1) * M_HEAD_DIM)
        hs = hf_ref[:, cols] + hb_ref[:, cols]
        mu = jnp.mean(hs, axis=1, keepdims=True)
        d = hs - mu
        var = jnp.mean(d * d, axis=1, keepdims=True)
        hn = d * lax.rsqrt(var + HN_EPS) * ng_ref[:, cols]
        o_ref[:, cols] = (hn * jax.nn.sigmoid(uo_ref[:, cols])).astype(o_ref.dtype)


def _headnorm_call(h_f, h_b, hin, norm_g):
    t = h_f.shape[0]
    tm = min(LN_TM, t)
    return pl.pallas_call(
        _headnorm_kernel,
        grid=(t // tm,),
        in_specs=[pl.BlockSpec((tm, M_WIDTH), lambda i: (i, 0)),
                  pl.BlockSpec((tm, M_WIDTH), lambda i: (i, 0)),
                  pl.BlockSpec((tm, M_WIDTH), lambda i: (i, OFF_O // M_WIDTH)),
                  pl.BlockSpec((1, M_WIDTH), lambda i: (0, 0))],
        out_specs=pl.BlockSpec((tm, M_WIDTH), lambda i: (i, 0)),
        out_shape=jax.ShapeDtypeStruct((t, M_WIDTH), BF16),
        compiler_params=_cparams("parallel"),
        name="headnorm",
    )(h_f, h_b, hin, norm_g.reshape(1, M_WIDTH))


def _pool_attn_kernel(prev_ref, cur_ref, next_ref, xq_ref, kv_ref, pw_ref, ps_ref, o_ref, *, tm, seq):
    xe, pos0, slen = _extended_tile(prev_ref, cur_ref, next_ref, tm, seq)
    pos = pos0 + lax.broadcasted_iota(I32, (tm, 1), 0)
    for gi, w in enumerate(POOL_WINDOWS):
        cols = slice(gi * POOL_GROUP, (gi + 1) * POOL_GROUP)
        u = xe[:, cols]
        win = _shift_rows(u, -1) + u
        half = 1
        while 2 * half < w:
            win = _shift_rows(win, -half) + _shift_rows(win, half)
            half *= 2
        cnt = (jnp.minimum(pos + w // 2, slen) - jnp.maximum(pos - w // 2, 0)).astype(F32)
        mean = win[HALO:HALO + tm] / cnt
        diff = (mean - u[HALO:HALO + tm]).astype(BF16)
        y = jnp.dot(diff, pw_ref[gi], preferred_element_type=F32)
        o_ref[:, cols] = (y * ps_ref[:, cols]).astype(o_ref.dtype)
    xq = xq_ref[...].astype(BF16)
    for h in range(X_HEADS):
        cols = slice(h * X_HEAD_DIM, (h + 1) * X_HEAD_DIM)
        kh = kv_ref[0, :, h * X_HEAD_DIM:(h + 1) * X_HEAD_DIM]
        vh = kv_ref[0, :, X_WIDTH + h * X_HEAD_DIM:X_WIDTH + (h + 1) * X_HEAD_DIM]
        s = lax.dot_general(xq[:, cols], kh, (((1,), (1,)), ((), ())),
                            preferred_element_type=F32) * (X_HEAD_DIM ** -0.5)
        e = jnp.exp(s - jnp.max(s, axis=1, keepdims=True))
        p = e / jnp.sum(e, axis=1, keepdims=True)
        y = jnp.dot(p.astype(BF16), vh, preferred_element_type=F32)
        o_ref[:, POOL_WIDTH + h * X_HEAD_DIM:POOL_WIDTH + (h + 1) * X_HEAD_DIM] = y.astype(o_ref.dtype)


def _pool_attn_call(hin, kv, pool_w, pool_scale, seq):
    t = hin.shape[0]
    tm = min(SEQ_TM, seq.s1, seq.s2)
    n_mem = kv.shape[1]
    prev, cur, nxt = _halo_specs(tm, POOL_WIDTH, OFF_P // POOL_WIDTH, t)
    return pl.pallas_call(
        functools.partial(_pool_attn_kernel, tm=tm, seq=seq),
        grid=(t // tm,),
        in_specs=[prev, cur, nxt,
                  pl.BlockSpec((tm, X_WIDTH), lambda i: (i, OFF_XQ // X_WIDTH)),
                  pl.BlockSpec((1, n_mem, 2 * X_WIDTH), lambda i: (seq.seq_id(i * tm), 0, 0)),
                  pl.BlockSpec((len(POOL_WINDOWS), POOL_GROUP, POOL_GROUP), lambda i: (0, 0, 0)),
                  pl.BlockSpec((1, POOL_WIDTH), lambda i: (0, 0))],
        out_specs=pl.BlockSpec((tm, POOL_WIDTH + X_WIDTH), lambda i: (i, 0)),
        out_shape=jax.ShapeDtypeStruct((t, POOL_WIDTH + X_WIDTH), BF16),
        compiler_params=_cparams("parallel"),
        name="pool_attn",
    )(hin, hin, hin, hin, kv, pool_w, pool_scale.reshape(1, POOL_WIDTH))


def _pack_halves(yb):
    n = yb.shape[1] // 2
    bits = pltpu.bitcast(yb.astype(F32), jnp.uint32)
    return jnp.bitwise_or(bits[:, n:], jnp.right_shift(bits[:, :n], jnp.uint32(16)))


SLAB = SUBLANES


def _store_slabs(o_ref, packed, row0=0):
    rows = packed.shape[0]
    for s in range(SLAB):
        o_ref[pl.ds(row0 * SLAB + s, rows, stride=SLAB), :] = packed[:, s * LANES:(s + 1) * LANES]


def _load_slabs_bf16(x_ref, rows):
    words = [x_ref[pl.ds(s, rows, stride=SLAB), :] for s in range(SLAB)]
    lo = [pltpu.bitcast(jnp.left_shift(w, jnp.uint32(16)), F32) for w in words]
    hi = [pltpu.bitcast(jnp.bitwise_and(w, jnp.uint32(0xFFFF0000)), F32) for w in words]
    return jnp.concatenate(lo + hi, axis=1).astype(BF16)


def _outproj_kernel(m_ref, pa_ref, x_ref, w_ref, g_ref, b_ref, o_ref, ob_ref, op_ref, *, tm, sub):
    for r0 in range(0, tm, sub):
        rows = slice(r0, r0 + sub)
        acc = jnp.dot(m_ref[rows, :], w_ref[0:M_WIDTH, :], preferred_element_type=F32)
        acc = acc + jnp.dot(pa_ref[rows, :], w_ref[M_WIDTH:, :], preferred_element_type=F32)
        y = _layer_norm(ALPHA * x_ref[rows, :] + acc, g_ref[...], b_ref[...], LN_EPS)
        o_ref[rows, :] = y
        yb = y.astype(BF16)
        ob_ref[rows, :] = yb
        _store_slabs(op_ref, _pack_halves(yb), r0)


def _outproj_call(m_out, pa_out, x, w_out, g, b):
    t, d = x.shape
    tm = min(OUT_TM, t)
    return pl.pallas_call(
        functools.partial(_outproj_kernel, tm=tm, sub=min(OUT_SUB, tm)),
        grid=(t // tm,),
        in_specs=[pl.BlockSpec((tm, M_WIDTH), lambda i: (i, 0)),
                  pl.BlockSpec((tm, POOL_WIDTH + X_WIDTH), lambda i: (i, 0)),
                  pl.BlockSpec((tm, d), lambda i: (i, 0)),
                  pl.BlockSpec((d, d), lambda i: (0, 0)),
                  pl.BlockSpec((1, d), lambda i: (0, 0)),
                  pl.BlockSpec((1, d), lambda i: (0, 0))],
        out_specs=[pl.BlockSpec((tm, d), lambda i: (i, 0)),
                   pl.BlockSpec((tm, d), lambda i: (i, 0)),
                   pl.BlockSpec((tm * SLAB, LANES), lambda i: (i, 0))],
        out_shape=[jax.ShapeDtypeStruct((t, d), F32), jax.ShapeDtypeStruct((t, d), BF16),
                   jax.ShapeDtypeStruct((t * SLAB, LANES), jnp.uint32)],
        compiler_params=_cparams("parallel"),
        name="outproj_ln",
    )(m_out, pa_out, x, w_out, g.reshape(1, d), b.reshape(1, d))


def _first_index_of_max(vals, idx, axis_size):
    mx = jnp.max(vals, axis=0, keepdims=True)
    first = jnp.min(jnp.where(vals == mx, idx, axis_size), axis=0, keepdims=True)
    return mx, first


def _router_kernel(x_ref, w_ref, b_ref, ids_ref, gate_ref, rank_ref, cnt_ref, carry_ref, *, tm):
    @pl.when(pl.program_id(0) == 0)
    def _():
        carry_ref[...] = jnp.zeros_like(carry_ref)

    logits = lax.dot_general(w_ref[...], x_ref[...], (((1,), (1,)), ((), ())),
                             preferred_element_type=F32)
    s = jax.nn.sigmoid(logits)
    sel = s + b_ref[...]
    neg = -jnp.inf
    sub = lax.broadcasted_iota(I32, (GROUP_SIZE, tm), 0)
    gscores = []
    for g in range(N_GROUPS):
        blk = sel[g * GROUP_SIZE:(g + 1) * GROUP_SIZE, :]
        m1, i1 = _first_index_of_max(blk, sub, GROUP_SIZE)
        m2 = jnp.max(jnp.where(sub == i1, neg, blk), axis=0, keepdims=True)
        gscores.append(m1 + m2)
    gs = jnp.concatenate(gscores, axis=0)
    gidx = lax.broadcasted_iota(I32, (N_GROUPS, tm), 0)
    gmask = jnp.zeros((N_GROUPS, tm), jnp.bool_)
    for _ in range(TOPK_GROUPS):
        _, gi = _first_index_of_max(gs, gidx, N_GROUPS)
        hit = gidx == gi
        gmask = jnp.logical_or(gmask, hit)
        gs = jnp.where(hit, neg, gs)
    gmask_f = gmask.astype(F32)
    masked = jnp.concatenate(
        [jnp.where(gmask_f[g:g + 1, :] > 0.5, sel[g * GROUP_SIZE:(g + 1) * GROUP_SIZE, :], neg)
         for g in range(N_GROUPS)], axis=0)
    eidx = lax.broadcasted_iota(I32, (N_EXPERTS, tm), 0)
    chosen = jnp.zeros((N_EXPERTS, tm), F32)
    ids, gates, hits = [], [], []
    for _ in range(TOP_K):
        _, ei = _first_index_of_max(masked, eidx, N_EXPERTS)
        hit = eidx == ei
        ids.append(ei)
        gates.append(jnp.sum(jnp.where(hit, s, 0.0), axis=0, keepdims=True))
        hits.append(hit)
        chosen = jnp.where(hit, 1.0, chosen)
        masked = jnp.where(hit, neg, masked)
    gate = jnp.concatenate(gates, axis=0)
    gate = gate / jnp.sum(gate, axis=0, keepdims=True) * ROUTED_SCALE
    ids_ref[...] = jnp.concatenate(ids, axis=0)
    gate_ref[...] = gate
    r = lax.broadcasted_iota(I32, (tm, tm), 0)
    c = lax.broadcasted_iota(I32, (tm, tm), 1)
    before = (r < c).astype(BF16)
    cnt = jnp.dot(chosen.astype(BF16), before, preferred_element_type=F32) + carry_ref[:, 0:1]
    ranks = [jnp.sum(jnp.where(hit, cnt, 0.0), axis=0, keepdims=True) for hit in hits]
    rank_ref[...] = jnp.concatenate(ranks, axis=0).astype(I32)
    total = carry_ref[:, 0:1] + jnp.sum(chosen, axis=1, keepdims=True)
    carry_ref[...] = jnp.broadcast_to(total, carry_ref.shape)
    cnt_ref[...] = jnp.broadcast_to(total, cnt_ref.shape)


def _router_call(xb, wrt, br):
    t, d = xb.shape
    tm = min(ROUTE_TM, t)
    return pl.pallas_call(
        functools.partial(_router_kernel, tm=tm),
        grid=(t // tm,),
        in_specs=[pl.BlockSpec((tm, d), lambda i: (i, 0)),
                  pl.BlockSpec((N_EXPERTS, d), lambda i: (0, 0)),
                  pl.BlockSpec((N_EXPERTS, 1), lambda i: (0, 0))],
        out_specs=[pl.BlockSpec((TOP_K, tm), lambda i: (0, i)),
                   pl.BlockSpec((TOP_K, tm), lambda i: (0, i)),
                   pl.BlockSpec((TOP_K, tm), lambda i: (0, i)),
                   pl.BlockSpec((N_EXPERTS, LANES), lambda i: (0, 0))],
        out_shape=[jax.ShapeDtypeStruct((TOP_K, t), I32),
                   jax.ShapeDtypeStruct((TOP_K, t), F32),
                   jax.ShapeDtypeStruct((TOP_K, t), I32),
                   jax.ShapeDtypeStruct((N_EXPERTS, LANES), F32)],
        scratch_shapes=[pltpu.VMEM((N_EXPERTS, LANES), F32)],
        compiler_params=_cparams("arbitrary"),
        name="router",
    )(xb, wrt, br.reshape(N_EXPERTS, 1))


def _dispatch_kernel(pad_lo_ref, pad_hi_ref, dest_ref, x_ref, xs_hbm, zrow, sem, *, tm):
    i = pl.program_id(0)

    @pl.when(i == 0)
    def _():
        zrow[...] = jnp.zeros_like(zrow)

        def per_expert(e, carry):
            lo, hi = pad_lo_ref[e], pad_hi_ref[e]

            def start(p, c):
                pltpu.make_async_copy(zrow, xs_hbm.at[p], sem.at[1]).start()
                return c

            def wait(p, c):
                pltpu.make_async_copy(zrow, xs_hbm.at[0], sem.at[1]).wait()
                return c

            lax.fori_loop(lo, hi, start, 0)
            lax.fori_loop(lo, hi, wait, 0)
            return carry

        lax.fori_loop(0, N_EXPERTS, per_expert, 0)

    for r in range(tm):
        for k in range(TOP_K):
            pltpu.make_async_copy(x_ref.at[r], xs_hbm.at[dest_ref[0, k, r]], sem.at[0]).start(priority=k % 2)
    for k in range(TOP_K):
        pltpu.make_async_copy(x_ref, xs_hbm.at[pl.ds(0, tm)], sem.at[0]).wait()


def _dispatch_call(pad_lo, pad_hi, dest3, xpk, n_rows):
    t = xpk.shape[0]
    tm = dest3.shape[2]
    grid_spec = pltpu.PrefetchScalarGridSpec(
        num_scalar_prefetch=2,
        grid=(t // tm,),
        in_specs=[pl.BlockSpec((1, TOP_K, tm), lambda i, lo, hi: (i, 0, 0), memory_space=pltpu.SMEM),
                  pl.BlockSpec((tm, SLAB, LANES), lambda i, lo, hi: (i, 0, 0))],
        out_specs=pl.BlockSpec(memory_space=pl.ANY),
        scratch_shapes=[pltpu.VMEM((SLAB, LANES), xpk.dtype), pltpu.SemaphoreType.DMA((2,))],
    )
    return pl.pallas_call(
        functools.partial(_dispatch_kernel, tm=tm),
        grid_spec=grid_spec,
        out_shape=jax.ShapeDtypeStruct((n_rows, SLAB, LANES), xpk.dtype),
        compiler_params=_cparams("arbitrary"),
        name="dispatch",
    )(pad_lo, pad_hi, dest3, xpk)


def _expert_kernel(blk_e_ref, nused_ref, x_ref, wgu_ref, wdn_ref, o_ref, wgu_b, wdn_b, *, bm):
    b = pl.program_id(0)

    @pl.when(jnp.logical_or(b == 0, blk_e_ref[b] != blk_e_ref[jnp.maximum(b - 1, 0)]))
    def _():
        wgu_b[...] = wgu_ref[0, 0].astype(BF16)
        wdn_b[...] = wdn_ref[0, 0].astype(BF16)

    @pl.when(b < nused_ref[0])
    def _():
        x = _load_slabs_bf16(x_ref, bm)
        y = None
        for c0 in range(0, D_EXPERT, FFN_SLICE):
            g = jnp.dot(x, wgu_b[:, c0:c0 + FFN_SLICE], preferred_element_type=F32)
            u = jnp.dot(x, wgu_b[:, D_EXPERT + c0:D_EXPERT + c0 + FFN_SLICE], preferred_element_type=F32)
            hmid = (g * jax.nn.sigmoid(g) * u).astype(BF16)
            part = jnp.dot(hmid, wdn_b[c0:c0 + FFN_SLICE, :], preferred_element_type=F32)
            y = part if y is None else y + part
        o_ref[...] = y

    @pl.when(b >= nused_ref[0])
    def _():
        o_ref[...] = jnp.zeros_like(o_ref)


def _expert_call(blk_e, n_used, xs, wgu, wdn, bm, layer):
    n_blocks = xs.shape[0] // (bm * SLAB)
    d = wdn.shape[-1]

    def x_block(b, be, nu):
        return (jnp.minimum(b, jnp.maximum(nu[0] - 1, 0)), 0)

    grid_spec = pltpu.PrefetchScalarGridSpec(
        num_scalar_prefetch=2,
        grid=(n_blocks,),
        in_specs=[pl.BlockSpec((bm * SLAB, LANES), x_block),
                  pl.BlockSpec((1, 1, d, 2 * D_EXPERT), lambda b, be, nu: (layer, be[b], 0, 0)),
                  pl.BlockSpec((1, 1, D_EXPERT, d), lambda b, be, nu: (layer, be[b], 0, 0))],
        out_specs=pl.BlockSpec((bm, d), lambda b, be, nu: (b, 0)),
        scratch_shapes=[pltpu.VMEM((d, 2 * D_EXPERT), BF16), pltpu.VMEM((D_EXPERT, d), BF16)],
    )
    return pl.pallas_call(
        functools.partial(_expert_kernel, bm=bm),
        grid_spec=grid_spec,
        out_shape=jax.ShapeDtypeStruct((n_blocks * bm, d), F32),
        compiler_params=_cparams("arbitrary"),
        name="experts",
    )(blk_e, n_used, xs, wgu, wdn)


def _combine_kernel(dest_ref, dest_next_ref, gate_ref, x_ref, xb_ref, ys_hbm, wsg_ref, wsd_ref, g_ref, b_ref,
                    *rest, tm, n1, nsteps):
    if n1 is None:
        o_ref, ob_ref, buf0, buf1, base_ref, sem = rest
    else:
        o1_ref, o2_ref, buf0, buf1, base_ref, sem = rest
    i = pl.program_id(0)
    bufs = (buf0, buf1)

    def issue(ref, s):
        for r in range(tm):
            for k in range(TOP_K):
                pltpu.make_async_copy(ys_hbm.at[pl.ds(ref[0, k, r], 1)],
                                      bufs[s].at[pl.ds(k * tm + r, 1)], sem.at[s]).start(priority=k % 2)

    def wait(s):
        pltpu.make_async_copy(ys_hbm.at[pl.ds(0, TOP_K * tm)], bufs[s], sem.at[s]).wait()

    @pl.when(i == 0)
    def _():
        issue(dest_ref, 0)

    gu = jnp.dot(xb_ref[...], wsg_ref[...], preferred_element_type=F32)
    g = gu[:, :D_SHARED]
    u = gu[:, D_SHARED:]
    hmid = (g * jax.nn.sigmoid(g) * u).astype(BF16)
    base_ref[...] = ALPHA * x_ref[...] + jnp.dot(hmid, wsd_ref[...], preferred_element_type=F32)

    def step(cur):
        wait(cur)
        issue(dest_next_ref, 1 - cur)
        acc = gate_ref[:, 0:1] * bufs[cur][0:tm, :]
        for k in range(1, TOP_K):
            acc = acc + gate_ref[:, k:k + 1] * bufs[cur][k * tm:(k + 1) * tm, :]
        y = _layer_norm(base_ref[...] + acc, g_ref[...], b_ref[...], LN_EPS)
        if n1 is None:
            o_ref[...] = y
            ob_ref[...] = y.astype(BF16)
        else:
            @pl.when(i < n1)
            def _():
                o1_ref[...] = y

            @pl.when(i >= n1)
            def _():
                o2_ref[...] = y

    for cur in range(2):
        pl.when(lax.rem(i, 2) == cur)(functools.partial(step, cur))

    @pl.when(i == nsteps - 1)
    def _():
        wait(nsteps % 2)


def _combine_call(dest3, gate_tk, x, xb, ys, wsg, wsd, g, b, split_rows=None):
    t, d = x.shape
    tm = dest3.shape[2]
    nsteps = t // tm
    if split_rows is None:
        n1 = None
        out_specs = [pl.BlockSpec((tm, d), lambda i: (i, 0)), pl.BlockSpec((tm, d), lambda i: (i, 0))]
        out_shape = [jax.ShapeDtypeStruct((t, d), F32), jax.ShapeDtypeStruct((t, d), BF16)]
    else:
        n1 = split_rows // tm
        out_specs = [pl.BlockSpec((tm, d), lambda i: (jnp.minimum(i, n1 - 1), 0)),
                     pl.BlockSpec((tm, d), lambda i: (jnp.maximum(i - n1, 0), 0))]
        out_shape = [jax.ShapeDtypeStruct((split_rows, d), F32), jax.ShapeDtypeStruct((t - split_rows, d), F32)]
    return pl.pallas_call(
        functools.partial(_combine_kernel, tm=tm, n1=n1, nsteps=nsteps),
        grid=(nsteps,),
        in_specs=[pl.BlockSpec((1, TOP_K, tm), lambda i: (i, 0, 0), memory_space=pltpu.SMEM),
                  pl.BlockSpec((1, TOP_K, tm), lambda i: (jnp.minimum(i + 1, nsteps - 1), 0, 0),
                               memory_space=pltpu.SMEM),
                  pl.BlockSpec((tm, TOP_K), lambda i: (i, 0)),
                  pl.BlockSpec((tm, d), lambda i: (i, 0)),
                  pl.BlockSpec((tm, d), lambda i: (i, 0)),
                  pl.BlockSpec(memory_space=pl.ANY),
                  pl.BlockSpec((d, 2 * D_SHARED), lambda i: (0, 0)),
                  pl.BlockSpec((D_SHARED, d), lambda i: (0, 0)),
                  pl.BlockSpec((1, d), lambda i: (0, 0)),
                  pl.BlockSpec((1, d), lambda i: (0, 0))],
        out_specs=out_specs,
        out_shape=out_shape,
        scratch_shapes=[pltpu.VMEM((TOP_K * tm, d), F32), pltpu.VMEM((TOP_K * tm, d), F32),
                        pltpu.VMEM((tm, d), F32), pltpu.SemaphoreType.DMA((2,))],
        compiler_params=_cparams("arbitrary"),
        name="combine_ln",
    )(dest3, dest3, gate_tk, x, xb, ys, wsg, wsd, g.reshape(1, d), b.reshape(1, d))


def _dispatch_tables(ids, rank, counts_f, bm):
    k, t = ids.shape
    tk = k * t
    n_blocks = -(-tk // bm) + N_EXPERTS
    counts = counts_f[:, 0].astype(I32)
    padded = (counts + bm - 1) // bm * bm
    pends = jnp.cumsum(padded)
    pstart = pends - padded
    experts = jnp.arange(N_EXPERTS, dtype=I32)
    dest = jnp.sum(jnp.where(ids[:, :, None] == experts, pstart, 0), axis=-1) + rank
    blk_first = jnp.arange(n_blocks, dtype=I32) * bm
    blk_e = jnp.minimum(jnp.sum(pends[None, :] <= blk_first[:, None], axis=1), N_EXPERTS - 1).astype(I32)
    n_used = (pends[-1:] // bm).astype(I32)
    return dest, blk_e, n_used, (pstart + counts).astype(I32), pends.astype(I32), n_blocks * bm


def _layer(x, xb, memb, seq, layer, w_in, b_gate, conv_w, conv_b, mh_norm_g, pool_w, pool_scale, w_mem_kv, w_out,
           ln1_g, ln1_b, w_router, b_router, w_gu_all, w_dn_all, w_sh_gu, w_sh_dn, ln2_g, ln2_b, last):
    t = x.shape[0]
    hin = _mm_call(xb, w_in[:, :OFF_G].astype(BF16), F32, "in_proj")
    g_rows, g_cols = _gates_call(xb, _gate_rows(w_in[:, OFF_G:].T).astype(BF16), _gate_rows(b_gate))
    gr3 = g_rows.reshape(3 * GDIR, t // CHUNK, CHUNK).transpose(1, 0, 2)
    q = _conv_call(hin, conv_w, conv_b, seq, col0=OFF_Q, scale=1.0, transpose_out=False)
    kt = _conv_call(hin, conv_w, conv_b, seq, col0=OFF_K, scale=M_HEAD_DIM ** -0.5, transpose_out=True)
    h_f, h_b = _mlstm_call(q, kt, hin, gr3, g_cols, seq)
    m_out = _headnorm_call(h_f, h_b, hin, mh_norm_g)
    kv = _mm_call(memb, w_mem_kv.astype(BF16), BF16, "mem_kv").reshape(seq.nseq, -1, 2 * X_WIDTH)
    pa_out = _pool_attn_call(hin, kv, pool_w.astype(BF16), pool_scale, seq)
    x1, x1b, x1p = _outproj_call(m_out, pa_out, x, w_out.astype(BF16), ln1_g, ln1_b)
    ids, gate, rank, counts = _router_call(x1b, w_router.T.astype(BF16), b_router)
    bm = MOE_BM
    dest, blk_e, n_used, pad_lo, pad_hi, n_rows = _dispatch_tables(ids, rank, counts, bm)
    ctm = min(COMB_TM, t)
    dest3 = dest.reshape(TOP_K, t // ctm, ctm).transpose(1, 0, 2)
    dtm = min(DISP_TM, t)
    dest3d = dest.reshape(TOP_K, t // dtm, dtm).transpose(1, 0, 2)
    xs = _dispatch_call(pad_lo, pad_hi, dest3d, x1p.reshape(t, SLAB, LANES), n_rows)
    ys = _expert_call(blk_e, n_used, xs.reshape(n_rows * SLAB, LANES), w_gu_all, w_dn_all, bm, layer)
    return _combine_call(dest3, gate.T, x1, x1b, ys, w_sh_gu.astype(BF16), w_sh_dn.astype(BF16), ln2_g, ln2_b,
                         split_rows=seq.t1 if last else None)


def kernel(x_prompt, x_sample, mem_prompt, mem_sample, ln_in_g, ln_in_b, w_in, b_gate, conv_w, conv_b, mh_norm_g, pool_w, pool_scale, w_mem_kv, w_out, ln1_g, ln1_b, w_router, b_router, w_gu, w_dn, w_sh_gu, w_sh_dn, ln2_g, ln2_b):
    b1, s1, d = x_prompt.shape
    b2, s2, _ = x_sample.shape
    seq = SeqInfo(b1, s1, b2, s2)
    memb = jnp.concatenate([mem_prompt.reshape(-1, d), mem_sample.reshape(-1, d)], axis=0).astype(BF16)
    x, xb = _ln_call(x_prompt.reshape(b1 * s1, d), x_sample.reshape(b2 * s2, d), ln_in_g, ln_in_b)
    depth = w_in.shape[0]
    for l in range(depth):
        x, xb = _layer(x, xb, memb, seq, l, w_in[l], b_gate[l], conv_w[l], conv_b[l], mh_norm_g[l], pool_w[l],
                       pool_scale[l], w_mem_kv[l], w_out[l], ln1_g[l], ln1_b[l], w_router[l], b_router[l],
                       w_gu, w_dn, w_sh_gu[l], w_sh_dn[l], ln2_g[l], ln2_b[l], last=l == depth - 1)
    return (x.reshape(b1, s1, d), xb.reshape(b2, s2, d))
```

```python
import functools

import jax
import jax.numpy as jnp
from jax import lax
from jax.experimental import pallas as pl
from jax.experimental.pallas import tpu as pltpu

F32 = jnp.float32
BF16 = jnp.bfloat16
I32 = jnp.int32

DEPTH = 4
D_MODEL = 2048
M_HEADS = 4
M_WIDTH = 1024
M_HEAD_DIM = 256
CHUNK = 128
CONV_K = 5
POOL_WINDOWS = (2, 4, 8, 16)
POOL_WIDTH = 512
POOL_GROUP = 128
X_HEADS = 4
X_WIDTH = 512
X_HEAD_DIM = 128
OFF_Q = 0
OFF_K = OFF_Q + M_WIDTH
OFF_V = OFF_K + M_WIDTH
OFF_O = OFF_V + M_WIDTH
OFF_P = OFF_O + M_WIDTH
OFF_XQ = OFF_P + POOL_WIDTH
OFF_G = OFF_XQ + X_WIDTH
N_GATES = 4 * M_HEADS
N_EXPERTS = 64
TOP_K = 8
N_GROUPS = 8
GROUP_SIZE = N_EXPERTS // N_GROUPS
TOPK_GROUPS = 4
D_EXPERT = 512
D_SHARED = 512
ROUTED_SCALE = 2.5
ALPHA = (2 * DEPTH) ** 0.25
LN_EPS = 1e-5
HN_EPS = 1e-6

LANES = 128
SUBLANES = 8
HALO = SUBLANES
VMEM_LIMIT = 58 * 1024 * 1024

LN_TM = 512
MM_TM = 1024
MM_TN = 1024
SEQ_TM = 512
MLSTM_CB = 4
OUT_TM = 512
OUT_SUB = 256
ROUTE_TM = 512
MOE_BM = 512
FFN_SLICE = 256
DISP_TM = 256
COMB_TM = 128


def _cparams(*sem):
    return pltpu.CompilerParams(dimension_semantics=sem, vmem_limit_bytes=VMEM_LIMIT)


class SeqInfo:
    def __init__(self, b1, s1, b2, s2):
        self.b1, self.s1, self.b2, self.s2 = b1, s1, b2, s2
        self.t1 = b1 * s1
        self.t = b1 * s1 + b2 * s2
        self.nseq = b1 + b2

    def pos_len(self, row0):
        in1 = row0 < self.t1
        pos = jnp.where(in1, lax.rem(row0, self.s1), lax.rem(row0 - self.t1, self.s2))
        slen = jnp.where(in1, self.s1, self.s2)
        return pos, slen

    def seq_id(self, row0):
        in1 = row0 < self.t1
        return jnp.where(in1, row0 // self.s1, self.b1 + (row0 - self.t1) // self.s2)


def _layer_norm(y, g, b, eps):
    mu = jnp.mean(y, axis=-1, keepdims=True)
    d = y - mu
    var = jnp.mean(d * d, axis=-1, keepdims=True)
    return d * lax.rsqrt(var + eps) * g + b


def _ln_kernel(x1_ref, x2_ref, g_ref, b_ref, o_ref, ob_ref, *, n1):
    x = jnp.where(pl.program_id(0) < n1, x1_ref[...], x2_ref[...])
    y = _layer_norm(x, g_ref[...], b_ref[...], LN_EPS)
    o_ref[...] = y
    ob_ref[...] = y.astype(BF16)


def _ln_call(x1, x2, g, b):
    t1, d = x1.shape
    t2 = x2.shape[0]
    tm = min(LN_TM, t1, t2)
    n1, n2 = t1 // tm, t2 // tm
    t = t1 + t2
    return pl.pallas_call(
        functools.partial(_ln_kernel, n1=n1),
        grid=(n1 + n2,),
        in_specs=[pl.BlockSpec((tm, d), lambda i: (jnp.minimum(i, n1 - 1), 0)),
                  pl.BlockSpec((tm, d), lambda i: (jnp.maximum(i - n1, 0), 0)),
                  pl.BlockSpec((1, d), lambda i: (0, 0)),
                  pl.BlockSpec((1, d), lambda i: (0, 0))],
        out_specs=[pl.BlockSpec((tm, d), lambda i: (i, 0)),
                   pl.BlockSpec((tm, d), lambda i: (i, 0))],
        out_shape=[jax.ShapeDtypeStruct((t, d), F32), jax.ShapeDtypeStruct((t, d), BF16)],
        compiler_params=_cparams("arbitrary"),
        name="ln_in",
    )(x1, x2, g.reshape(1, d), b.reshape(1, d))


def _mm_kernel(x_ref, w_ref, o_ref):
    o_ref[...] = jnp.dot(x_ref[...], w_ref[...], preferred_element_type=F32).astype(o_ref.dtype)


def _mm_call(xb, wb, out_dtype, name):
    m, k = xb.shape
    n = wb.shape[1]
    tm = min(MM_TM, m)
    while m % tm:
        tm //= 2
    tn = min(MM_TN, n)
    return pl.pallas_call(
        _mm_kernel,
        grid=(n // tn, m // tm),
        in_specs=[pl.BlockSpec((tm, k), lambda j, i: (i, 0)),
                  pl.BlockSpec((k, tn), lambda j, i: (0, j))],
        out_specs=pl.BlockSpec((tm, tn), lambda j, i: (i, j)),
        out_shape=jax.ShapeDtypeStruct((m, n), out_dtype),
        compiler_params=_cparams("parallel", "parallel"),
        name=name,
    )(xb, wb)


GDIR = 2 * M_HEADS


def _chunk_scan(x, op, reverse):
    n = x.shape[1]
    pos = lax.rem(lax.broadcasted_iota(I32, x.shape, 1), CHUNK)
    s = 1
    while s < CHUNK:
        if reverse:
            x = jnp.where(pos < CHUNK - s, op(x, pltpu.roll(x, n - s, axis=1)), x)
        else:
            x = jnp.where(pos >= s, op(x, pltpu.roll(x, s, axis=1)), x)
        s *= 2
    return x


def _gates_kernel(x_ref, w_ref, b_ref, row_ref, col_ref):
    g = lax.dot_general(w_ref[...], x_ref[...], (((1,), (1,)), ((), ())),
                        preferred_element_type=F32) + b_ref[...]
    gi = g[0:GDIR]
    logf = jax.nn.log_sigmoid(g[GDIR:2 * GDIR])
    is_fwd = lax.broadcasted_iota(I32, gi.shape, 0) < M_HEADS
    pre = _chunk_scan(logf, jnp.add, False)
    suf = _chunk_scan(logf, jnp.add, True)
    b = jnp.where(is_fwd, pre, suf)
    total = pre + suf - logf
    ib = gi - b
    cmax = jnp.where(is_fwd, _chunk_scan(ib, jnp.maximum, False), _chunk_scan(ib, jnp.maximum, True))
    lw = total + ib
    lw_max = jnp.maximum(_chunk_scan(lw, jnp.maximum, False), _chunk_scan(lw, jnp.maximum, True))
    row_ref[...] = jnp.concatenate([ib, total, lw_max], axis=0)
    pad = jnp.zeros((LANES - 3 * GDIR, gi.shape[1]), F32)
    col_ref[...] = jnp.concatenate([b, cmax, lw, pad], axis=0).T


def _gates_call(xb, wgt, bg):
    t, d = xb.shape
    tm = min(MM_TM, t)
    return pl.pallas_call(
        _gates_kernel,
        grid=(t // tm,),
        in_specs=[pl.BlockSpec((tm, d), lambda i: (i, 0)),
                  pl.BlockSpec((N_GATES, d), lambda i: (0, 0)),
                  pl.BlockSpec((N_GATES, 1), lambda i: (0, 0))],
        out_specs=[pl.BlockSpec((3 * GDIR, tm), lambda i: (0, i)),
                   pl.BlockSpec((tm, LANES), lambda i: (i, 0))],
        out_shape=[jax.ShapeDtypeStruct((3 * GDIR, t), F32), jax.ShapeDtypeStruct((t, LANES), F32)],
        compiler_params=_cparams("parallel"),
        name="gates",
    )(xb, wgt, bg.reshape(N_GATES, 1))


def _gate_rows(a):
    h = M_HEADS
    return jnp.concatenate([a[0:h], a[2 * h:3 * h], a[h:2 * h], a[3 * h:4 * h]], axis=0)


def _halo_specs(tm, width, col_block, n_rows):
    per = tm // HALO
    last = n_rows // HALO - 1
    prev = pl.BlockSpec((HALO, width), lambda i, *_: (jnp.maximum(i * per - 1, 0), col_block))
    cur = pl.BlockSpec((tm, width), lambda i, *_: (i, col_block))
    nxt = pl.BlockSpec((HALO, width), lambda i, *_: (jnp.minimum((i + 1) * per, last), col_block))
    return prev, cur, nxt


def _extended_tile(prev_ref, cur_ref, next_ref, tm, seq):
    row0 = pl.program_id(0) * tm
    pos, slen = seq.pos_len(row0)
    prev = jnp.where(pos == 0, 0.0, prev_ref[...])
    nxt = jnp.where(pos + tm == slen, 0.0, next_ref[...])
    return jnp.concatenate([prev, cur_ref[...], nxt], axis=0), pos, slen


def _shift_rows(xe, d):
    n = xe.shape[0]
    if d == 0:
        return xe
    return pltpu.roll(xe, (n - d) % n, axis=0)


def _conv_kernel(prev_ref, cur_ref, next_ref, w_ref, b_ref, o_ref, *, tm, tc, seq, scale, transpose_out):
    xe, _, _ = _extended_tile(prev_ref, cur_ref, next_ref, tm, seq)
    acc = jnp.broadcast_to(b_ref[...], (tm, tc))
    for j in range(CONV_K):
        sh = _shift_rows(xe, j - CONV_K // 2)
        acc = acc + sh[HALO:HALO + tm] * w_ref[j:j + 1, :]
    y = acc * jax.nn.sigmoid(acc) * scale
    o_ref[...] = (y.T if transpose_out else y).astype(BF16)


def _conv_call(hin, conv_w, conv_b, seq, *, col0, scale, transpose_out):
    t = hin.shape[0]
    tm = min(SEQ_TM, seq.s1, seq.s2)
    tc = 512
    j0 = col0 // tc
    prev = pl.BlockSpec((HALO, tc), lambda i, j: (jnp.maximum(i * (tm // HALO) - 1, 0), j + j0))
    cur = pl.BlockSpec((tm, tc), lambda i, j: (i, j + j0))
    nxt = pl.BlockSpec((HALO, tc), lambda i, j: (jnp.minimum((i + 1) * (tm // HALO), t // HALO - 1), j + j0))
    if transpose_out:
        out_spec = pl.BlockSpec((tc, tm), lambda i, j: (j, i))
        out_shape = jax.ShapeDtypeStruct((M_WIDTH, t), BF16)
    else:
        out_spec = pl.BlockSpec((tm, tc), lambda i, j: (i, j))
        out_shape = jax.ShapeDtypeStruct((t, M_WIDTH), BF16)
    return pl.pallas_call(
        functools.partial(_conv_kernel, tm=tm, tc=tc, seq=seq, scale=scale, transpose_out=transpose_out),
        grid=(t // tm, M_WIDTH // tc),
        in_specs=[prev, cur, nxt,
                  pl.BlockSpec((CONV_K, tc), lambda i, j: (0, j + j0)),
                  pl.BlockSpec((1, tc), lambda i, j: (0, j + j0))],
        out_specs=out_spec,
        out_shape=out_shape,
        compiler_params=_cparams("parallel", "parallel"),
        name="conv_silu_t" if transpose_out else "conv_silu",
    )(hin, hin, hin, conv_w, conv_b.reshape(1, 2 * M_WIDTH))


def _mlstm_kernel(qf, ktf, vf, grf, gcf, qb, ktb, vb, grb, gcb, of_ref, ob_ref, cf, mf, cb_ref, mb, *, cb, seq):
    @pl.when(pl.program_id(0) == 0)
    def _():
        for ref in (cf, mf, cb_ref, mb):
            ref[...] = jnp.zeros_like(ref)

    _mlstm_direction(qf, ktf, vf, grf, gcf, of_ref, cf, mf, reverse=False, cb=cb, seq=seq)
    _mlstm_direction(qb, ktb, vb, grb, gcb, ob_ref, cb_ref, mb, reverse=True, cb=cb, seq=seq)


def _mlstm_direction(q_ref, kt_ref, v_ref, gr_ref, gc_ref, o_ref, c_ref, m_ref, *, reverse, cb, seq):
    L, DK, DV = CHUNK, M_HEAD_DIM, M_HEAD_DIM
    step = pl.program_id(0)
    nsteps = pl.num_programs(0)
    blk = (nsteps - 1 - step) if reverse else step
    ti = lax.broadcasted_iota(I32, (L, L), 0)
    si = lax.broadcasted_iota(I32, (L, L), 1)
    causal = (si >= ti) if reverse else (si <= ti)
    slot0 = M_HEADS if reverse else 0

    m_vals = [m_ref[h, 0:1, 0:1] for h in range(M_HEADS)]
    for jj in range(cb):
        j = (cb - 1 - jj) if reverse else jj
        rows = slice(j * L, (j + 1) * L)
        pos, slen = seq.pos_len((blk * cb + j) * L)
        at_start = (pos + L == slen) if reverse else (pos == 0)
        grow = gr_ref[j]
        ones = jnp.ones((L, LANES), F32)
        for h in range(M_HEADS):
            cols = slice(h * DK, (h + 1) * DK)
            sl = slot0 + h
            q = q_ref[rows, cols]
            kt = kt_ref[cols, rows]
            v32 = v_ref[rows, cols]
            ib_row = grow[sl:sl + 1, :]
            g_tot = grow[GDIR + sl:GDIR + sl + 1, 0:1]
            lw_max = grow[2 * GDIR + sl:2 * GDIR + sl + 1, 0:1]
            b_col = gc_ref[rows, sl:sl + 1]
            cmax_col = gc_ref[rows, GDIR + sl:GDIR + sl + 1]
            lw_col = gc_ref[rows, 2 * GDIR + sl:2 * GDIR + sl + 1]
            m_prev = jnp.where(at_start, 0.0, m_vals[h])
            m_rel = jnp.maximum(m_prev, cmax_col)
            m_t = b_col + m_rel
            dexp = jnp.exp(jnp.where(causal, ib_row - m_rel, -jnp.inf))
            s = jnp.dot(q, kt, preferred_element_type=F32)
            a = (s * dexp).astype(BF16)
            inter = jnp.exp(m_prev - m_rel)
            c_aug = jnp.where(at_start, 0.0, c_ref[h])
            qc = jnp.dot(q, c_aug.astype(BF16), preferred_element_type=F32)
            v_ext = jnp.concatenate([v32, ones], axis=1).astype(BF16)
            nd = jnp.dot(a, v_ext, preferred_element_type=F32) + inter * qc
            den = nd[:, DV:DV + 1]
            hh = nd[:, :DV] / jnp.maximum(jnp.abs(den), jnp.exp(-m_t))
            m_new = jnp.maximum(g_tot + m_prev, lw_max)
            wk_col = jnp.exp(lw_col - m_new)
            decay = jnp.exp(g_tot + m_prev - m_new)
            v_aug = (jnp.concatenate([v32, ones], axis=1) * wk_col).astype(BF16)
            upd = jnp.dot(kt, v_aug, preferred_element_type=F32)
            c_ref[h] = decay * c_aug + upd
            m_vals[h] = m_new
            o_ref[rows, cols] = hh
    for h in range(M_HEADS):
        m_ref[h] = jnp.broadcast_to(m_vals[h], (SUBLANES, LANES))


def _mlstm_call(q, kt, hin, gr3, gc, seq):
    t = q.shape[0]
    cb = MLSTM_CB
    while (seq.s1 // CHUNK) % cb or (seq.s2 // CHUNK) % cb:
        cb //= 2
    rows = cb * CHUNK
    nsteps = t // rows
    vblk = OFF_V // M_WIDTH

    def specs(blk):
        return [pl.BlockSpec((rows, M_WIDTH), lambda s: (blk(s), 0)),
                pl.BlockSpec((M_WIDTH, rows), lambda s: (0, blk(s))),
                pl.BlockSpec((rows, M_WIDTH), lambda s: (blk(s), vblk)),
                pl.BlockSpec((cb, 3 * GDIR, CHUNK), lambda s: (blk(s), 0, 0)),
                pl.BlockSpec((rows, LANES), lambda s: (blk(s), 0))]

    fwd = lambda s: s
    bwd = lambda s: nsteps - 1 - s
    state = [pltpu.VMEM((M_HEADS, M_HEAD_DIM, M_HEAD_DIM + LANES), F32), pltpu.VMEM((M_HEADS, SUBLANES, LANES), F32)]
    return pl.pallas_call(
        functools.partial(_mlstm_kernel, cb=cb, seq=seq),
        grid=(nsteps,),
        in_specs=specs(fwd) + specs(bwd),
        out_specs=[pl.BlockSpec((rows, M_WIDTH), lambda s: (fwd(s), 0)),
                   pl.BlockSpec((rows, M_WIDTH), lambda s: (bwd(s), 0))],
        out_shape=[jax.ShapeDtypeStruct((t, M_WIDTH), F32), jax.ShapeDtypeStruct((t, M_WIDTH), F32)],
        scratch_shapes=state + state,
        compiler_params=_cparams("arbitrary"),
        name="mlstm",
    )(q, kt, hin, gr3, gc, q, kt, hin, gr3, gc)


def _headnorm_kernel(hf_ref, hb_ref, uo_ref, ng_ref, o_ref):
    for h in range(M_HEADS):
        cols = slice(h * M_HEAD_DIM, (h + 1) * M_HEAD_DIM)
        hs = hf_ref[:, cols] + hb_ref[:, cols]
        mu = jnp.mean(hs, axis=1, keepdims=True)
        d = hs - mu
        var = jnp.mean(d * d, axis=1, keepdims=True)
        hn = d * lax.rsqrt(var + HN_EPS) * ng_ref[:, cols]
        o_ref[:, cols] = (hn * jax.nn.sigmoid(uo_ref[:, cols])).astype(o_ref.dtype)


def _headnorm_call(h_f, h_b, hin, norm_g):
    t = h_f.shape[0]
    tm = min(LN_TM, t)
    return pl.pallas_call(
        _headnorm_kernel,
        grid=(t // tm,),
        in_specs=[pl.BlockSpec((tm, M_WIDTH), lambda i: (i, 0)),
                  pl.BlockSpec((tm, M_WIDTH), lambda i: (i, 0)),
                  pl.BlockSpec((tm, M_WIDTH), lambda i: (i, OFF_O // M_WIDTH)),
                  pl.BlockSpec((1, M_WIDTH), lambda i: (0, 0))],
        out_specs=pl.BlockSpec((tm, M_WIDTH), lambda i: (i, 0)),
        out_shape=jax.ShapeDtypeStruct((t, M_WIDTH), BF16),
        compiler_params=_cparams("parallel"),
        name="headnorm",
    )(h_f, h_b, hin, norm_g.reshape(1, M_WIDTH))


def _pool_attn_kernel(prev_ref, cur_ref, next_ref, xq_ref, kv_ref, pw_ref, ps_ref, o_ref, *, tm, seq):
    xe, pos0, slen = _extended_tile(prev_ref, cur_ref, next_ref, tm, seq)
    pos = pos0 + lax.broadcasted_iota(I32, (tm, 1), 0)
    for gi, w in enumerate(POOL_WINDOWS):
        cols = slice(gi * POOL_GROUP, (gi + 1) * POOL_GROUP)
        u = xe[:, cols]
        win = _shift_rows(u, -1) + u
        half = 1
        while 2 * half < w:
            win = _shift_rows(win, -half) + _shift_rows(win, half)
            half *= 2
        cnt = (jnp.minimum(pos + w // 2, slen) - jnp.maximum(pos - w // 2, 0)).astype(F32)
        mean = win[HALO:HALO + tm] / cnt
        diff = (mean - u[HALO:HALO + tm]).astype(BF16)
        y = jnp.dot(diff, pw_ref[gi], preferred_element_type=F32)
        o_ref[:, cols] = (y * ps_ref[:, cols]).astype(o_ref.dtype)
    xq = xq_ref[...].astype(BF16)
    for h in range(X_HEADS):
        cols = slice(h * X_HEAD_DIM, (h + 1) * X_HEAD_DIM)
        kh = kv_ref[0, :, h * X_HEAD_DIM:(h + 1) * X_HEAD_DIM]
        vh = kv_ref[0, :, X_WIDTH + h * X_HEAD_DIM:X_WIDTH + (h + 1) * X_HEAD_DIM]
        s = lax.dot_general(xq[:, cols], kh, (((1,), (1,)), ((), ())),
                            preferred_element_type=F32) * (X_HEAD_DIM ** -0.5)
        e = jnp.exp(s - jnp.max(s, axis=1, keepdims=True))
        p = e / jnp.sum(e, axis=1, keepdims=True)
        y = jnp.dot(p.astype(BF16), vh, preferred_element_type=F32)
        o_ref[:, POOL_WIDTH + h * X_HEAD_DIM:POOL_WIDTH + (h + 1) * X_HEAD_DIM] = y.astype(o_ref.dtype)


def _pool_attn_call(hin, kv, pool_w, pool_scale, seq):
    t = hin.shape[0]
    tm = min(SEQ_TM, seq.s1, seq.s2)
    n_mem = kv.shape[1]
    prev, cur, nxt = _halo_specs(tm, POOL_WIDTH, OFF_P // POOL_WIDTH, t)
    return pl.pallas_call(
        functools.partial(_pool_attn_kernel, tm=tm, seq=seq),
        grid=(t // tm,),
        in_specs=[prev, cur, nxt,
                  pl.BlockSpec((tm, X_WIDTH), lambda i: (i, OFF_XQ // X_WIDTH)),
                  pl.BlockSpec((1, n_mem, 2 * X_WIDTH), lambda i: (seq.seq_id(i * tm), 0, 0)),
                  pl.BlockSpec((len(POOL_WINDOWS), POOL_GROUP, POOL_GROUP), lambda i: (0, 0, 0)),
                  pl.BlockSpec((1, POOL_WIDTH), lambda i: (0, 0))],
        out_specs=pl.BlockSpec((tm, POOL_WIDTH + X_WIDTH), lambda i: (i, 0)),
        out_shape=jax.ShapeDtypeStruct((t, POOL_WIDTH + X_WIDTH), BF16),
        compiler_params=_cparams("parallel"),
        name="pool_attn",
    )(hin, hin, hin, hin, kv, pool_w, pool_scale.reshape(1, POOL_WIDTH))


def _pack_halves(yb):
    n = yb.shape[1] // 2
    bits = pltpu.bitcast(yb.astype(F32), jnp.uint32)
    return jnp.bitwise_or(bits[:, n:], jnp.right_shift(bits[:, :n], jnp.uint32(16)))


SLAB = SUBLANES


def _store_slabs(o_ref, packed, row0=0):
    rows = packed.shape[0]
    for s in range(SLAB):
        o_ref[pl.ds(row0 * SLAB + s, rows, stride=SLAB), :] = packed[:, s * LANES:(s + 1) * LANES]


def _load_slabs_bf16(x_ref, rows):
    words = [x_ref[pl.ds(s, rows, stride=SLAB), :] for s in range(SLAB)]
    lo = [pltpu.bitcast(jnp.left_shift(w, jnp.uint32(16)), F32) for w in words]
    hi = [pltpu.bitcast(jnp.bitwise_and(w, jnp.uint32(0xFFFF0000)), F32) for w in words]
    return jnp.concatenate(lo + hi, axis=1).astype(BF16)


def _outproj_kernel(m_ref, pa_ref, x_ref, w_ref, g_ref, b_ref, o_ref, ob_ref, op_ref, *, tm, sub):
    for r0 in range(0, tm, sub):
        rows = slice(r0, r0 + sub)
        acc = jnp.dot(m_ref[rows, :], w_ref[0:M_WIDTH, :], preferred_element_type=F32)
        acc = acc + jnp.dot(pa_ref[rows, :], w_ref[M_WIDTH:, :], preferred_element_type=F32)
        y = _layer_norm(ALPHA * x_ref[rows, :] + acc, g_ref[...], b_ref[...], LN_EPS)
        o_ref[rows, :] = y
        yb = y.astype(BF16)
        ob_ref[rows, :] = yb
        _store_slabs(op_ref, _pack_halves(yb), r0)


def _outproj_call(m_out, pa_out, x, w_out, g, b):
    t, d = x.shape
    tm = min(OUT_TM, t)
    return pl.pallas_call(
        functools.partial(_outproj_kernel, tm=tm, sub=min(OUT_SUB, tm)),
        grid=(t // tm,),
        in_specs=[pl.BlockSpec((tm, M_WIDTH), lambda i: (i, 0)),
                  pl.BlockSpec((tm, POOL_WIDTH + X_WIDTH), lambda i: (i, 0)),
                  pl.BlockSpec((tm, d), lambda i: (i, 0)),
                  pl.BlockSpec((d, d), lambda i: (0, 0)),
                  pl.BlockSpec((1, d), lambda i: (0, 0)),
                  pl.BlockSpec((1, d), lambda i: (0, 0))],
        out_specs=[pl.BlockSpec((tm, d), lambda i: (i, 0)),
                   pl.BlockSpec((tm, d), lambda i: (i, 0)),
                   pl.BlockSpec((tm * SLAB, LANES), lambda i: (i, 0))],
        out_shape=[jax.ShapeDtypeStruct((t, d), F32), jax.ShapeDtypeStruct((t, d), BF16),
                   jax.ShapeDtypeStruct((t * SLAB, LANES), jnp.uint32)],
        compiler_params=_cparams("parallel"),
        name="outproj_ln",
    )(m_out, pa_out, x, w_out, g.reshape(1, d), b.reshape(1, d))


def _first_index_of_max(vals, idx, axis_size):
    mx = jnp.max(vals, axis=0, keepdims=True)
    first = jnp.min(jnp.where(vals == mx, idx, axis_size), axis=0, keepdims=True)
    return mx, first


def _router_kernel(x_ref, w_ref, b_ref, ids_ref, gate_ref, rank_ref, cnt_ref, carry_ref, *, tm):
    @pl.when(pl.program_id(0) == 0)
    def _():
        carry_ref[...] = jnp.zeros_like(carry_ref)

    logits = lax.dot_general(w_ref[...], x_ref[...], (((1,), (1,)), ((), ())),
                             preferred_element_type=F32)
    s = jax.nn.sigmoid(logits)
    sel = s + b_ref[...]
    neg = -jnp.inf
    sub = lax.broadcasted_iota(I32, (GROUP_SIZE, tm), 0)
    gscores = []
    for g in range(N_GROUPS):
        blk = sel[g * GROUP_SIZE:(g + 1) * GROUP_SIZE, :]
        m1, i1 = _first_index_of_max(blk, sub, GROUP_SIZE)
        m2 = jnp.max(jnp.where(sub == i1, neg, blk), axis=0, keepdims=True)
        gscores.append(m1 + m2)
    gs = jnp.concatenate(gscores, axis=0)
    gidx = lax.broadcasted_iota(I32, (N_GROUPS, tm), 0)
    gmask = jnp.zeros((N_GROUPS, tm), jnp.bool_)
    for _ in range(TOPK_GROUPS):
        _, gi = _first_index_of_max(gs, gidx, N_GROUPS)
        hit = gidx == gi
        gmask = jnp.logical_or(gmask, hit)
        gs = jnp.where(hit, neg, gs)
    gmask_f = gmask.astype(F32)
    masked = jnp.concatenate(
        [jnp.where(gmask_f[g:g + 1, :] > 0.5, sel[g * GROUP_SIZE:(g + 1) * GROUP_SIZE, :], neg)
         for g in range(N_GROUPS)], axis=0)
    eidx = lax.broadcasted_iota(I32, (N_EXPERTS, tm), 0)
    chosen = jnp.zeros((N_EXPERTS, tm), F32)
    ids, gates, hits = [], [], []
    for _ in range(TOP_K):
        _, ei = _first_index_of_max(masked, eidx, N_EXPERTS)
        hit = eidx == ei
        ids.append(ei)
        gates.append(jnp.sum(jnp.where(hit, s, 0.0), axis=0, keepdims=True))
        hits.append(hit)
        chosen = jnp.where(hit, 1.0, chosen)
        masked = jnp.where(hit, neg, masked)
    gate = jnp.concatenate(gates, axis=0)
    gate = gate / jnp.sum(gate, axis=0, keepdims=True) * ROUTED_SCALE
    ids_ref[...] = jnp.concatenate(ids, axis=0)
    gate_ref[...] = gate
    r = lax.broadcasted_iota(I32, (tm, tm), 0)
    c = lax.broadcasted_iota(I32, (tm, tm), 1)
    before = (r < c).astype(BF16)
    cnt = jnp.dot(chosen.astype(BF16), before, preferred_element_type=F32) + carry_ref[:, 0:1]
    ranks = [jnp.sum(jnp.where(hit, cnt, 0.0), axis=0, keepdims=True) for hit in hits]
    rank_ref[...] = jnp.concatenate(ranks, axis=0).astype(I32)
    total = carry_ref[:, 0:1] + jnp.sum(chosen, axis=1, keepdims=True)
    carry_ref[...] = jnp.broadcast_to(total, carry_ref.shape)
    cnt_ref[...] = jnp.broadcast_to(total, cnt_ref.shape)


def _router_call(xb, wrt, br):
    t, d = xb.shape
    tm = min(ROUTE_TM, t)
    return pl.pallas_call(
        functools.partial(_router_kernel, tm=tm),
        grid=(t // tm,),
        in_specs=[pl.BlockSpec((tm, d), lambda i: (i, 0)),
                  pl.BlockSpec((N_EXPERTS, d), lambda i: (0, 0)),
                  pl.BlockSpec((N_EXPERTS, 1), lambda i: (0, 0))],
        out_specs=[pl.BlockSpec((TOP_K, tm), lambda i: (0, i)),
                   pl.BlockSpec((TOP_K, tm), lambda i: (0, i)),
                   pl.BlockSpec((TOP_K, tm), lambda i: (0, i)),
                   pl.BlockSpec((N_EXPERTS, LANES), lambda i: (0, 0))],
        out_shape=[jax.ShapeDtypeStruct((TOP_K, t), I32),
                   jax.ShapeDtypeStruct((TOP_K, t), F32),
                   jax.ShapeDtypeStruct((TOP_K, t), I32),
                   jax.ShapeDtypeStruct((N_EXPERTS, LANES), F32)],
        scratch_shapes=[pltpu.VMEM((N_EXPERTS, LANES), F32)],
        compiler_params=_cparams("arbitrary"),
        name="router",
    )(xb, wrt, br.reshape(N_EXPERTS, 1))


def _dispatch_kernel(pad_lo_ref, pad_hi_ref, dest_ref, x_ref, xs_hbm, zrow, sem, *, tm):
    i = pl.program_id(0)

    @pl.when(i == 0)
    def _():
        zrow[...] = jnp.zeros_like(zrow)

        def per_expert(e, carry):
            lo, hi = pad_lo_ref[e], pad_hi_ref[e]

            def start(p, c):
                pltpu.make_async_copy(zrow, xs_hbm.at[p], sem.at[1]).start()
                return c

            def wait(p, c):
                pltpu.make_async_copy(zrow, xs_hbm.at[0], sem.at[1]).wait()
                return c

            lax.fori_loop(lo, hi, start, 0)
            lax.fori_loop(lo, hi, wait, 0)
            return carry

        lax.fori_loop(0, N_EXPERTS, per_expert, 0)

    for r in range(tm):
        for k in range(TOP_K):
            pltpu.make_async_copy(x_ref.at[r], xs_hbm.at[dest_ref[0, k, r]], sem.at[0]).start(priority=k % 2)
    for k in range(TOP_K):
        pltpu.make_async_copy(x_ref, xs_hbm.at[pl.ds(0, tm)], sem.at[0]).wait()


def _dispatch_call(pad_lo, pad_hi, dest3, xpk, n_rows):
    t = xpk.shape[0]
    tm = dest3.shape[2]
    grid_spec = pltpu.PrefetchScalarGridSpec(
        num_scalar_prefetch=2,
        grid=(t // tm,),
        in_specs=[pl.BlockSpec((1, TOP_K, tm), lambda i, lo, hi: (i, 0, 0), memory_space=pltpu.SMEM),
                  pl.BlockSpec((tm, SLAB, LANES), lambda i, lo, hi: (i, 0, 0))],
        out_specs=pl.BlockSpec(memory_space=pl.ANY),
        scratch_shapes=[pltpu.VMEM((SLAB, LANES), xpk.dtype), pltpu.SemaphoreType.DMA((2,))],
    )
    return pl.pallas_call(
        functools.partial(_dispatch_kernel, tm=tm),
        grid_spec=grid_spec,
        out_shape=jax.ShapeDtypeStruct((n_rows, SLAB, LANES), xpk.dtype),
        compiler_params=_cparams("arbitrary"),
        name="dispatch",
    )(pad_lo, pad_hi, dest3, xpk)


def _expert_kernel(blk_e_ref, nused_ref, x_ref, wgu_ref, wdn_ref, o_ref, wgu_b, wdn_b, *, bm):
    b = pl.program_id(0)

    @pl.when(jnp.logical_or(b == 0, blk_e_ref[b] != blk_e_ref[jnp.maximum(b - 1, 0)]))
    def _():
        wgu_b[...] = wgu_ref[0, 0].astype(BF16)
        wdn_b[...] = wdn_ref[0, 0].astype(BF16)

    @pl.when(b < nused_ref[0])
    def _():
        x = _load_slabs_bf16(x_ref, bm)
        y = None
        for c0 in range(0, D_EXPERT, FFN_SLICE):
            g = jnp.dot(x, wgu_b[:, c0:c0 + FFN_SLICE], preferred_element_type=F32)
            u = jnp.dot(x, wgu_b[:, D_EXPERT + c0:D_EXPERT + c0 + FFN_SLICE], preferred_element_type=F32)
            hmid = (g * jax.nn.sigmoid(g) * u).astype(BF16)
            part = jnp.dot(hmid, wdn_b[c0:c0 + FFN_SLICE, :], preferred_element_type=F32)
            y = part if y is None else y + part
        o_ref[...] = y

    @pl.when(b >= nused_ref[0])
    def _():
        o_ref[...] = jnp.zeros_like(o_ref)


def _expert_call(blk_e, n_used, xs, wgu, wdn, bm, layer):
    n_blocks = xs.shape[0] // (bm * SLAB)
    d = wdn.shape[-1]

    def x_block(b, be, nu):
        return (jnp.minimum(b, jnp.maximum(nu[0] - 1, 0)), 0)

    grid_spec = pltpu.PrefetchScalarGridSpec(
        num_scalar_prefetch=2,
        grid=(n_blocks,),
        in_specs=[pl.BlockSpec((bm * SLAB, LANES), x_block),
                  pl.BlockSpec((1, 1, d, 2 * D_EXPERT), lambda b, be, nu: (layer, be[b], 0, 0)),
                  pl.BlockSpec((1, 1, D_EXPERT, d), lambda b, be, nu: (layer, be[b], 0, 0))],
        out_specs=pl.BlockSpec((bm, d), lambda b, be, nu: (b, 0)),
        scratch_shapes=[pltpu.VMEM((d, 2 * D_EXPERT), BF16), pltpu.VMEM((D_EXPERT, d), BF16)],
    )
    return pl.pallas_call(
        functools.partial(_expert_kernel, bm=bm),
        grid_spec=grid_spec,
        out_shape=jax.ShapeDtypeStruct((n_blocks * bm, d), F32),
        compiler_params=_cparams("arbitrary"),
        name="experts",
    )(blk_e, n_used, xs, wgu, wdn)


def _combine_kernel(dest_ref, dest_next_ref, gate_ref, x_ref, xb_ref, ys_hbm, wsg_ref, wsd_ref, g_ref, b_ref,
                    *rest, tm, n1, nsteps):
    if n1 is None:
        o_ref, ob_ref, buf0, buf1, base_ref, sem = rest
    else:
        o1_ref, o2_ref, buf0, buf1, base_ref, sem = rest
    i = pl.program_id(0)
    bufs = (buf0, buf1)

    def issue(ref, s):
        for r in range(tm):
            for k in range(TOP_K):
                pltpu.make_async_copy(ys_hbm.at[pl.ds(ref[0, k, r], 1)],
                                      bufs[s].at[pl.ds(k * tm + r, 1)], sem.at[s]).start(priority=k % 2)

    def wait(s):
        pltpu.make_async_copy(ys_hbm.at[pl.ds(0, TOP_K * tm)], bufs[s], sem.at[s]).wait()

    @pl.when(i == 0)
    def _():
        issue(dest_ref, 0)

    gu = jnp.dot(xb_ref[...], wsg_ref[...], preferred_element_type=F32)
    g = gu[:, :D_SHARED]
    u = gu[:, D_SHARED:]
    hmid = (g * jax.nn.sigmoid(g) * u).astype(BF16)
    base_ref[...] = ALPHA * x_ref[...] + jnp.dot(hmid, wsd_ref[...], preferred_element_type=F32)

    def step(cur):
        wait(cur)
        issue(dest_next_ref, 1 - cur)
        acc = gate_ref[:, 0:1] * bufs[cur][0:tm, :]
        for k in range(1, TOP_K):
            acc = acc + gate_ref[:, k:k + 1] * bufs[cur][k * tm:(k + 1) * tm, :]
        y = _layer_norm(base_ref[...] + acc, g_ref[...], b_ref[...], LN_EPS)
        if n1 is None:
            o_ref[...] = y
            ob_ref[...] = y.astype(BF16)
        else:
            @pl.when(i < n1)
            def _():
                o1_ref[...] = y

            @pl.when(i >= n1)
            def _():
                o2_ref[...] = y

    for cur in range(2):
        pl.when(lax.rem(i, 2) == cur)(functools.partial(step, cur))

    @pl.when(i == nsteps - 1)
    def _():
        wait(nsteps % 2)


def _combine_call(dest3, gate_tk, x, xb, ys, wsg, wsd, g, b, split_rows=None):
    t, d = x.shape
    tm = dest3.shape[2]
    nsteps = t // tm
    if split_rows is None:
        n1 = None
        out_specs = [pl.BlockSpec((tm, d), lambda i: (i, 0)), pl.BlockSpec((tm, d), lambda i: (i, 0))]
        out_shape = [jax.ShapeDtypeStruct((t, d), F32), jax.ShapeDtypeStruct((t, d), BF16)]
    else:
        n1 = split_rows // tm
        out_specs = [pl.BlockSpec((tm, d), lambda i: (jnp.minimum(i, n1 - 1), 0)),
                     pl.BlockSpec((tm, d), lambda i: (jnp.maximum(i - n1, 0), 0))]
        out_shape = [jax.ShapeDtypeStruct((split_rows, d), F32), jax.ShapeDtypeStruct((t - split_rows, d), F32)]
    return pl.pallas_call(
        functools.partial(_combine_kernel, tm=tm, n1=n1, nsteps=nsteps),
        grid=(nsteps,),
        in_specs=[pl.BlockSpec((1, TOP_K, tm), lambda i: (i, 0, 0), memory_space=pltpu.SMEM),
                  pl.BlockSpec((1, TOP_K, tm), lambda i: (jnp.minimum(i + 1, nsteps - 1), 0, 0),
                               memory_space=pltpu.SMEM),
                  pl.BlockSpec((tm, TOP_K), lambda i: (i, 0)),
                  pl.BlockSpec((tm, d), lambda i: (i, 0)),
                  pl.BlockSpec((tm, d), lambda i: (i, 0)),
                  pl.BlockSpec(memory_space=pl.ANY),
                  pl.BlockSpec((d, 2 * D_SHARED), lambda i: (0, 0)),
                  pl.BlockSpec((D_SHARED, d), lambda i: (0, 0)),
                  pl.BlockSpec((1, d), lambda i: (0, 0)),
                  pl.BlockSpec((1, d), lambda i: (0, 0))],
        out_specs=out_specs,
        out_shape=out_shape,
        scratch_shapes=[pltpu.VMEM((TOP_K * tm, d), F32), pltpu.VMEM((TOP_K * tm, d), F32),
                        pltpu.VMEM((tm, d), F32), pltpu.SemaphoreType.DMA((2,))],
        compiler_params=_cparams("arbitrary"),
        name="combine_ln",
    )(dest3, dest3, gate_tk, x, xb, ys, wsg, wsd, g.reshape(1, d), b.reshape(1, d))


def _dispatch_tables(ids, rank, counts_f, bm):
    k, t = ids.shape
    tk = k * t
    n_blocks = -(-tk // bm) + N_EXPERTS
    counts = counts_f[:, 0].astype(I32)
    padded = (counts + bm - 1) // bm * bm
    pends = jnp.cumsum(padded)
    pstart = pends - padded
    experts = jnp.arange(N_EXPERTS, dtype=I32)
    dest = jnp.sum(jnp.where(ids[:, :, None] == experts, pstart, 0), axis=-1) + rank
    blk_first = jnp.arange(n_blocks, dtype=I32) * bm
    blk_e = jnp.minimum(jnp.sum(pends[None, :] <= blk_first[:, None], axis=1), N_EXPERTS - 1).astype(I32)
    n_used = (pends[-1:] // bm).astype(I32)
    return dest, blk_e, n_used, (pstart + counts).astype(I32), pends.astype(I32), n_blocks * bm


def _layer(x, xb, memb, seq, layer, w_in, b_gate, conv_w, conv_b, mh_norm_g, pool_w, pool_scale, w_mem_kv, w_out,
           ln1_g, ln1_b, w_router, b_router, w_gu_all, w_dn_all, w_sh_gu, w_sh_dn, ln2_g, ln2_b, last):
    t = x.shape[0]
    hin = _mm_call(xb, w_in[:, :OFF_G].astype(BF16), F32, "in_proj")
    g_rows, g_cols = _gates_call(xb, _gate_rows(w_in[:, OFF_G:].T).astype(BF16), _gate_rows(b_gate))
    gr3 = g_rows.reshape(3 * GDIR, t // CHUNK, CHUNK).transpose(1, 0, 2)
    q = _conv_call(hin, conv_w, conv_b, seq, col0=OFF_Q, scale=1.0, transpose_out=False)
    kt = _conv_call(hin, conv_w, conv_b, seq, col0=OFF_K, scale=M_HEAD_DIM ** -0.5, transpose_out=True)
    h_f, h_b = _mlstm_call(q, kt, hin, gr3, g_cols, seq)
    m_out = _headnorm_call(h_f, h_b, hin, mh_norm_g)
    kv = _mm_call(memb, w_mem_kv.astype(BF16), BF16, "mem_kv").reshape(seq.nseq, -1, 2 * X_WIDTH)
    pa_out = _pool_attn_call(hin, kv, pool_w.astype(BF16), pool_scale, seq)
    x1, x1b, x1p = _outproj_call(m_out, pa_out, x, w_out.astype(BF16), ln1_g, ln1_b)
    ids, gate, rank, counts = _router_call(x1b, w_router.T.astype(BF16), b_router)
    bm = MOE_BM
    dest, blk_e, n_used, pad_lo, pad_hi, n_rows = _dispatch_tables(ids, rank, counts, bm)
    ctm = min(COMB_TM, t)
    dest3 = dest.reshape(TOP_K, t // ctm, ctm).transpose(1, 0, 2)
    dtm = min(DISP_TM, t)
    dest3d = dest.reshape(TOP_K, t // dtm, dtm).transpose(1, 0, 2)
    xs = _dispatch_call(pad_lo, pad_hi, dest3d, x1p.reshape(t, SLAB, LANES), n_rows)
    ys = _expert_call(blk_e, n_used, xs.reshape(n_rows * SLAB, LANES), w_gu_all, w_dn_all, bm, layer)
    return _combine_call(dest3, gate.T, x1, x1b, ys, w_sh_gu.astype(BF16), w_sh_dn.astype(BF16), ln2_g, ln2_b,
                         split_rows=seq.t1 if last else None)


def kernel(x_prompt, x_sample, mem_prompt, mem_sample, ln_in_g, ln_in_b, w_in, b_gate, conv_w, conv_b, mh_norm_g, pool_w, pool_scale, w_mem_kv, w_out, ln1_g, ln1_b, w_router, b_router, w_gu, w_dn, w_sh_gu, w_sh_dn, ln2_g, ln2_b):
    b1, s1, d = x_prompt.shape
    b2, s2, _ = x_sample.shape
    seq = SeqInfo(b1, s1, b2, s2)
    memb = jnp.concatenate([mem_prompt.reshape(-1, d), mem_sample.reshape(-1, d)], axis=0).astype(BF16)
    x, xb = _ln_call(x_prompt.reshape(b1 * s1, d), x_sample.reshape(b2 * s2, d), ln_in_g, ln_in_b)
    depth = w_in.shape[0]
    for l in range(depth):
        x, xb = _layer(x, xb, memb, seq, l, w_in[l], b_gate[l], conv_w[l], conv_b[l], mh_norm_g[l], pool_w[l],
                       pool_scale[l], w_mem_kv[l], w_out[l], ln1_g[l], ln1_b[l], w_router[l], b_router[l],
                       w_gu, w_dn, w_sh_gu[l], w_sh_dn[l], ln2_g[l], ln2_b[l], last=l == depth - 1)
    return (x.reshape(b1, s1, d), xb.reshape(b2, s2, d))
```
